```python
import jax, jax.numpy as jnp
from jax import lax
import numpy as np

D_MODEL = 1024
BATCH = 2
SEQ = 8192
DEPTH = 1
DEC_BATCH = 128
DEC_SEQ = 8
PAST_LEN = 2048
PAGE_SIZE = 128

HEAD_DIM = 64
NSA_HEADS = 8
NSA_KV_HEADS = 2
NSA_GROUP = NSA_HEADS // NSA_KV_HEADS
CMP_BLOCK = 32
CMP_STRIDE = 16
SLC_BLOCK = 64
SLC_TOPN = 16
WINDOW = 512
Q_BLOCK = 128
ROPE_THETA = 10000.0
HG_HEADS = 4
HG_DK = 128
HG_DV = 128
HG_CHUNK = 64
NSA_WIDTH = NSA_HEADS * HEAD_DIM
HG_WIDTH = HG_HEADS * HG_DV
MIX_WIDTH = NSA_WIDTH + HG_WIDTH
KV_WIDTH = NSA_KV_HEADS * HEAD_DIM
PROJ_SIZES = (NSA_WIDTH, 6 * KV_WIDTH, 3 * NSA_HEADS, HG_HEADS * HG_DK, HG_HEADS * HG_DK, HG_WIDTH, HG_WIDTH)
PROJ_WIDTH = sum(PROJ_SIZES)
PEER_HEADS = 8
PEER_KEYS = 128
PEER_EXPERTS = PEER_KEYS * PEER_KEYS
PEER_QDIM = 256
PEER_TOPK = 16
PEER_TOK_BLOCK = 128
RMS_EPS = 1e-6
NEG_INF = -1e30
FORCED_SCORE = 1e6

kernel_name = 'nsa_hgrn2_peer_hybrid_step'


def rms_norm(x, g):
    xf = x.astype(jnp.float32)
    y = xf * lax.rsqrt(jnp.mean(xf * xf, axis=-1, keepdims=True) + RMS_EPS)
    return (y * g.astype(jnp.float32)).astype(x.dtype)


def rope(x, pos):
    half = x.shape[-1] // 2
    inv = ROPE_THETA ** (-jnp.arange(half, dtype=jnp.float32) / half)
    ang = pos.astype(jnp.float32)[:, None] * inv[None, :]
    cos = jnp.cos(ang)[None, :, None, :]
    sin = jnp.sin(ang)[None, :, None, :]
    xf = x.astype(jnp.float32)
    x1, x2 = xf[..., :half], xf[..., half:]
    return jnp.concatenate([x1 * cos - x2 * sin, x2 * cos + x1 * sin], axis=-1).astype(x.dtype)


def masked_softmax(s, mask):
    p = jax.nn.softmax(jnp.where(mask, s.astype(jnp.float32), NEG_INF), axis=-1)
    return p * mask


def split_proj(h, w_in):
    b, t = h.shape[:2]
    parts = jnp.split(h @ w_in, [int(c) for c in np.cumsum(PROJ_SIZES)[:-1]], axis=-1)
    q = parts[0].reshape(b, t, NSA_HEADS, HEAD_DIM)
    kv = parts[1].reshape(b, t, 6, NSA_KV_HEADS, HEAD_DIM)
    gates = jax.nn.sigmoid(parts[2].astype(jnp.float32)).reshape(b, t, 3, NSA_HEADS).astype(h.dtype)
    hq = parts[3].reshape(b, t, HG_HEADS, HG_DK)
    hf = parts[4].reshape(b, t, HG_HEADS, HG_DK)
    hi = parts[5].reshape(b, t, HG_HEADS, HG_DV)
    return q, kv, gates, hq, hf, parts[5].reshape(b, t, HG_HEADS, HG_DV) if False else hi, parts[6]


def compress(k, w, pe):
    b, l, h, d = k.shape
    n16 = l // CMP_STRIDE
    ch = k[:, :n16 * CMP_STRIDE].reshape(b, n16, CMP_STRIDE, h, d)
    blocks = jnp.concatenate([ch[:, :-1], ch[:, 1:]], axis=2)
    return jnp.einsum('bnjhd,jde->bnhe', blocks + pe[None, None, :, None, :], w)


def cmp_end_positions(n_cmp):
    return jnp.arange(n_cmp, dtype=jnp.int32) * CMP_STRIDE + (CMP_BLOCK - 1)


def overlap_matrix(n_cmp, nblk):
    c0 = jnp.arange(n_cmp, dtype=jnp.int32)[:, None] * CMP_STRIDE
    s0 = jnp.arange(nblk, dtype=jnp.int32)[None, :] * SLC_BLOCK
    return ((c0 < s0 + SLC_BLOCK) & (c0 + CMP_BLOCK > s0)).astype(jnp.float32)


def to_blocks(k, nblk):
    b, l, h, d = k.shape
    k = jnp.pad(k, ((0, 0), (0, nblk * SLC_BLOCK - l), (0, 0), (0, 0)))
    return k.reshape(b, nblk, SLC_BLOCK, h, d).transpose(0, 3, 1, 2, 4)


def gather_blocks(kb, idx):
    return jax.vmap(jax.vmap(lambda blk, ix: blk[ix]))(kb, idx)


def nsa_cmp_slc(q_nope, q_rope, pos_q, kc, vc, cmp_end, overlap, kb, vb):
    b, tq = q_nope.shape[:2]
    scale = HEAD_DIM ** -0.5
    qn = q_nope.reshape(b, tq, NSA_KV_HEADS, NSA_GROUP, HEAD_DIM)
    s = jnp.einsum('btgqd,bngd->btgqn', qn, kc) * scale
    mask_c = (cmp_end[None, :] <= pos_q[:, None])[None, :, None, None, :]
    p = masked_softmax(s, mask_c)
    o_cmp = jnp.einsum('btgqn,bngd->btgqd', p.astype(vc.dtype), vc).reshape(b, tq, NSA_HEADS, HEAD_DIM)
    imp = jnp.einsum('btgqn,nj->btgj', p, overlap)
    nblk = overlap.shape[1]
    blk = jnp.arange(nblk, dtype=jnp.int32)[None, :]
    cur = (pos_q // SLC_BLOCK)[:, None]
    forced = (blk == 0) | (blk == cur) | (blk == cur - 1)
    causal = blk * SLC_BLOCK <= pos_q[:, None]
    score = jnp.where(causal[None, :, None, :], jnp.where(forced[None, :, None, :], FORCED_SCORE, imp), -FORCED_SCORE)
    n_sel = min(SLC_TOPN, nblk)
    _, idx = lax.top_k(score, n_sel)
    idx = idx.transpose(0, 2, 1, 3)
    ks = gather_blocks(kb, idx)
    vs = gather_blocks(vb, idx)
    qr = q_rope.reshape(b, tq, NSA_KV_HEADS, NSA_GROUP, HEAD_DIM).transpose(0, 2, 1, 3, 4)
    s2 = jnp.einsum('bgtqd,bgtnkd->bgtqnk', qr, ks) * scale
    s2 = s2.reshape(b, NSA_KV_HEADS, tq, NSA_GROUP, n_sel * SLC_BLOCK)
    kpos = idx[..., None] * SLC_BLOCK + jnp.arange(SLC_BLOCK, dtype=jnp.int32)
    m2 = (kpos <= pos_q[None, None, :, None, None]).reshape(b, NSA_KV_HEADS, tq, 1, n_sel * SLC_BLOCK)
    p2 = masked_softmax(s2, m2)
    o_slc = jnp.einsum('bgtqm,bgtmd->bgtqd', p2.astype(vs.dtype),
                       vs.reshape(b, NSA_KV_HEADS, tq, n_sel * SLC_BLOCK, HEAD_DIM))
    o_slc = o_slc.transpose(0, 2, 1, 3, 4).reshape(b, tq, NSA_HEADS, HEAD_DIM)
    return o_cmp, o_slc


def window_attn(q_rope, pos_q, k, v, pos_k):
    b, tq = q_rope.shape[:2]
    qg = q_rope.reshape(b, tq, NSA_KV_HEADS, NSA_GROUP, HEAD_DIM)
    s = jnp.einsum('btgqd,bsgd->btgqs', qg, k) * (HEAD_DIM ** -0.5)
    pk, pq = pos_k[None, :], pos_q[:, None]
    mask = (pk <= pq) & (pk >= pq - WINDOW) & (pk >= 0)
    p = masked_softmax(s, mask[None, :, None, None, :])
    return jnp.einsum('btgqs,bsgd->btgqd', p.astype(v.dtype), v).reshape(b, tq, NSA_HEADS, HEAD_DIM)


def combine_branches(g, o_c, o_s, o_w):
    return g[:, :, 0, :, None] * o_c + g[:, :, 1, :, None] * o_s + g[:, :, 2, :, None] * o_w


def nsa_prompt(q, kv, gates, cmp_wk, cmp_wv, cmp_pek, cmp_pev):
    b, t = q.shape[:2]
    pos = jnp.arange(t, dtype=jnp.int32)
    kcr, vcr, ks, vs, kw, vw = [kv[:, :, j] for j in range(6)]
    q_rope = rope(q, pos)
    ks = rope(ks, pos)
    kw = rope(kw, pos)
    kc = compress(kcr, cmp_wk, cmp_pek)
    vc = compress(vcr, cmp_wv, cmp_pev)
    n_cmp = kc.shape[1]
    cmp_end = cmp_end_positions(n_cmp)
    nblk = -(-t // SLC_BLOCK)
    ov = overlap_matrix(n_cmp, nblk)
    kb, vb = to_blocks(ks, nblk), to_blocks(vs, nblk)
    pad = ((0, 0), (WINDOW, 0), (0, 0), (0, 0))
    kw_pad, vw_pad = jnp.pad(kw, pad), jnp.pad(vw, pad)
    band = Q_BLOCK + WINDOW

    def one_block(t0):
        pos_q = t0 + jnp.arange(Q_BLOCK, dtype=jnp.int32)
        sl = lambda a, n: lax.dynamic_slice_in_dim(a, t0, n, axis=1)
        qr = sl(q_rope, Q_BLOCK)
        o_c, o_s = nsa_cmp_slc(sl(q, Q_BLOCK), qr, pos_q, kc, vc, cmp_end, ov, kb, vb)
        pos_k = t0 - WINDOW + jnp.arange(band, dtype=jnp.int32)
        o_w = window_attn(qr, pos_q, sl(kw_pad, band), sl(vw_pad, band), pos_k)
        return combine_branches(sl(gates, Q_BLOCK), o_c, o_s, o_w)

    out = lax.map(one_block, jnp.arange(0, t, Q_BLOCK, dtype=jnp.int32))
    out = jnp.moveaxis(out, 0, 1).reshape(b, t, NSA_WIDTH)
    keep = min(WINDOW, t)
    return out, jnp.stack([kcr, vcr], 2), jnp.stack([ks, vs], 2), jnp.stack([kw, vw], 2)[:, t - keep:]


def nsa_sample(q, kv, gates, cache_cmp, cache_slc, cache_win, page_table, cmp_wk, cmp_wv, cmp_pek, cmp_pev):
    db, tq = q.shape[:2]
    past = page_table.shape[1] * cache_cmp.shape[1]
    pos = past + jnp.arange(tq, dtype=jnp.int32)
    kcr, vcr, ks, vs, kw, vw = [kv[:, :, j] for j in range(6)]
    q_rope = rope(q, pos)
    ks = rope(ks, pos)
    kw = rope(kw, pos)

    def past_rows(cache):
        return cache[page_table].reshape(db, past, 2, NSA_KV_HEADS, HEAD_DIM)

    pc, ps = past_rows(cache_cmp), past_rows(cache_slc)
    kc = compress(jnp.concatenate([pc[:, :, 0], kcr], axis=1), cmp_wk, cmp_pek)
    vc = compress(jnp.concatenate([pc[:, :, 1], vcr], axis=1), cmp_wv, cmp_pev)
    n_cmp = kc.shape[1]
    nblk = -(-(past + tq) // SLC_BLOCK)
    kb = to_blocks(jnp.concatenate([ps[:, :, 0], ks], axis=1), nblk)
    vb = to_blocks(jnp.concatenate([ps[:, :, 1], vs], axis=1), nblk)
    o_c, o_s = nsa_cmp_slc(q, q_rope, pos, kc, vc, cmp_end_positions(n_cmp), overlap_matrix(n_cmp, nblk), kb, vb)
    wb = cache_win.shape[1]
    win = jnp.concatenate([cache_win, jnp.stack([kw, vw], 2)], axis=1)
    pos_k = past - wb + jnp.arange(wb + tq, dtype=jnp.int32)
    o_w = window_attn(q_rope, pos, win[:, :, 0], win[:, :, 1], pos_k)
    out = combine_branches(gates, o_c, o_s, o_w).reshape(db, tq, NSA_WIDTH)
    keep = min(WINDOW, wb + tq)
    return out, jnp.stack([kcr, vcr], 2), jnp.stack([ks, vs], 2), win[:, wb + tq - keep:]


def hgrn2_chunk(s0, q, k, v, logf):
    c = q.shape[1]
    cum = jnp.cumsum(logf, axis=1)
    o_inter = jnp.einsum('bthk,bhkv->bthv', q * jnp.exp(cum), s0)
    tri = (jnp.arange(c)[:, None] >= jnp.arange(c)[None, :])[None, :, :, None, None]
    decay = jnp.exp(jnp.where(tri, cum[:, :, None] - cum[:, None, :], -jnp.inf))
    a = jnp.einsum('bthk,btshk,bshk->btsh', q, decay, k)
    o_intra = jnp.einsum('btsh,bshv->bthv', a, v)
    s_new = jnp.exp(cum[:, -1])[..., None] * s0 + jnp.einsum('bshk,bshv->bhkv', k * jnp.exp(cum[:, -1:] - cum), v)
    return s_new, o_inter + o_intra


def hgrn2_mixer(hq, hf, hi, hg, lb, norm_g, s0):
    f32 = jnp.float32
    b, t = hq.shape[:2]
    lbh = lb.astype(f32).reshape(HG_HEADS, HG_DK)
    f = lbh + (1.0 - lbh) * jax.nn.sigmoid(hf.astype(f32))
    logf = jnp.log(f)
    k = 1.0 - f
    chunk = HG_CHUNK if t % HG_CHUNK == 0 else t
    n = t // chunk
    to_c = lambda a: a.reshape(b, n, chunk, *a.shape[2:]).swapaxes(0, 1)
    step = lambda s, xs: hgrn2_chunk(s, *xs)
    s_fin, o = lax.scan(step, s0.astype(f32), (to_c(hq.astype(f32)), to_c(k), to_c(hi.astype(f32)), to_c(logf)))
    o = o.swapaxes(0, 1).reshape(b, t, HG_HEADS, HG_DV)
    o = o * lax.rsqrt(jnp.mean(o * o, axis=-1, keepdims=True) + RMS_EPS) * norm_g.astype(f32).reshape(HG_HEADS, HG_DV)
    o = o.reshape(b, t, HG_WIDTH) * jax.nn.sigmoid(hg.astype(f32))
    return o.astype(hq.dtype), s_fin.astype(s0.dtype)


def peer(x, wq, sub_keys, u, v):
    b, t, d = x.shape
    n = b * t
    nb = -(-n // PEER_TOK_BLOCK)
    flat = jnp.pad(x.reshape(n, d), ((0, nb * PEER_TOK_BLOCK - n), (0, 0))).reshape(nb, PEER_TOK_BLOCK, d)

    def blk(xb):
        qh = (xb @ wq).reshape(PEER_TOK_BLOCK, PEER_HEADS, 2, PEER_QDIM // 2)
        s = jnp.einsum('nhcd,hckd->nhck', qh, sub_keys).astype(jnp.float32)
        s1, i1 = lax.top_k(s[:, :, 0], PEER_TOPK)
        s2, i2 = lax.top_k(s[:, :, 1], PEER_TOPK)
        cand = (s1[..., :, None] + s2[..., None, :]).reshape(PEER_TOK_BLOCK, PEER_HEADS, PEER_TOPK * PEER_TOPK)
        sc, ci = lax.top_k(cand, PEER_TOPK)
        e = (jnp.take_along_axis(i1, ci // PEER_TOPK, axis=-1) * PEER_KEYS
             + jnp.take_along_axis(i2, ci % PEER_TOPK, axis=-1))
        gate = jax.nn.softmax(sc, axis=-1)
        act = jax.nn.gelu(jnp.einsum('nd,nhkd->nhk', xb, u[e]).astype(jnp.float32)) * gate
        return jnp.einsum('nhk,nhkd->nd', act.astype(v.dtype), v[e])

    y = lax.map(blk, flat).reshape(nb * PEER_TOK_BLOCK, d)[:n]
    return y.reshape(b, t, d).astype(x.dtype)


def trunk_layer(x, nsa_fn, s0, norm_mix, w_in, lb, hg_norm, w_out, norm_ffn, peer_wq, peer_keys, peer_u, peer_v):
    h = rms_norm(x, norm_mix)
    q, kv, gates, hq, hf, hi, hg = split_proj(h, w_in)
    o_nsa, new_c, new_s, new_w = nsa_fn(q, kv, gates)
    o_hg, s_new = hgrn2_mixer(hq, hf, hi, hg, lb, hg_norm, s0)
    x = x + jnp.concatenate([o_nsa.astype(x.dtype), o_hg], axis=-1) @ w_out
    x = x + peer(rms_norm(x, norm_ffn), peer_wq, peer_keys, peer_u, peer_v)
    return x, (new_c, new_s, new_w, s_new)


def setup_inputs(seed: int = 0) -> dict:
    key = jax.random.key(seed)
    ks = jax.random.split(key, 24)
    f32 = jnp.float32
    nrm = lambda k, shape, scale: jax.random.normal(k, shape, f32) * scale
    n_pages = PAST_LEN // PAGE_SIZE
    n_used = DEC_BATCH * n_pages
    n_pool = n_used + max(1, n_used // 4)
    wb = min(WINDOW, PAST_LEN)
    page_table = jax.random.permutation(ks[0], n_pool)[:n_used].reshape(DEC_BATCH, n_pages).astype(jnp.int32)
    kv_cache_shape = (DEPTH, n_pool, PAGE_SIZE, 2, NSA_KV_HEADS, HEAD_DIM)
    return {
        'x_prompt': nrm(ks[1], (BATCH, SEQ, D_MODEL), 1.0),
        'x_sample': nrm(ks[2], (DEC_BATCH, DEC_SEQ, D_MODEL), 1.0),
        'cache_cmp': nrm(ks[3], kv_cache_shape, 1.0),
        'cache_slc': nrm(ks[4], kv_cache_shape, 1.0),
        'cache_win': nrm(ks[5], (DEPTH, DEC_BATCH, wb, 2, NSA_KV_HEADS, HEAD_DIM), 1.0),
        'state_hgrn': nrm(ks[6], (DEPTH, DEC_BATCH, HG_HEADS, HG_DK, HG_DV), 0.3),
        'page_table': page_table,
        'norm_mix': 1.0 + nrm(ks[7], (DEPTH, D_MODEL), 0.01),
        'w_in': nrm(ks[8], (DEPTH, D_MODEL, PROJ_WIDTH), D_MODEL ** -0.5),
        'cmp_wk': nrm(ks[9], (DEPTH, CMP_BLOCK, HEAD_DIM, HEAD_DIM), (CMP_BLOCK * HEAD_DIM) ** -0.5),
        'cmp_wv': nrm(ks[10], (DEPTH, CMP_BLOCK, HEAD_DIM, HEAD_DIM), (CMP_BLOCK * HEAD_DIM) ** -0.5),
        'cmp_pek': nrm(ks[11], (DEPTH, CMP_BLOCK, HEAD_DIM), 0.02),
        'cmp_pev': nrm(ks[12], (DEPTH, CMP_BLOCK, HEAD_DIM), 0.02),
        'hg_lb': nrm(ks[13], (DEPTH + 1, HG_HEADS * HG_DK), 0.5),
        'hg_norm': 1.0 + nrm(ks[14], (DEPTH, HG_WIDTH), 0.01),
        'w_out': nrm(ks[15], (DEPTH, MIX_WIDTH, D_MODEL), MIX_WIDTH ** -0.5),
        'norm_ffn': 1.0 + nrm(ks[16], (DEPTH, D_MODEL), 0.01),
        'peer_wq': nrm(ks[17], (DEPTH, D_MODEL, PEER_HEADS * PEER_QDIM), D_MODEL ** -0.5),
        'peer_keys': nrm(ks[18], (DEPTH, PEER_HEADS, 2, PEER_KEYS, PEER_QDIM // 2), (PEER_QDIM // 2) ** -0.5),
        'peer_u': nrm(ks[19], (DEPTH, PEER_EXPERTS, D_MODEL), D_MODEL ** -0.5),
        'peer_v': nrm(ks[20], (DEPTH, PEER_EXPERTS, D_MODEL), PEER_HEADS ** -0.5),
        'norm_final': 1.0 + nrm(ks[21], (D_MODEL,), 0.01),
    }


def reference(x_prompt, x_sample, cache_cmp, cache_slc, cache_win, state_hgrn, page_table,
              norm_mix, w_in, cmp_wk, cmp_wv, cmp_pek, cmp_pev, hg_lb, hg_norm, w_out,
              norm_ffn, peer_wq, peer_keys, peer_u, peer_v, norm_final):
    lb_all = jnp.cumsum(jax.nn.softmax(hg_lb.astype(jnp.float32), axis=0), axis=0)
    yp, ys = x_prompt, x_sample
    st_p, st_s = [], []
    for l in range(DEPTH):
        shared = (norm_mix[l], w_in[l], lb_all[l], hg_norm[l], w_out[l], norm_ffn[l],
                  peer_wq[l], peer_keys[l], peer_u[l], peer_v[l])
        fp = lambda q, kv, g: nsa_prompt(q, kv, g, cmp_wk[l], cmp_wv[l], cmp_pek[l], cmp_pev[l])
        s0 = jnp.zeros((yp.shape[0], HG_HEADS, HG_DK, HG_DV), yp.dtype)
        yp, new_p = trunk_layer(yp, fp, s0, *shared)
        fs = lambda q, kv, g: nsa_sample(q, kv, g, cache_cmp[l], cache_slc[l], cache_win[l], page_table,
                                         cmp_wk[l], cmp_wv[l], cmp_pek[l], cmp_pev[l])
        ys, new_s = trunk_layer(ys, fs, state_hgrn[l], *shared)
        st_p.append(new_p)
        st_s.append(new_s)
    y_prompt = rms_norm(yp, norm_final)
    y_sample = rms_norm(ys, norm_final)
    new_cmp_prompt = jnp.stack([s[0] for s in st_p])
    new_slc_prompt = jnp.stack([s[1] for s in st_p])
    new_win_prompt = jnp.stack([s[2] for s in st_p])
    new_hgrn_prompt = jnp.stack([s[3] for s in st_p])
    new_cmp_sample = jnp.stack([s[0] for s in st_s])
    new_slc_sample = jnp.stack([s[1] for s in st_s])
    new_win_sample = jnp.stack([s[2] for s in st_s])
    new_hgrn_sample = jnp.stack([s[3] for s in st_s])
    return (y_prompt, y_sample, new_cmp_prompt, new_slc_prompt, new_win_prompt, new_hgrn_prompt,
            new_cmp_sample, new_slc_sample, new_win_sample, new_hgrn_sample)
```

```python
import functools

import numpy as np
import jax
import jax.numpy as jnp
from jax import lax
from jax.experimental import pallas as pl
from jax.experimental.pallas import tpu as pltpu

F32 = jnp.float32
BF16 = jnp.bfloat16

D_MODEL = 1024
HEAD_DIM = 64
NSA_HEADS = 8
NSA_KV_HEADS = 2
NSA_GROUP = NSA_HEADS // NSA_KV_HEADS
CMP_BLOCK = 32
CMP_STRIDE = 16
SLC_BLOCK = 64
SLC_TOPN = 16
WINDOW = 512
ROPE_THETA = 10000.0
HG_HEADS = 4
HG_DK = 128
HG_DV = 128
NSA_WIDTH = NSA_HEADS * HEAD_DIM
HG_WIDTH = HG_HEADS * HG_DV
KV_WIDTH = NSA_KV_HEADS * HEAD_DIM
PEER_HEADS = 8
PEER_KEYS = 128
PEER_QDIM = 256
PEER_TOPK = 16
RMS_EPS = 1e-6
NEG_INF = -1e30
FORCED_SCORE = 1e6
LOWEST = -3e38

LANES = 128
HALF = 64
VMEM_LIMIT = 56 * 1024 * 1024

COL_Q = 0
COL_KV = 512
COL_H = 1280
COL_G = 3328
PROJ_PAD = 3456


def _dot(a, b):
    return jnp.dot(a, b, preferred_element_type=F32)


def _dot_nt(a, b):
    return lax.dot_general(a, b, (((1,), (1,)), ((), ())), preferred_element_type=F32)


def _iota(shape, dim):
    return lax.broadcasted_iota(jnp.int32, shape, dim)


def _params(*sem):
    return pltpu.CompilerParams(dimension_semantics=sem, vmem_limit_bytes=VMEM_LIMIT)


def _rope_tile(x, cos, sin_signed, first_half):
    partner = jnp.where(first_half, pltpu.roll(x, LANES - 32, 1), pltpu.roll(x, 32, 1))
    return x * cos + partner * sin_signed


def _inproj_kernel(x_ref, g_ref, w_ref, cos_ref, sin_ref,
                   q_ref, qr_ref, cmp_ref, slc_ref, win_ref, slcb_ref, winb_ref, gates_ref, h_ref):
    x = x_ref[...]
    ms = jnp.mean(x * x, axis=-1, keepdims=True)
    xn = (x * lax.rsqrt(ms + RMS_EPS)) * g_ref[...]
    proj = _dot(xn.astype(BF16), w_ref[...])
    cos = cos_ref[...]
    sin = sin_ref[...]
    first_half = (_iota(cos.shape, 1) & (HALF - 1)) < 32
    rope = lambda t: _rope_tile(t, cos, sin, first_half)
    q_ref[...] = proj[:, COL_Q:COL_Q + NSA_WIDTH]
    for c in range(NSA_WIDTH // LANES):
        qr_ref[:, c * LANES:(c + 1) * LANES] = rope(proj[:, COL_Q + c * LANES:COL_Q + (c + 1) * LANES])
    cmp_ref[...] = proj[:, COL_KV:COL_KV + 256]
    ks = rope(proj[:, COL_KV + 256:COL_KV + 384])
    vs = proj[:, COL_KV + 384:COL_KV + 512]
    kw = rope(proj[:, COL_KV + 512:COL_KV + 640])
    vw = proj[:, COL_KV + 640:COL_KV + 768]
    slc_ref[:, 0:LANES] = ks
    slc_ref[:, LANES:2 * LANES] = vs
    win_ref[:, 0:LANES] = kw
    win_ref[:, LANES:2 * LANES] = vw
    slcb_ref[:, 0:LANES] = ks.astype(BF16)
    slcb_ref[:, LANES:2 * LANES] = vs.astype(BF16)
    winb_ref[:, 0:LANES] = kw.astype(BF16)
    winb_ref[:, LANES:2 * LANES] = vw.astype(BF16)
    gates_ref[...] = jax.nn.sigmoid(proj[:, COL_G:COL_G + LANES])
    h_ref[...] = proj[:, COL_H:COL_H + 4 * HG_WIDTH]


def _inproj(x, norm_g, w_all, cos_t, sin_t, tm):
    n = x.shape[0]
    ntab = cos_t.shape[0] // tm
    row = lambda w: pl.BlockSpec((tm, w), lambda i: (i, 0))
    tab = pl.BlockSpec((tm, LANES), lambda i: (i % ntab, 0))
    out_shapes = [
        jax.ShapeDtypeStruct((n, NSA_WIDTH), F32), jax.ShapeDtypeStruct((n, NSA_WIDTH), F32),
        jax.ShapeDtypeStruct((n, 256), F32), jax.ShapeDtypeStruct((n, 256), F32),
        jax.ShapeDtypeStruct((n, 256), F32), jax.ShapeDtypeStruct((n, 256), BF16),
        jax.ShapeDtypeStruct((n, 256), BF16), jax.ShapeDtypeStruct((n, LANES), F32),
        jax.ShapeDtypeStruct((n, 4 * HG_WIDTH), F32),
    ]
    return pl.pallas_call(
        _inproj_kernel,
        grid=(n // tm,),
        in_specs=[row(D_MODEL), pl.BlockSpec((1, D_MODEL), lambda i: (0, 0)),
                  pl.BlockSpec((D_MODEL, PROJ_PAD), lambda i: (0, 0)), tab, tab],
        out_specs=[row(NSA_WIDTH), row(NSA_WIDTH), row(256), row(256), row(256), row(256), row(256),
                   row(LANES), row(4 * HG_WIDTH)],
        out_shape=out_shapes,
        compiler_params=_params("parallel"),
        name="inproj",
    )(x, norm_g, w_all, cos_t, sin_t)


def _rope_tables(pos):
    half = HEAD_DIM // 2
    inv = ROPE_THETA ** (-jnp.arange(half, dtype=F32) / half)
    ang = pos.astype(F32)[:, None] * inv[None, :]
    cos = jnp.tile(jnp.cos(ang), (1, 4))
    sin = jnp.sin(ang)
    return cos, jnp.tile(jnp.concatenate([-sin, sin], axis=1), (1, 2))


def _compress_rows(rows_ref, n_out, w_ref, pe_ref):
    a = jnp.zeros((n_out, LANES), F32)
    b = jnp.zeros((n_out, LANES), F32)
    for j in range(CMP_STRIDE):
        xj = rows_ref[pl.ds(j, n_out, stride=CMP_STRIDE), :]
        a = a + _dot((xj + pe_ref[j:j + 1, :]).astype(BF16), w_ref[j])
        b = b + _dot((xj + pe_ref[CMP_STRIDE + j:CMP_STRIDE + j + 1, :]).astype(BF16), w_ref[CMP_STRIDE + j])
    return a + pltpu.roll(b, n_out - 1, 0)


def _compress_prompt_kernel(rk_ref, rv_ref, wk_ref, wv_ref, pek_ref, pev_ref, out_ref):
    n_out = out_ref.shape[0]
    out_ref[:, 0:LANES] = _compress_rows(rk_ref, n_out, wk_ref, pek_ref).astype(BF16)
    out_ref[:, LANES:2 * LANES] = _compress_rows(rv_ref, n_out, wv_ref, pev_ref).astype(BF16)


def _compress_prompt(rows, cw):
    b, t, _ = rows.shape
    n_out = t // CMP_STRIDE
    wspec = pl.BlockSpec((CMP_BLOCK, LANES, LANES), lambda i: (0, 0, 0))
    pspec = pl.BlockSpec((CMP_BLOCK, LANES), lambda i: (0, 0))
    return pl.pallas_call(
        _compress_prompt_kernel,
        grid=(b,),
        in_specs=[pl.BlockSpec((None, t, LANES), lambda i: (i, 0, 0)),
                  pl.BlockSpec((None, t, LANES), lambda i: (i, 0, 1)), wspec, wspec, pspec, pspec],
        out_specs=pl.BlockSpec((None, n_out, 256), lambda i: (i, 0, 0)),
        out_shape=jax.ShapeDtypeStruct((b, n_out, 256), BF16),
        compiler_params=_params("parallel"),
        name="compress_prompt",
    )(rows, rows, *cw)


def _stack_heads(ref, g, half_g, nq):
    parts = []
    for qh in range(NSA_GROUP):
        c0 = 256 * g + LANES * (qh // 2)
        tile = ref[:, c0:c0 + LANES]
        if qh % 2 != g:
            tile = pltpu.roll(tile, HALF, 1)
        parts.append(jnp.where(half_g, tile, 0.0))
    return (jnp.concatenate(parts, axis=0) * (HEAD_DIM ** -0.5)).astype(BF16)


def _masked_softmax_rows(s, valid):
    sm = jnp.where(valid, s, NEG_INF)
    m = jnp.max(sm, axis=-1, keepdims=True)
    e = jnp.where(valid, jnp.exp(sm - m), 0.0)
    l = jnp.sum(e, axis=-1, keepdims=True)
    return e / jnp.where(l > 0.0, l, 1.0)


def _select_blocks(imp, pos_col, nq):
    blk = _iota((nq, LANES), 1)
    cur = pos_col >> 6
    forced = (blk == 0) | (blk == cur) | (blk == cur - 1)
    causal = (blk << 6) <= pos_col
    score = jnp.where(causal, jnp.where(forced, FORCED_SCORE, imp), -FORCED_SCORE)
    return score, causal


def _topk_mask_t(score_t, n_sel):
    bi = _iota(score_t.shape, 0)
    sel = jnp.zeros(score_t.shape, F32)
    sc = score_t
    for _ in range(n_sel):
        m = jnp.max(sc, axis=0, keepdims=True)
        idx = jnp.min(jnp.where(sc == m, bi, LANES), axis=0, keepdims=True)
        hit = bi == idx
        sel = jnp.where(hit, 1.0, sel)
        sc = jnp.where(hit, LOWEST, sc)
    return sel


def _cmp_and_select(qn, kc, vc, ov, pos_rows, pos_col, nq):
    s = _dot_nt(qn, kc)
    n_i = _iota(s.shape, 1)
    valid = (n_i * CMP_STRIDE + (CMP_BLOCK - 1)) <= pos_rows
    p = _masked_softmax_rows(s, valid)
    o_c = _dot(p.astype(BF16), vc)
    psum = p[0:nq] + p[nq:2 * nq] + p[2 * nq:3 * nq] + p[3 * nq:4 * nq]
    hi = psum.astype(BF16)
    lo = (psum - hi.astype(F32)).astype(BF16)
    imp = _dot(hi, ov) + _dot(lo, ov)
    score, causal = _select_blocks(imp, pos_col, nq)
    return o_c, score, causal


def _combine_group(out_ref, gates, g, lo_half, o_c, o_s, o_w, nq):
    combs = []
    for qh in range(NSA_GROUP):
        h = NSA_GROUP * g + qh
        rows = slice(qh * nq, (qh + 1) * nq)
        comb = (gates[:, h:h + 1] * o_c[rows] + gates[:, NSA_HEADS + h:NSA_HEADS + h + 1] * o_s[rows]
                + gates[:, 2 * NSA_HEADS + h:2 * NSA_HEADS + h + 1] * o_w[rows])
        if qh % 2 != g:
            comb = pltpu.roll(comb, HALF, 1)
        combs.append(comb)
    for r in range(2):
        out_ref[:, 256 * g + LANES * r:256 * g + LANES * (r + 1)] = jnp.where(lo_half, combs[2 * r], combs[2 * r + 1])


KEY_TILE = 256


def _nsa_prompt_kernel(q_ref, qr_ref, gates_ref, cmp_ref, slc_ref, win_ref, ov_ref, et_ref, out_ref, *, tq):
    t0 = pl.program_id(1) * tq
    rows = NSA_GROUP * tq
    lo_half = _iota((tq, LANES), 1) < HALF
    pos_rows = t0 + (_iota((rows, 1), 0) & (tq - 1))
    pos_col = t0 + _iota((tq, 1), 0)
    gates = gates_ref[...]
    ov = ov_ref[...]
    n_tiles = (t0 + tq + KEY_TILE - 1) // KEY_TILE
    w_start = pl.multiple_of(jnp.maximum(t0 - WINDOW, 0), LANES)
    w_len = WINDOW + tq

    for g in range(NSA_KV_HEADS):
        half_g = lo_half if g == 0 else jnp.logical_not(lo_half)
        qn = _stack_heads(q_ref, g, half_g, tq)
        qr = _stack_heads(qr_ref, g, half_g, tq)

        o_c, score, causal = _cmp_and_select(qn, cmp_ref[:, 0:LANES], cmp_ref[:, LANES:2 * LANES], ov,
                                             pos_rows, pos_col, tq)
        sel_t = _topk_mask_t(score.T, SLC_TOPN)
        sel = jnp.where(causal, sel_t.T, 0.0)
        notsel = (1.0 - sel).astype(BF16)

        def tile_step(kt, carry, diag):
            m, l, acc = carry
            k0 = pl.multiple_of(kt * KEY_TILE, KEY_TILE)
            s = _dot_nt(qr, slc_ref[pl.ds(k0, KEY_TILE), 0:LANES])
            negb = _dot_nt(notsel, et_ref[pl.ds(k0, KEY_TILE), :]) * NEG_INF
            s = s + jnp.concatenate([negb] * NSA_GROUP, axis=0)
            if diag:
                kp = k0 + _iota(s.shape, 1)
                s = jnp.where(kp <= pos_rows, s, NEG_INF)
            m_new = jnp.maximum(m, jnp.max(s, axis=-1, keepdims=True))
            alpha = jnp.exp(m - m_new)
            p = jnp.exp(s - m_new)
            l = alpha * l + jnp.sum(p, axis=-1, keepdims=True)
            acc = alpha * acc + _dot(p.astype(BF16), slc_ref[pl.ds(k0, KEY_TILE), LANES:2 * LANES])
            return m_new, l, acc

        init = (jnp.full((rows, 1), NEG_INF, F32), jnp.zeros((rows, 1), F32), jnp.zeros((rows, LANES), F32))
        carry = lax.fori_loop(0, n_tiles - 1, lambda kt, c: tile_step(kt, c, False), init)
        _, l, acc = tile_step(n_tiles - 1, carry, True)
        o_s = acc / l

        sw = _dot_nt(qr, win_ref[pl.ds(w_start, w_len), 0:LANES])
        kp = w_start + _iota(sw.shape, 1)
        pw = _masked_softmax_rows(sw, (kp <= pos_rows) & (kp >= pos_rows - WINDOW))
        o_w = _dot(pw.astype(BF16), win_ref[pl.ds(w_start, w_len), LANES:2 * LANES])

        _combine_group(out_ref, gates, g, lo_half, o_c, o_s, o_w, tq)


def _nsa_prompt(q, qr, gates, kcvc, slcb, winb, ov, et, tq):
    b, t, _ = q.shape
    ncp = kcvc.shape[1]
    blk = lambda w: pl.BlockSpec((None, tq, w), lambda bi, i: (bi, i, 0))
    full = lambda r, w: pl.BlockSpec((None, r, w), lambda bi, i: (bi, 0, 0))
    return pl.pallas_call(
        functools.partial(_nsa_prompt_kernel, tq=tq),
        grid=(b, t // tq),
        in_specs=[blk(NSA_WIDTH), blk(NSA_WIDTH), blk(LANES), full(ncp, 256), full(t, 256), full(t, 256),
                  pl.BlockSpec((ncp, LANES), lambda bi, i: (0, 0)),
                  pl.BlockSpec((t, LANES), lambda bi, i: (0, 0))],
        out_specs=blk(NSA_WIDTH),
        out_shape=jax.ShapeDtypeStruct((b, t, NSA_WIDTH), F32),
        compiler_params=_params("parallel", "arbitrary"),
        name="nsa_prompt",
    )(q, qr, gates, kcvc, slcb, winb, ov, et)


def _overlap_matrix(n_cmp_pad, n_cmp):
    c0 = jnp.arange(n_cmp_pad, dtype=jnp.int32)[:, None] * CMP_STRIDE
    s0 = jnp.arange(LANES, dtype=jnp.int32)[None, :] * SLC_BLOCK
    real = jnp.arange(n_cmp_pad, dtype=jnp.int32)[:, None] < n_cmp
    return ((c0 < s0 + SLC_BLOCK) & (c0 + CMP_BLOCK > s0) & real).astype(BF16)


def _block_expand_matrix(n_keys):
    r = jnp.arange(n_keys, dtype=jnp.int32)[:, None] // SLC_BLOCK
    return (r == jnp.arange(LANES, dtype=jnp.int32)[None, :]).astype(BF16)


PAGE_ROWS = 128


def _compress_paged_kernel(pt_ref, *refs, n_pages):
    pages = refs[:n_pages]
    wk_ref, wv_ref, pek_ref, pev_ref, out_ref, rk_scr, rv_scr = refs[n_pages:]
    for p in range(n_pages):
        rk_scr[p * PAGE_ROWS:(p + 1) * PAGE_ROWS, :] = pages[p][:, 0:LANES]
        rv_scr[p * PAGE_ROWS:(p + 1) * PAGE_ROWS, :] = pages[p][:, LANES:2 * LANES]
    n_out = out_ref.shape[0]
    out_ref[:, 0:LANES] = _compress_rows(rk_scr, n_out, wk_ref, pek_ref).astype(BF16)
    out_ref[:, LANES:2 * LANES] = _compress_rows(rv_scr, n_out, wv_ref, pev_ref).astype(BF16)


def _page_specs(n_pages):
    return [pl.BlockSpec((None, PAGE_ROWS, 256), functools.partial(lambda i, pt, p: (pt[i * n_pages + p], 0, 0), p=p))
            for p in range(n_pages)]


def _compress_paged(cache, page_table, cw):
    db, n_pages = page_table.shape
    past = n_pages * PAGE_ROWS
    n_out = past // CMP_STRIDE
    wspec = pl.BlockSpec((CMP_BLOCK, LANES, LANES), lambda i, pt: (0, 0, 0))
    pspec = pl.BlockSpec((CMP_BLOCK, LANES), lambda i, pt: (0, 0))
    grid_spec = pltpu.PrefetchScalarGridSpec(
        num_scalar_prefetch=1, grid=(db,),
        in_specs=_page_specs(n_pages) + [wspec, wspec, pspec, pspec],
        out_specs=pl.BlockSpec((None, n_out, 256), lambda i, pt: (i, 0, 0)),
        scratch_shapes=[pltpu.VMEM((past, LANES), F32), pltpu.VMEM((past, LANES), F32)])
    return pl.pallas_call(
        functools.partial(_compress_paged_kernel, n_pages=n_pages),
        grid_spec=grid_spec,
        out_shape=jax.ShapeDtypeStruct((db, n_out, 256), BF16),
        compiler_params=_params("arbitrary"),
        name="compress_paged",
    )(page_table.reshape(-1), *([cache] * n_pages), *cw)


KEY_PAD = 64


def _nsa_sample_kernel(pt_ref, q_ref, qr_ref, gates_ref, cmp_ref, *refs, n_pages):
    pages = refs[:n_pages]
    snew_ref, cwin_ref, wnew_ref, ov_ref, et_ref, out_ref, nwin_ref, slc_scr, win_scr = refs[n_pages:]
    nq = q_ref.shape[0]
    past = n_pages * PAGE_ROWS
    wb = cwin_ref.shape[0]
    rows = NSA_GROUP * nq
    for p in range(n_pages):
        slc_scr[p * PAGE_ROWS:(p + 1) * PAGE_ROWS, :] = pages[p][...]
    slc_scr[past:past + nq, :] = snew_ref[...]
    slc_scr[past + nq:past + KEY_PAD, :] = jnp.zeros((KEY_PAD - nq, 256), F32)
    win_scr[0:wb, :] = cwin_ref[...]
    win_scr[wb:wb + nq, :] = wnew_ref[...]
    win_scr[wb + nq:wb + KEY_PAD, :] = jnp.zeros((KEY_PAD - nq, 256), F32)
    nwin_ref[0:wb - nq, :] = cwin_ref[nq:wb, :]
    nwin_ref[wb - nq:wb, :] = wnew_ref[...]

    lo_half = _iota((nq, LANES), 1) < HALF
    pos_rows = past + (_iota((rows, 1), 0) & (nq - 1))
    pos_col = past + _iota((nq, 1), 0)
    gates = gates_ref[...]
    ov = ov_ref[...]
    et = et_ref[...]
    ks = slc_scr[:, 0:LANES].astype(BF16)
    vs = slc_scr[:, LANES:2 * LANES].astype(BF16)
    kw = win_scr[:, 0:LANES].astype(BF16)
    vw = win_scr[:, LANES:2 * LANES].astype(BF16)

    for g in range(NSA_KV_HEADS):
        half_g = lo_half if g == 0 else jnp.logical_not(lo_half)
        qn = _stack_heads(q_ref, g, half_g, nq)
        qr = _stack_heads(qr_ref, g, half_g, nq)
        o_c, score, causal = _cmp_and_select(qn, cmp_ref[:, 0:LANES], cmp_ref[:, LANES:2 * LANES], ov,
                                             pos_rows, pos_col, nq)
        score_sq = jnp.concatenate([score, jnp.full((LANES - nq, LANES), LOWEST, F32)], axis=0)
        sel_t = _topk_mask_t(score_sq.T, SLC_TOPN)
        sel = jnp.where(causal, sel_t.T[0:nq], 0.0)
        notsel = (1.0 - sel).astype(BF16)

        s = _dot_nt(qr, ks)
        negb = _dot_nt(notsel, et) * NEG_INF
        s = s + jnp.concatenate([negb] * NSA_GROUP, axis=0)
        s = jnp.where(_iota(s.shape, 1) <= pos_rows, s, NEG_INF)
        e = jnp.exp(s - jnp.max(s, axis=-1, keepdims=True))
        o_s = _dot(e.astype(BF16), vs) / jnp.sum(e, axis=-1, keepdims=True)

        sw = _dot_nt(qr, kw)
        kp = (past - wb) + _iota(sw.shape, 1)
        pw = _masked_softmax_rows(sw, (kp <= pos_rows) & (kp >= pos_rows - WINDOW) & (kp >= 0))
        o_w = _dot(pw.astype(BF16), vw)
        _combine_group(out_ref, gates, g, lo_half, o_c, o_s, o_w, nq)


def _nsa_sample(q, qr, gates, kcvc, cache_slc, slc_new, cache_win, win_new, page_table, ov, et, nq):
    db, n_pages = page_table.shape
    past = n_pages * PAGE_ROWS
    wb = cache_win.shape[1]
    ncp = kcvc.shape[1]
    blk = lambda w: pl.BlockSpec((nq, w), lambda i, pt: (i, 0))
    grid_spec = pltpu.PrefetchScalarGridSpec(
        num_scalar_prefetch=1, grid=(db,),
        in_specs=[blk(NSA_WIDTH), blk(NSA_WIDTH), blk(LANES),
                  pl.BlockSpec((None, ncp, 256), lambda i, pt: (i, 0, 0))] + _page_specs(n_pages) + [
                  blk(256), pl.BlockSpec((None, wb, 256), lambda i, pt: (i, 0, 0)), blk(256),
                  pl.BlockSpec((ncp, LANES), lambda i, pt: (0, 0)),
                  pl.BlockSpec((past + KEY_PAD, LANES), lambda i, pt: (0, 0))],
        out_specs=[blk(NSA_WIDTH), pl.BlockSpec((None, wb, 256), lambda i, pt: (i, 0, 0))],
        scratch_shapes=[pltpu.VMEM((past + KEY_PAD, 256), F32), pltpu.VMEM((wb + KEY_PAD, 256), F32)])
    return pl.pallas_call(
        functools.partial(_nsa_sample_kernel, n_pages=n_pages),
        grid_spec=grid_spec,
        out_shape=[jax.ShapeDtypeStruct((db * nq, NSA_WIDTH), F32), jax.ShapeDtypeStruct((db, wb, 256), F32)],
        compiler_params=_params("arbitrary"),
        name="nsa_sample",
    )(page_table.reshape(-1), q, qr, gates, kcvc, *([cache_slc] * n_pages), slc_new, cache_win, win_new, ov, et)


HG_ROWS = 128
HG_SUB = 16


def _dot_split3(m_bf16, x):
    a = x.astype(BF16)
    r = x - a.astype(F32)
    b = r.astype(BF16)
    c = (r - b.astype(F32)).astype(BF16)
    return _dot(m_bf16, a) + _dot(m_bf16, b) + _dot(m_bf16, c)


def _hgrn_chunk(q, fpre, v, gpre, lb, ng, s0, n_real):
    f = lb + (1.0 - lb) * jax.nn.sigmoid(fpre)
    logf = jnp.log(f)
    kk = 1.0 - f
    if n_real < HG_ROWS:
        pad = lambda a: jnp.concatenate([a, jnp.zeros((HG_ROWS - n_real, LANES), F32)], axis=0)
        q, logf, kk, v = pad(q), pad(logf), pad(kk), pad(v)
    r_i = _iota((HG_ROWS, HG_ROWS), 0)
    c_i = _iota((HG_ROWS, HG_ROWS), 1)
    tri_b = r_i >= c_i
    tri = jnp.where(tri_b, 1.0, 0.0).astype(BF16)
    cum = _dot_split3(tri, logf)
    cprev = cum - logf
    cum_last = cum[HG_ROWS - 1:HG_ROWS, :]
    s0b = s0.astype(BF16)
    o = _dot((q * jnp.exp(cum)).astype(BF16), s0b)
    row = _iota((HG_ROWS, LANES), 0)
    a_rows = []
    for i in range(HG_ROWS // HG_SUB):
        r0 = i * HG_SUB
        c_ref = cprev[r0:r0 + 1, :]
        qt = q[r0:r0 + HG_SUB] * jnp.exp(cum[r0:r0 + HG_SUB] - c_ref)
        kt = kk * jnp.exp(jnp.where(row < r0 + HG_SUB, c_ref - cum, NEG_INF))
        a_rows.append(_dot_nt(qt.astype(BF16), kt.astype(BF16)))
    a = jnp.where(tri_b, jnp.concatenate(a_rows, axis=0), 0.0)
    vb = v.astype(BF16)
    o = o + _dot(a.astype(BF16), vb)
    kdec = kk * jnp.exp(cum_last - cum)
    scale = jnp.broadcast_to(jnp.exp(cum_last), (HG_ROWS, LANES)).T
    s_new = scale * s0 + _dot(kdec.T.astype(BF16), vb)
    o = o[0:n_real]
    o = o * lax.rsqrt(jnp.mean(o * o, axis=-1, keepdims=True) + RMS_EPS) * ng
    return o * jax.nn.sigmoid(gpre), s_new


def _hgrn_kernel(q_ref, f_ref, v_ref, g_ref, s0_ref, lb_ref, ng_ref, o_ref, sfin_ref, s_scr, *, n_chunks, n_real):
    @pl.when(pl.program_id(2) == 0)
    def _init():
        s_scr[...] = s0_ref[...]

    lbp = lb_ref[...]
    e = jnp.exp(lbp - jnp.max(lbp, axis=0, keepdims=True))
    lb = e[0:1, :] / jnp.sum(e, axis=0, keepdims=True)
    ng = ng_ref[...]

    def chunk(c, carry):
        rows = slice(0, n_real) if n_chunks == 1 else pl.ds(pl.multiple_of(c * n_real, n_real), n_real)
        o, s_new = _hgrn_chunk(q_ref[rows, :], f_ref[rows, :], v_ref[rows, :], g_ref[rows, :], lb, ng,
                               s_scr[...], n_real)
        o_ref[rows, :] = o
        s_scr[...] = s_new
        return carry

    if n_chunks == 1:
        chunk(0, 0)
    else:
        lax.fori_loop(0, n_chunks, chunk, 0)
    sfin_ref[...] = s_scr[...]


def _hgrn(h, s0, hg_lb, hg_norm, nb, t, n_chunks, n_real):
    rows = n_chunks * n_real
    steps = t // rows
    col = lambda sec: pl.BlockSpec((rows, LANES), lambda b, hd, j: (b * steps + j, sec * HG_HEADS + hd))
    st = pl.BlockSpec((None, None, HG_DK, HG_DV), lambda b, hd, j: (b, hd, 0, 0))
    return pl.pallas_call(
        functools.partial(_hgrn_kernel, n_chunks=n_chunks, n_real=n_real),
        grid=(nb, HG_HEADS, steps),
        in_specs=[col(0), col(1), col(2), col(3), st,
                  pl.BlockSpec((hg_lb.shape[0], LANES), lambda b, hd, j: (0, hd)),
                  pl.BlockSpec((1, LANES), lambda b, hd, j: (0, hd))],
        out_specs=[pl.BlockSpec((rows, LANES), lambda b, hd, j: (b * steps + j, hd)), st],
        out_shape=[jax.ShapeDtypeStruct((nb * t, HG_WIDTH), F32),
                   jax.ShapeDtypeStruct((nb, HG_HEADS, HG_DK, HG_DV), F32)],
        scratch_shapes=[pltpu.VMEM((HG_DK, HG_DV), F32)],
        compiler_params=_params("parallel", "parallel", "arbitrary"),
        name="hgrn",
    )(h, h, h, h, s0, hg_lb, hg_norm)


def _outproj_kernel(x_ref, a_ref, hg_ref, wo_ref, nf_ref, wq_ref, x1_ref, xn_ref, pq_ref):
    mix = jnp.concatenate([a_ref[...], hg_ref[...]], axis=1).astype(BF16)
    x1 = x_ref[...] + _dot(mix, wo_ref[...])
    x1_ref[...] = x1
    ms = jnp.mean(x1 * x1, axis=-1, keepdims=True)
    xb = ((x1 * lax.rsqrt(ms + RMS_EPS)) * nf_ref[...]).astype(BF16)
    xn_ref[...] = xb
    pq_ref[...] = _dot(xb, wq_ref[...])


def _outproj(x, attn, ohg, w_out, norm_ffn, wq, tm):
    n = x.shape[0]
    row = lambda w: pl.BlockSpec((tm, w), lambda i: (i, 0))
    full = lambda a: pl.BlockSpec(a.shape, lambda i: (0, 0))
    nq = wq.shape[1]
    return pl.pallas_call(
        _outproj_kernel,
        grid=(n // tm,),
        in_specs=[row(D_MODEL), row(NSA_WIDTH), row(HG_WIDTH), full(w_out), full(norm_ffn), full(wq)],
        out_specs=[row(D_MODEL), row(D_MODEL), row(nq)],
        out_shape=[jax.ShapeDtypeStruct((n, D_MODEL), F32), jax.ShapeDtypeStruct((n, D_MODEL), BF16),
                   jax.ShapeDtypeStruct((n, nq), F32)],
        compiler_params=_params("parallel"),
        name="outproj",
    )(x, attn, ohg, w_out, norm_ffn, wq)


def _top_rows(s, k, val_scr, idx_scr):
    bi = _iota(s.shape, 0)
    big = s.shape[0]
    for a in range(k):
        m = jnp.max(s, axis=0, keepdims=True)
        idx = jnp.min(jnp.where(s == m, bi, big), axis=0, keepdims=True)
        val_scr[a:a + 1, :] = m
        idx_scr[a:a + 1, :] = idx
        s = jnp.where(bi == idx, LOWEST, s)


def _peer_select_kernel(pq_ref, keys_ref, i_ref, j_ref, g_ref, v1, i1, v2, i2, sc, cd):
    tms = pq_ref.shape[0]
    half = PEER_QDIM // 2
    s1 = _dot_nt(keys_ref[0].astype(BF16), pq_ref[:, 0:half].astype(BF16))
    s2 = _dot_nt(keys_ref[1].astype(BF16), pq_ref[:, half:2 * half].astype(BF16))
    _top_rows(s1, PEER_TOPK, v1, i1)
    _top_rows(s2, PEER_TOPK, v2, i2)
    a1 = v1[...]
    a2 = v2[...]
    r16 = _iota((PEER_TOPK, tms), 0)
    r8 = _iota((8, tms), 0)
    parts = [a1[0:1] + a2]
    codes = [r16]
    for a in range(1, 8):
        parts.append(a1[a:a + 1] + a2[0:8])
        codes.append(r8 + PEER_TOPK * a)
    parts.append(a1[8:16] + a2[0:1])
    codes.append((r8 + 8) * PEER_TOPK)
    cand = jnp.concatenate(parts, axis=0)
    code = jnp.concatenate(codes, axis=0)
    for k in range(PEER_TOPK):
        m = jnp.max(cand, axis=0, keepdims=True)
        cs = jnp.min(jnp.where(cand == m, code, PEER_TOPK * PEER_TOPK), axis=0, keepdims=True)
        sc[k:k + 1, :] = m
        cd[k:k + 1, :] = cs
        cand = jnp.where(code == cs, LOWEST, cand)
    scv = sc[...]
    cdv = cd[...]
    ak = cdv >> 4
    bk = cdv & (PEER_TOPK - 1)
    idx1 = i1[...]
    idx2 = i2[...]
    ik = jnp.zeros((PEER_TOPK, tms), jnp.int32)
    jk = jnp.zeros((PEER_TOPK, tms), jnp.int32)
    for a in range(PEER_TOPK):
        ik = jnp.where(ak == a, idx1[a:a + 1], ik)
        jk = jnp.where(bk == a, idx2[a:a + 1], jk)
    e = jnp.exp(scv - scv[0:1])
    i_ref[...] = ik.astype(F32)
    j_ref[...] = jk.astype(F32)
    g_ref[...] = e / jnp.sum(e, axis=0, keepdims=True)


def _peer_select(pq, keys, tms):
    n = pq.shape[0]
    out = pl.BlockSpec((PEER_TOPK, tms), lambda i, h: (h, i))
    shp = jax.ShapeDtypeStruct((PEER_HEADS * PEER_TOPK, n), F32)
    return pl.pallas_call(
        _peer_select_kernel,
        grid=(n // tms, PEER_HEADS),
        in_specs=[pl.BlockSpec((tms, PEER_QDIM), lambda i, h: (i, h)),
                  pl.BlockSpec((None, 2, PEER_KEYS, PEER_QDIM // 2), lambda i, h: (h, 0, 0, 0))],
        out_specs=[out, out, out],
        out_shape=[shp, shp, shp],
        scratch_shapes=[pltpu.VMEM((PEER_TOPK, tms), F32), pltpu.VMEM((PEER_TOPK, tms), jnp.int32),
                        pltpu.VMEM((PEER_TOPK, tms), F32), pltpu.VMEM((PEER_TOPK, tms), jnp.int32),
                        pltpu.VMEM((PEER_TOPK, tms), F32), pltpu.VMEM((PEER_TOPK, tms), jnp.int32)],
        compiler_params=_params("parallel", "arbitrary"),
        name="peer_select",
    )(pq, keys)


def _peer_dense_kernel(xn_ref, ik_ref, jk_ref, gk_ref, u_ref, v_ref, x1_ref, nf_ref, out_ref, w_scr, acc_scr, *, tm, te):
    e_idx = pl.program_id(1)

    @pl.when(e_idx == 0)
    def _build():
        sub = _iota((PEER_KEYS, LANES), 0).astype(F32)

        def body(n, carry):
            irow = ik_ref[pl.ds(n, 1), :]
            jrow = jk_ref[pl.ds(n, 1), :]
            grow = gk_ref[pl.ds(n, 1), :]
            a = jnp.where(irow == sub, grow, 0.0).astype(BF16)
            bt = jnp.where(jrow == sub, 1.0, 0.0).astype(BF16)
            w_scr[pl.ds(pl.multiple_of(n * PEER_KEYS, PEER_KEYS), PEER_KEYS), :] = _dot_nt(a, bt)
            return carry

        lax.fori_loop(0, tm, body, 0)
        acc_scr[...] = jnp.zeros(acc_scr.shape, F32)

    h = _dot_nt(xn_ref[...], u_ref[...])
    ni = te // PEER_KEYS
    wt = jnp.concatenate([w_scr[pl.ds(e_idx * ni + ii, tm, stride=PEER_KEYS), :] for ii in range(ni)], axis=1)
    act = (jax.nn.gelu(h) * wt).astype(BF16)
    acc_scr[...] += _dot(act, v_ref[...])

    @pl.when(e_idx == pl.num_programs(1) - 1)
    def _finish():
        y = x1_ref[...] + acc_scr[...]
        ms = jnp.mean(y * y, axis=-1, keepdims=True)
        out_ref[...] = (y * lax.rsqrt(ms + RMS_EPS)) * nf_ref[...]


def _peer_dense(xn, ik, jk, gk, u, v, x1, norm_final, tm, te):
    n = xn.shape[0]
    n_exp = u.shape[0]
    row = lambda w: pl.BlockSpec((tm, w), lambda i, e: (i, 0))
    exp_spec = pl.BlockSpec((te, D_MODEL), lambda i, e: (e, 0))
    return pl.pallas_call(
        functools.partial(_peer_dense_kernel, tm=tm, te=te),
        grid=(n // tm, n_exp // te),
        in_specs=[row(D_MODEL), row(LANES), row(LANES), row(LANES), exp_spec, exp_spec, row(D_MODEL),
                  pl.BlockSpec((1, D_MODEL), lambda i, e: (0, 0))],
        out_specs=row(D_MODEL),
        out_shape=jax.ShapeDtypeStruct((n, D_MODEL), F32),
        scratch_shapes=[pltpu.VMEM((tm * PEER_KEYS, LANES), F32), pltpu.VMEM((tm, D_MODEL), F32)],
        compiler_params=_params("parallel", "arbitrary"),
        name="peer_dense",
    )(xn, ik, jk, gk, u, v, x1, norm_final)


def _prep_w_in(w_in):
    w_main = jnp.concatenate([w_in[:, :COL_KV + 768], w_in[:, COL_KV + 768 + 3 * NSA_HEADS:]], axis=1)
    w_gate = jnp.pad(w_in[:, COL_KV + 768:COL_KV + 768 + 3 * NSA_HEADS], ((0, 0), (0, LANES - 3 * NSA_HEADS)))
    return jnp.concatenate([w_main, w_gate], axis=1).astype(BF16)


def _prep_cmp(cmp_wk, cmp_wv, cmp_pek, cmp_pev):
    def bd(w):
        z = jnp.zeros_like(w)
        return jnp.concatenate([jnp.concatenate([w, z], axis=2), jnp.concatenate([z, w], axis=2)], axis=1).astype(BF16)
    dup = lambda pe: jnp.concatenate([pe, pe], axis=1).astype(F32)
    return bd(cmp_wk), bd(cmp_wv), dup(cmp_pek), dup(cmp_pev)


def _ffn_tail(x, attn, ohg, w_out, norm_ffn, wq, keys, u, v, norm_final, tm, tms, tmd, te):
    x1, xn, pq = _outproj(x, attn, ohg, w_out, norm_ffn, wq, tm)
    ik, jk, gk = _peer_select(pq, keys, tms)
    return _peer_dense(xn, ik.T, jk.T, gk.T, u, v, x1, norm_final, tmd, te)


def kernel(x_prompt, x_sample, cache_cmp, cache_slc, cache_win, state_hgrn, page_table, norm_mix, w_in, cmp_wk, cmp_wv, cmp_pek, cmp_pev, hg_lb, hg_norm, w_out, norm_ffn, peer_wq, peer_keys, peer_u, peer_v, norm_final):
    b, t, d = x_prompt.shape
    db, tq, _ = x_sample.shape
    n_pool = cache_cmp.shape[1]
    n_pages = page_table.shape[1]
    past = n_pages * PAGE_ROWS
    wb = cache_win.shape[2]
    row = lambda a: a.reshape(1, -1)

    w_all = _prep_w_in(w_in[0])
    cw = _prep_cmp(cmp_wk[0], cmp_wv[0], cmp_pek[0], cmp_pev[0])
    w_out_b = w_out[0].astype(BF16)
    wq_b = peer_wq[0].astype(BF16)
    u_b = peer_u[0].astype(BF16)
    v_b = peer_v[0].astype(BF16)
    tail = lambda x, attn, ohg, tmd: _ffn_tail(x, attn, ohg, w_out_b, row(norm_ffn[0]), wq_b, peer_keys[0], u_b, v_b,
                                               row(norm_final), 512, 256, tmd, 512)

    cos, sin = _rope_tables(jnp.arange(t, dtype=jnp.int32))
    xp = x_prompt.reshape(b * t, d)
    q, qr, cmp_p, slc_p, win_p, slcb, winb, gates, hp = _inproj(xp, row(norm_mix[0]), w_all, cos, sin, 512)
    kcvc = _compress_prompt(cmp_p.reshape(b, t, 256), cw)
    ncp = t // CMP_STRIDE
    attn_p = _nsa_prompt(q.reshape(b, t, -1), qr.reshape(b, t, -1), gates.reshape(b, t, -1), kcvc,
                         slcb.reshape(b, t, 256), winb.reshape(b, t, 256),
                         _overlap_matrix(ncp, ncp - 1), _block_expand_matrix(t), 128)
    ohg_p, s_p = _hgrn(hp, jnp.zeros((b, HG_HEADS, HG_DK, HG_DV), F32), hg_lb, row(hg_norm[0]), b, t, 4, HG_ROWS)
    y_p = tail(xp, attn_p.reshape(b * t, -1), ohg_p, 256)

    pos_s = past + (jnp.arange(db * tq, dtype=jnp.int32) % tq)
    cos_s, sin_s = _rope_tables(pos_s)
    xs = x_sample.reshape(db * tq, d)
    q_s, qr_s, cmp_s, slc_s, win_s, _, _, gates_s, hs = _inproj(xs, row(norm_mix[0]), w_all, cos_s, sin_s, 256)
    kcvc_s = _compress_paged(cache_cmp[0].reshape(n_pool, PAGE_ROWS, 256), page_table, cw)
    ncs = past // CMP_STRIDE
    attn_s, nwin_s = _nsa_sample(q_s, qr_s, gates_s, kcvc_s, cache_slc[0].reshape(n_pool, PAGE_ROWS, 256), slc_s,
                                 cache_win[0].reshape(db, wb, 256), win_s, page_table,
                                 _overlap_matrix(ncs, ncs - 1), _block_expand_matrix(past + KEY_PAD), tq)
    ohg_s, s_s = _hgrn(hs, state_hgrn[0], hg_lb, row(hg_norm[0]), db, tq, 1, tq)
    y_s = tail(xs, attn_s, ohg_s, 256)

    kv5 = lambda a, nb, nt: a.reshape(1, nb, nt, 2, NSA_KV_HEADS, HEAD_DIM)
    keep = min(WINDOW, t)
    return (y_p.reshape(b, t, d), y_s.reshape(db, tq, d),
            kv5(cmp_p, b, t), kv5(slc_p, b, t), kv5(win_p.reshape(b, t, 256)[:, t - keep:], b, keep), s_p[None],
            kv5(cmp_s, db, tq), kv5(slc_s, db, tq), kv5(nwin_s, db, wb), s_s[None])
```

```python
import functools

import numpy as np
import jax
import jax.numpy as jnp
from jax import lax
from jax.experimental import pallas as pl
from jax.experimental.pallas import tpu as pltpu

F32 = jnp.float32
BF16 = jnp.bfloat16

D_MODEL = 1024
HEAD_DIM = 64
NSA_HEADS = 8
NSA_KV_HEADS = 2
NSA_GROUP = NSA_HEADS // NSA_KV_HEADS
CMP_BLOCK = 32
CMP_STRIDE = 16
SLC_BLOCK = 64
SLC_TOPN = 16
WINDOW = 512
ROPE_THETA = 10000.0
HG_HEADS = 4
HG_DK = 128
HG_DV = 128
NSA_WIDTH = NSA_HEADS * HEAD_DIM
HG_WIDTH = HG_HEADS * HG_DV
KV_WIDTH = NSA_KV_HEADS * HEAD_DIM
PEER_HEADS = 8
PEER_KEYS = 128
PEER_QDIM = 256
PEER_TOPK = 16
RMS_EPS = 1e-6
NEG_INF = -1e30
FORCED_SCORE = 1e6
LOWEST = -3e38

LANES = 128
HALF = 64
VMEM_LIMIT = 56 * 1024 * 1024

COL_Q = 0
COL_KV = 512
COL_H = 1280
COL_G = 3328
PROJ_PAD = 3456


def _dot(a, b):
    return jnp.dot(a, b, preferred_element_type=F32)


def _dot_nt(a, b):
    return lax.dot_general(a, b, (((1,), (1,)), ((), ())), preferred_element_type=F32)


def _iota(shape, dim):
    return lax.broadcasted_iota(jnp.int32, shape, dim)


def _params(*sem):
    return pltpu.CompilerParams(dimension_semantics=sem, vmem_limit_bytes=VMEM_LIMIT)


def _rope_tile(x, cos, sin_signed, first_half):
    partner = jnp.where(first_half, pltpu.roll(x, LANES - 32, 1), pltpu.roll(x, 32, 1))
    return x * cos + partner * sin_signed


def _inproj_kernel(x_ref, g_ref, w_ref, cos_ref, sin_ref,
                   q_ref, qr_ref, cmp_ref, slc_ref, win_ref, slcb_ref, winb_ref, gates_ref, h_ref):
    x = x_ref[...]
    ms = jnp.mean(x * x, axis=-1, keepdims=True)
    xn = (x * lax.rsqrt(ms + RMS_EPS)) * g_ref[...]
    proj = _dot(xn.astype(BF16), w_ref[...])
    cos = cos_ref[...]
    sin = sin_ref[...]
    first_half = (_iota(cos.shape, 1) & (HALF - 1)) < 32
    rope = lambda t: _rope_tile(t, cos, sin, first_half)
    q_ref[...] = proj[:, COL_Q:COL_Q + NSA_WIDTH]
    for c in range(NSA_WIDTH // LANES):
        qr_ref[:, c * LANES:(c + 1) * LANES] = rope(proj[:, COL_Q + c * LANES:COL_Q + (c + 1) * LANES])
    cmp_ref[...] = proj[:, COL_KV:COL_KV + 256]
    ks = rope(proj[:, COL_KV + 256:COL_KV + 384])
    vs = proj[:, COL_KV + 384:COL_KV + 512]
    kw = rope(proj[:, COL_KV + 512:COL_KV + 640])
    vw = proj[:, COL_KV + 640:COL_KV + 768]
    slc_ref[:, 0:LANES] = ks
    slc_ref[:, LANES:2 * LANES] = vs
    win_ref[:, 0:LANES] = kw
    win_ref[:, LANES:2 * LANES] = vw
    slcb_ref[:, 0:LANES] = ks.astype(BF16)
    slcb_ref[:, LANES:2 * LANES] = vs.astype(BF16)
    winb_ref[:, 0:LANES] = kw.astype(BF16)
    winb_ref[:, LANES:2 * LANES] = vw.astype(BF16)
    gates_ref[...] = jax.nn.sigmoid(proj[:, COL_G:COL_G + LANES])
    h_ref[...] = proj[:, COL_H:COL_H + 4 * HG_WIDTH]


def _inproj(x, norm_g, w_all, cos_t, sin_t, tm):
    n = x.shape[0]
    ntab = cos_t.shape[0] // tm
    row = lambda w: pl.BlockSpec((tm, w), lambda i: (i, 0))
    tab = pl.BlockSpec((tm, LANES), lambda i: (i % ntab, 0))
    out_shapes = [
        jax.ShapeDtypeStruct((n, NSA_WIDTH), F32), jax.ShapeDtypeStruct((n, NSA_WIDTH), F32),
        jax.ShapeDtypeStruct((n, 256), F32), jax.ShapeDtypeStruct((n, 256), F32),
        jax.ShapeDtypeStruct((n, 256), F32), jax.ShapeDtypeStruct((n, 256), BF16),
        jax.ShapeDtypeStruct((n, 256), BF16), jax.ShapeDtypeStruct((n, LANES), F32),
        jax.ShapeDtypeStruct((n, 4 * HG_WIDTH), F32),
    ]
    return pl.pallas_call(
        _inproj_kernel,
        grid=(n // tm,),
        in_specs=[row(D_MODEL), pl.BlockSpec((1, D_MODEL), lambda i: (0, 0)),
                  pl.BlockSpec((D_MODEL, PROJ_PAD), lambda i: (0, 0)), tab, tab],
        out_specs=[row(NSA_WIDTH), row(NSA_WIDTH), row(256), row(256), row(256), row(256), row(256),
                   row(LANES), row(4 * HG_WIDTH)],
        out_shape=out_shapes,
        compiler_params=_params("parallel"),
        name="inproj",
    )(x, norm_g, w_all, cos_t, sin_t)


def _rope_tables(pos):
    half = HEAD_DIM // 2
    inv = ROPE_THETA ** (-jnp.arange(half, dtype=F32) / half)
    ang = pos.astype(F32)[:, None] * inv[None, :]
    cos = jnp.tile(jnp.cos(ang), (1, 4))
    sin = jnp.sin(ang)
    return cos, jnp.tile(jnp.concatenate([-sin, sin], axis=1), (1, 2))


def _compress_rows(rows_ref, n_out, w_ref, pe_ref):
    a = jnp.zeros((n_out, LANES), F32)
    b = jnp.zeros((n_out, LANES), F32)
    for j in range(CMP_STRIDE):
        xj = rows_ref[pl.ds(j, n_out, stride=CMP_STRIDE), :]
        a = a + _dot((xj + pe_ref[j:j + 1, :]).astype(BF16), w_ref[j])
        b = b + _dot((xj + pe_ref[CMP_STRIDE + j:CMP_STRIDE + j + 1, :]).astype(BF16), w_ref[CMP_STRIDE + j])
    return a + pltpu.roll(b, n_out - 1, 0)


def _compress_prompt_kernel(rk_ref, rv_ref, wk_ref, wv_ref, pek_ref, pev_ref, out_ref):
    n_out = out_ref.shape[0]
    out_ref[:, 0:LANES] = _compress_rows(rk_ref, n_out, wk_ref, pek_ref).astype(BF16)
    out_ref[:, LANES:2 * LANES] = _compress_rows(rv_ref, n_out, wv_ref, pev_ref).astype(BF16)


def _compress_prompt(rows, cw):
    b, t, _ = rows.shape
    n_out = t // CMP_STRIDE
    wspec = pl.BlockSpec((CMP_BLOCK, LANES, LANES), lambda i: (0, 0, 0))
    pspec = pl.BlockSpec((CMP_BLOCK, LANES), lambda i: (0, 0))
    return pl.pallas_call(
        _compress_prompt_kernel,
        grid=(b,),
        in_specs=[pl.BlockSpec((None, t, LANES), lambda i: (i, 0, 0)),
                  pl.BlockSpec((None, t, LANES), lambda i: (i, 0, 1)), wspec, wspec, pspec, pspec],
        out_specs=pl.BlockSpec((None, n_out, 256), lambda i: (i, 0, 0)),
        out_shape=jax.ShapeDtypeStruct((b, n_out, 256), BF16),
        compiler_params=_params("parallel"),
        name="compress_prompt",
    )(rows, rows, *cw)


def _stack_heads(ref, lo_half, nq):
    hi_half = jnp.logical_not(lo_half)
    parts = []
    for h in range(NSA_HEADS):
        g = h // NSA_GROUP
        tile = ref[:, LANES * (h // 2):LANES * (h // 2 + 1)]
        if h % 2 != g:
            tile = pltpu.roll(tile, HALF, 1)
        parts.append(jnp.where(lo_half if g == 0 else hi_half, tile, 0.0))
    return (jnp.concatenate(parts, axis=0) * (HEAD_DIM ** -0.5)).astype(BF16)


def _per_head(x, nq):
    return jnp.concatenate([x[0:nq]] * NSA_GROUP + [x[nq:2 * nq]] * NSA_GROUP, axis=0)


def _masked_softmax_rows(s, valid):
    sm = jnp.where(valid, s, NEG_INF)
    m = jnp.max(sm, axis=-1, keepdims=True)
    e = jnp.where(valid, jnp.exp(sm - m), 0.0)
    l = jnp.sum(e, axis=-1, keepdims=True)
    return e / jnp.where(l > 0.0, l, 1.0)


def _select_blocks(imp, pos_col):
    blk = _iota(imp.shape, 1)
    cur = pos_col >> 6
    forced = (blk == 0) | (blk == cur) | (blk == cur - 1)
    causal = (blk << 6) <= pos_col
    score = jnp.where(causal, jnp.where(forced, FORCED_SCORE, imp), -FORCED_SCORE)
    return score, causal


def _topk_mask_t(score_t, n_sel):
    bi = _iota(score_t.shape, 0)
    sel = jnp.zeros(score_t.shape, F32)
    sc = score_t
    for _ in range(n_sel):
        m = jnp.max(sc, axis=0, keepdims=True)
        idx = jnp.min(jnp.where(sc == m, bi, LANES), axis=0, keepdims=True)
        hit = bi == idx
        sel = jnp.where(hit, 1.0, sel)
        sc = jnp.where(hit, LOWEST, sc)
    return sel


def _cmp_branch(qn, kc, vc, ov, pos_rows, pos_grp, nq):
    s = _dot_nt(qn, kc)
    valid = (_iota(s.shape, 1) * CMP_STRIDE + (CMP_BLOCK - 1)) <= pos_rows
    p = _masked_softmax_rows(s, valid)
    o_c = _dot(p.astype(BF16), vc)
    slab = lambda h: p[h * nq:(h + 1) * nq]
    psum = jnp.concatenate([slab(0) + slab(1) + slab(2) + slab(3), slab(4) + slab(5) + slab(6) + slab(7)], axis=0)
    hi = psum.astype(BF16)
    lo = (psum - hi.astype(F32)).astype(BF16)
    imp = _dot(hi, ov) + _dot(lo, ov)
    score, causal = _select_blocks(imp, pos_grp)
    return o_c, score, causal


def _combine_heads(out_ref, gates, lo_half, o_c, o_s, o_w, nq):
    combs = []
    for h in range(NSA_HEADS):
        rows = slice(h * nq, (h + 1) * nq)
        comb = (gates[:, h:h + 1] * o_c[rows] + gates[:, NSA_HEADS + h:NSA_HEADS + h + 1] * o_s[rows]
                + gates[:, 2 * NSA_HEADS + h:2 * NSA_HEADS + h + 1] * o_w[rows])
        if h % 2 != h // NSA_GROUP:
            comb = pltpu.roll(comb, HALF, 1)
        combs.append(comb)
    for r in range(NSA_HEADS // 2):
        out_ref[:, LANES * r:LANES * (r + 1)] = jnp.where(lo_half, combs[2 * r], combs[2 * r + 1])


KEY_TILE = 256


def _nsa_prompt_kernel(q_ref, qr_ref, gates_ref, cmp_ref, slc_ref, win_ref, ov_ref, et_ref, out_ref, *, tq):
    t0 = pl.program_id(1) * tq
    rows = NSA_HEADS * tq
    lo_half = _iota((tq, LANES), 1) < HALF
    pos_rows = t0 + (_iota((rows, 1), 0) & (tq - 1))
    pos_grp = t0 + (_iota((NSA_KV_HEADS * tq, 1), 0) & (tq - 1))
    n_tiles = (t0 + tq + KEY_TILE - 1) // KEY_TILE
    w_start = pl.multiple_of(jnp.maximum(t0 - WINDOW, 0), LANES)
    w_len = WINDOW + tq

    qn = _stack_heads(q_ref, lo_half, tq)
    qr = _stack_heads(qr_ref, lo_half, tq)
    o_c, score, causal = _cmp_branch(qn, cmp_ref[:, 0:LANES], cmp_ref[:, LANES:2 * LANES], ov_ref[...],
                                     pos_rows, pos_grp, tq)
    sel = jnp.where(causal, _topk_mask_t(score.T, SLC_TOPN).T, 0.0)
    notsel = (1.0 - sel).astype(BF16)

    sw = _dot_nt(qr, win_ref[pl.ds(w_start, w_len), 0:LANES])
    kp = w_start + _iota(sw.shape, 1)
    pw = _masked_softmax_rows(sw, (kp <= pos_rows) & (kp >= pos_rows - WINDOW))
    o_w = _dot(pw.astype(BF16), win_ref[pl.ds(w_start, w_len), LANES:2 * LANES])

    def tile_step(kt, carry, diag):
        m, l, acc = carry
        k0 = pl.multiple_of(kt * KEY_TILE, KEY_TILE)
        s = _dot_nt(qr, slc_ref[pl.ds(k0, KEY_TILE), 0:LANES])
        negb = _dot_nt(notsel, et_ref[pl.ds(k0, KEY_TILE), :]) * NEG_INF
        s = s + _per_head(negb, tq)
        if diag:
            s = jnp.where(k0 + _iota(s.shape, 1) <= pos_rows, s, NEG_INF)
        m_new = jnp.maximum(m, jnp.max(s, axis=-1, keepdims=True))
        alpha = jnp.exp(m - m_new)
        p = jnp.exp(s - m_new)
        l = alpha * l + jnp.sum(p, axis=-1, keepdims=True)
        acc = alpha * acc + _dot(p.astype(BF16), slc_ref[pl.ds(k0, KEY_TILE), LANES:2 * LANES])
        return m_new, l, acc

    init = (jnp.full((rows, 1), NEG_INF, F32), jnp.zeros((rows, 1), F32), jnp.zeros((rows, LANES), F32))
    carry = lax.fori_loop(0, n_tiles - 1, lambda kt, c: tile_step(kt, c, False), init)
    _, l, acc = tile_step(n_tiles - 1, carry, True)
    _combine_heads(out_ref, gates_ref[...], lo_half, o_c, acc / l, o_w, tq)


def _nsa_prompt(q, qr, gates, kcvc, slcb, winb, ov, et, tq):
    b, t, _ = q.shape
    ncp = kcvc.shape[1]
    blk = lambda w: pl.BlockSpec((None, tq, w), lambda bi, i: (bi, i, 0))
    full = lambda r, w: pl.BlockSpec((None, r, w), lambda bi, i: (bi, 0, 0))
    return pl.pallas_call(
        functools.partial(_nsa_prompt_kernel, tq=tq),
        grid=(b, t // tq),
        in_specs=[blk(NSA_WIDTH), blk(NSA_WIDTH), blk(LANES), full(ncp, 256), full(t, 256), full(t, 256),
                  pl.BlockSpec((ncp, LANES), lambda bi, i: (0, 0)),
                  pl.BlockSpec((t, LANES), lambda bi, i: (0, 0))],
        out_specs=blk(NSA_WIDTH),
        out_shape=jax.ShapeDtypeStruct((b, t, NSA_WIDTH), F32),
        compiler_params=_params("parallel", "arbitrary"),
        name="nsa_prompt",
    )(q, qr, gates, kcvc, slcb, winb, ov, et)


def _overlap_matrix(n_cmp_pad, n_cmp):
    c0 = jnp.arange(n_cmp_pad, dtype=jnp.int32)[:, None] * CMP_STRIDE
    s0 = jnp.arange(LANES, dtype=jnp.int32)[None, :] * SLC_BLOCK
    real = jnp.arange(n_cmp_pad, dtype=jnp.int32)[:, None] < n_cmp
    return ((c0 < s0 + SLC_BLOCK) & (c0 + CMP_BLOCK > s0) & real).astype(BF16)


def _block_expand_matrix(n_keys):
    r = jnp.arange(n_keys, dtype=jnp.int32)[:, None] // SLC_BLOCK
    return (r == jnp.arange(LANES, dtype=jnp.int32)[None, :]).astype(BF16)


PAGE_ROWS = 128


def _compress_paged_kernel(pt_ref, *refs, n_pages):
    pages = refs[:n_pages]
    wk_ref, wv_ref, pek_ref, pev_ref, out_ref, rk_scr, rv_scr = refs[n_pages:]
    for p in range(n_pages):
        rk_scr[p * PAGE_ROWS:(p + 1) * PAGE_ROWS, :] = pages[p][:, 0:LANES]
        rv_scr[p * PAGE_ROWS:(p + 1) * PAGE_ROWS, :] = pages[p][:, LANES:2 * LANES]
    n_out = out_ref.shape[0]
    out_ref[:, 0:LANES] = _compress_rows(rk_scr, n_out, wk_ref, pek_ref).astype(BF16)
    out_ref[:, LANES:2 * LANES] = _compress_rows(rv_scr, n_out, wv_ref, pev_ref).astype(BF16)


def _page_specs(n_pages):
    return [pl.BlockSpec((None, PAGE_ROWS, 256), functools.partial(lambda i, pt, p: (pt[i * n_pages + p], 0, 0), p=p))
            for p in range(n_pages)]


def _compress_paged(cache, page_table, cw):
    db, n_pages = page_table.shape
    past = n_pages * PAGE_ROWS
    n_out = past // CMP_STRIDE
    wspec = pl.BlockSpec((CMP_BLOCK, LANES, LANES), lambda i, pt: (0, 0, 0))
    pspec = pl.BlockSpec((CMP_BLOCK, LANES), lambda i, pt: (0, 0))
    grid_spec = pltpu.PrefetchScalarGridSpec(
        num_scalar_prefetch=1, grid=(db,),
        in_specs=_page_specs(n_pages) + [wspec, wspec, pspec, pspec],
        out_specs=pl.BlockSpec((None, n_out, 256), lambda i, pt: (i, 0, 0)),
        scratch_shapes=[pltpu.VMEM((past, LANES), F32), pltpu.VMEM((past, LANES), F32)])
    return pl.pallas_call(
        functools.partial(_compress_paged_kernel, n_pages=n_pages),
        grid_spec=grid_spec,
        out_shape=jax.ShapeDtypeStruct((db, n_out, 256), BF16),
        compiler_params=_params("arbitrary"),
        name="compress_paged",
    )(page_table.reshape(-1), *([cache] * n_pages), *cw)


KEY_PAD = 64


def _nsa_sample_kernel(pt_ref, q_ref, qr_ref, gates_ref, cmp_ref, *refs, n_pages):
    pages = refs[:n_pages]
    snew_ref, cwin_ref, wnew_ref, ov_ref, et_ref, out_ref, nwin_ref, slc_scr, win_scr = refs[n_pages:]
    nq = q_ref.shape[0]
    past = n_pages * PAGE_ROWS
    wb = cwin_ref.shape[0]
    rows = NSA_HEADS * nq
    for p in range(n_pages):
        slc_scr[p * PAGE_ROWS:(p + 1) * PAGE_ROWS, :] = pages[p][...]
    slc_scr[past:past + nq, :] = snew_ref[...]
    slc_scr[past + nq:past + KEY_PAD, :] = jnp.zeros((KEY_PAD - nq, 256), F32)
    win_scr[0:wb, :] = cwin_ref[...]
    win_scr[wb:wb + nq, :] = wnew_ref[...]
    win_scr[wb + nq:wb + KEY_PAD, :] = jnp.zeros((KEY_PAD - nq, 256), F32)
    nwin_ref[0:wb - nq, :] = cwin_ref[nq:wb, :]
    nwin_ref[wb - nq:wb, :] = wnew_ref[...]

    lo_half = _iota((nq, LANES), 1) < HALF
    pos_rows = past + (_iota((rows, 1), 0) & (nq - 1))
    n_grp = NSA_KV_HEADS * nq
    pos_grp = past + (_iota((n_grp, 1), 0) & (nq - 1))
    qn = _stack_heads(q_ref, lo_half, nq)
    qr = _stack_heads(qr_ref, lo_half, nq)
    o_c, score, causal = _cmp_branch(qn, cmp_ref[:, 0:LANES], cmp_ref[:, LANES:2 * LANES], ov_ref[...],
                                     pos_rows, pos_grp, nq)
    score_sq = jnp.concatenate([score, jnp.full((LANES - n_grp, LANES), LOWEST, F32)], axis=0)
    sel = jnp.where(causal, _topk_mask_t(score_sq.T, SLC_TOPN).T[0:n_grp], 0.0)
    notsel = (1.0 - sel).astype(BF16)

    s = _dot_nt(qr, slc_scr[:, 0:LANES].astype(BF16))
    s = s + _per_head(_dot_nt(notsel, et_ref[...]) * NEG_INF, nq)
    s = jnp.where(_iota(s.shape, 1) <= pos_rows, s, NEG_INF)
    e = jnp.exp(s - jnp.max(s, axis=-1, keepdims=True))
    o_s = _dot(e.astype(BF16), slc_scr[:, LANES:2 * LANES].astype(BF16)) / jnp.sum(e, axis=-1, keepdims=True)

    sw = _dot_nt(qr, win_scr[:, 0:LANES].astype(BF16))
    kp = (past - wb) + _iota(sw.shape, 1)
    pw = _masked_softmax_rows(sw, (kp <= pos_rows) & (kp >= pos_rows - WINDOW) & (kp >= 0))
    o_w = _dot(pw.astype(BF16), win_scr[:, LANES:2 * LANES].astype(BF16))
    _combine_heads(out_ref, gates_ref[...], lo_half, o_c, o_s, o_w, nq)


def _nsa_sample(q, qr, gates, kcvc, cache_slc, slc_new, cache_win, win_new, page_table, ov, et, nq):
    db, n_pages = page_table.shape
    past = n_pages * PAGE_ROWS
    wb = cache_win.shape[1]
    ncp = kcvc.shape[1]
    blk = lambda w: pl.BlockSpec((nq, w), lambda i, pt: (i, 0))
    grid_spec = pltpu.PrefetchScalarGridSpec(
        num_scalar_prefetch=1, grid=(db,),
        in_specs=[blk(NSA_WIDTH), blk(NSA_WIDTH), blk(LANES),
                  pl.BlockSpec((None, ncp, 256), lambda i, pt: (i, 0, 0))] + _page_specs(n_pages) + [
                  blk(256), pl.BlockSpec((None, wb, 256), lambda i, pt: (i, 0, 0)), blk(256),
                  pl.BlockSpec((ncp, LANES), lambda i, pt: (0, 0)),
                  pl.BlockSpec((past + KEY_PAD, LANES), lambda i, pt: (0, 0))],
        out_specs=[blk(NSA_WIDTH), pl.BlockSpec((None, wb, 256), lambda i, pt: (i, 0, 0))],
        scratch_shapes=[pltpu.VMEM((past + KEY_PAD, 256), F32), pltpu.VMEM((wb + KEY_PAD, 256), F32)])
    return pl.pallas_call(
        functools.partial(_nsa_sample_kernel, n_pages=n_pages),
        grid_spec=grid_spec,
        out_shape=[jax.ShapeDtypeStruct((db * nq, NSA_WIDTH), F32), jax.ShapeDtypeStruct((db, wb, 256), F32)],
        compiler_params=_params("arbitrary"),
        name="nsa_sample",
    )(page_table.reshape(-1), q, qr, gates, kcvc, *([cache_slc] * n_pages), slc_new, cache_win, win_new, ov, et)


HG_ROWS = 128
HG_SUB = 16


def _dot_split3(m_bf16, x):
    a = x.astype(BF16)
    r = x - a.astype(F32)
    b = r.astype(BF16)
    c = (r - b.astype(F32)).astype(BF16)
    return _dot(m_bf16, a) + _dot(m_bf16, b) + _dot(m_bf16, c)


def _hgrn_chunk(q, fpre, v, gpre, lb, ng, s0, n_real):
    f = lb + (1.0 - lb) * jax.nn.sigmoid(fpre)
    logf = jnp.log(f)
    kk = 1.0 - f
    if n_real < HG_ROWS:
        pad = lambda a: jnp.concatenate([a, jnp.zeros((HG_ROWS - n_real, LANES), F32)], axis=0)
        q, logf, kk, v = pad(q), pad(logf), pad(kk), pad(v)
    r_i = _iota((HG_ROWS, HG_ROWS), 0)
    c_i = _iota((HG_ROWS, HG_ROWS), 1)
    tri_b = r_i >= c_i
    tri = jnp.where(tri_b, 1.0, 0.0).astype(BF16)
    cum = _dot_split3(tri, logf)
    cprev = cum - logf
    cum_last = cum[HG_ROWS - 1:HG_ROWS, :]
    s0b = s0.astype(BF16)
    o = _dot((q * jnp.exp(cum)).astype(BF16), s0b)
    row = _iota((HG_ROWS, LANES), 0)
    a_rows = []
    n_sub = -(-n_real // HG_SUB)
    for i in range(n_sub):
        r0 = i * HG_SUB
        c_ref = cprev[r0:r0 + 1, :]
        qt = q[r0:r0 + HG_SUB] * jnp.exp(cum[r0:r0 + HG_SUB] - c_ref)
        kt = kk * jnp.exp(jnp.where(row < r0 + HG_SUB, c_ref - cum, NEG_INF))
        a_rows.append(_dot_nt(qt.astype(BF16), kt.astype(BF16)))
    if n_sub * HG_SUB < HG_ROWS:
        a_rows.append(jnp.zeros((HG_ROWS - n_sub * HG_SUB, HG_ROWS), F32))
    a = jnp.where(tri_b, jnp.concatenate(a_rows, axis=0), 0.0)
    vb = v.astype(BF16)
    o = o + _dot(a.astype(BF16), vb)
    kdec = kk * jnp.exp(cum_last - cum)
    scale = jnp.broadcast_to(jnp.exp(cum_last), (HG_ROWS, LANES)).T
    s_new = scale * s0 + _dot(kdec.T.astype(BF16), vb)
    o = o[0:n_real]
    o = o * lax.rsqrt(jnp.mean(o * o, axis=-1, keepdims=True) + RMS_EPS) * ng
    return o * jax.nn.sigmoid(gpre), s_new


def _hgrn_kernel(q_ref, f_ref, v_ref, g_ref, s0_ref, lb_ref, ng_ref, o_ref, sfin_ref, s_scr, *, n_chunks, n_real):
    @pl.when(pl.program_id(1) == 0)
    def _init():
        s_scr[...] = s0_ref[...]

    lbp = lb_ref[...]
    e = jnp.exp(lbp - jnp.max(lbp, axis=0, keepdims=True))
    lb = e[0:1, :] / jnp.sum(e, axis=0, keepdims=True)
    ng = ng_ref[...]

    def chunk(c, carry):
        rows = slice(0, n_real) if n_chunks == 1 else pl.ds(pl.multiple_of(c * n_real, n_real), n_real)
        for hd in range(HG_HEADS):
            cols = slice(hd * LANES, (hd + 1) * LANES)
            o, s_new = _hgrn_chunk(q_ref[rows, cols], f_ref[rows, cols], v_ref[rows, cols], g_ref[rows, cols],
                                   lb[:, cols], ng[:, cols], s_scr[hd], n_real)
            o_ref[rows, cols] = o
            s_scr[hd] = s_new
        return carry

    if n_chunks == 1:
        chunk(0, 0)
    else:
        lax.fori_loop(0, n_chunks, chunk, 0)
    sfin_ref[...] = s_scr[...]


def _hgrn(h, s0, hg_lb, hg_norm, nb, t, n_chunks, n_real):
    rows = n_chunks * n_real
    steps = t // rows
    col = lambda sec: pl.BlockSpec((rows, HG_WIDTH), lambda b, j: (b * steps + j, sec))
    st = pl.BlockSpec((None, HG_HEADS, HG_DK, HG_DV), lambda b, j: (b, 0, 0, 0))
    return pl.pallas_call(
        functools.partial(_hgrn_kernel, n_chunks=n_chunks, n_real=n_real),
        grid=(nb, steps),
        in_specs=[col(0), col(1), col(2), col(3), st,
                  pl.BlockSpec(hg_lb.shape, lambda b, j: (0, 0)),
                  pl.BlockSpec((1, HG_WIDTH), lambda b, j: (0, 0))],
        out_specs=[pl.BlockSpec((rows, HG_WIDTH), lambda b, j: (b * steps + j, 0)), st],
        out_shape=[jax.ShapeDtypeStruct((nb * t, HG_WIDTH), F32),
                   jax.ShapeDtypeStruct((nb, HG_HEADS, HG_DK, HG_DV), F32)],
        scratch_shapes=[pltpu.VMEM((HG_HEADS, HG_DK, HG_DV), F32)],
        compiler_params=_params("parallel", "arbitrary"),
        name="hgrn",
    )(h, h, h, h, s0, hg_lb, hg_norm)


def _outproj_kernel(x_ref, a_ref, hg_ref, wo_ref, nf_ref, wq_ref, x1_ref, xn_ref, pq_ref):
    mix = jnp.concatenate([a_ref[...], hg_ref[...]], axis=1).astype(BF16)
    x1 = x_ref[...] + _dot(mix, wo_ref[...])
    x1_ref[...] = x1
    ms = jnp.mean(x1 * x1, axis=-1, keepdims=True)
    xb = ((x1 * lax.rsqrt(ms + RMS_EPS)) * nf_ref[...]).astype(BF16)
    xn_ref[...] = xb
    pq_ref[...] = _dot(xb, wq_ref[...])


def _outproj(x, attn, ohg, w_out, norm_ffn, wq, tm):
    n = x.shape[0]
    row = lambda w: pl.BlockSpec((tm, w), lambda i: (i, 0))
    full = lambda a: pl.BlockSpec(a.shape, lambda i: (0, 0))
    nq = wq.shape[1]
    return pl.pallas_call(
        _outproj_kernel,
        grid=(n // tm,),
        in_specs=[row(D_MODEL), row(NSA_WIDTH), row(HG_WIDTH), full(w_out), full(norm_ffn), full(wq)],
        out_specs=[row(D_MODEL), row(D_MODEL), row(nq)],
        out_shape=[jax.ShapeDtypeStruct((n, D_MODEL), F32), jax.ShapeDtypeStruct((n, D_MODEL), BF16),
                   jax.ShapeDtypeStruct((n, nq), F32)],
        compiler_params=_params("parallel"),
        name="outproj",
    )(x, attn, ohg, w_out, norm_ffn, wq)


def _top_rows(s, k, val_scr, idx_scr):
    bi = _iota(s.shape, 0)
    big = s.shape[0]
    for a in range(k):
        m = jnp.max(s, axis=0, keepdims=True)
        idx = jnp.min(jnp.where(s == m, bi, big), axis=0, keepdims=True)
        val_scr[a:a + 1, :] = m
        idx_scr[a:a + 1, :] = idx
        s = jnp.where(bi == idx, LOWEST, s)


def _peer_select_kernel(pq_ref, keys_ref, i_ref, j_ref, g_ref, v12, i12, sc, cd):
    tms = pq_ref.shape[0]
    half = PEER_QDIM // 2
    s1 = _dot_nt(keys_ref[0].astype(BF16), pq_ref[:, 0:half].astype(BF16))
    s2 = _dot_nt(keys_ref[1].astype(BF16), pq_ref[:, half:2 * half].astype(BF16))
    _top_rows(jnp.concatenate([s1, s2], axis=1), PEER_TOPK, v12, i12)
    a1 = v12[:, 0:tms]
    a2 = v12[:, tms:2 * tms]
    r16 = _iota((PEER_TOPK, tms), 0)
    r8 = _iota((8, tms), 0)
    parts = [a1[0:1] + a2]
    codes = [r16]
    for a in range(1, 8):
        parts.append(a1[a:a + 1] + a2[0:8])
        codes.append(r8 + PEER_TOPK * a)
    parts.append(a1[8:16] + a2[0:1])
    codes.append((r8 + 8) * PEER_TOPK)
    cand = jnp.concatenate(parts, axis=0)
    code = jnp.concatenate(codes, axis=0)
    for k in range(PEER_TOPK):
        m = jnp.max(cand, axis=0, keepdims=True)
        cs = jnp.min(jnp.where(cand == m, code, PEER_TOPK * PEER_TOPK), axis=0, keepdims=True)
        sc[k:k + 1, :] = m
        cd[k:k + 1, :] = cs
        cand = jnp.where(code == cs, LOWEST, cand)
    scv = sc[...]
    cdv = cd[...]
    ak = cdv >> 4
    bk = cdv & (PEER_TOPK - 1)
    idx1 = i12[:, 0:tms]
    idx2 = i12[:, tms:2 * tms]
    ik = jnp.zeros((PEER_TOPK, tms), jnp.int32)
    jk = jnp.zeros((PEER_TOPK, tms), jnp.int32)
    for a in range(PEER_TOPK):
        ik = jnp.where(ak == a, idx1[a:a + 1], ik)
        jk = jnp.where(bk == a, idx2[a:a + 1], jk)
    e = jnp.exp(scv - scv[0:1])
    i_ref[...] = ik.astype(F32)
    j_ref[...] = jk.astype(F32)
    g_ref[...] = e / jnp.sum(e, axis=0, keepdims=True)


def _peer_select(pq, keys, tms):
    n = pq.shape[0]
    out = pl.BlockSpec((PEER_TOPK, tms), lambda i, h: (h, i))
    shp = jax.ShapeDtypeStruct((PEER_HEADS * PEER_TOPK, n), F32)
    return pl.pallas_call(
        _peer_select_kernel,
        grid=(n // tms, PEER_HEADS),
        in_specs=[pl.BlockSpec((tms, PEER_QDIM), lambda i, h: (i, h)),
                  pl.BlockSpec((None, 2, PEER_KEYS, PEER_QDIM // 2), lambda i, h: (h, 0, 0, 0))],
        out_specs=[out, out, out],
        out_shape=[shp, shp, shp],
        scratch_shapes=[pltpu.VMEM((PEER_TOPK, 2 * tms), F32), pltpu.VMEM((PEER_TOPK, 2 * tms), jnp.int32),
                        pltpu.VMEM((PEER_TOPK, tms), F32), pltpu.VMEM((PEER_TOPK, tms), jnp.int32)],
        compiler_params=_params("parallel", "arbitrary"),
        name="peer_select",
    )(pq, keys)


W_PITCH = PEER_KEYS + 8


def _peer_dense_kernel(xn_ref, ik_ref, jk_ref, gk_ref, u_ref, v_ref, x1_ref, nf_ref, out_ref, w_scr, acc_scr, *, tm, te):
    e_idx = pl.program_id(1)

    @pl.when(e_idx == 0)
    def _build():
        sub = _iota((PEER_KEYS, LANES), 0).astype(F32)

        def body(n, carry):
            irow = ik_ref[pl.ds(n, 1), :]
            jrow = jk_ref[pl.ds(n, 1), :]
            grow = gk_ref[pl.ds(n, 1), :]
            a = jnp.where(irow == sub, grow, 0.0).astype(BF16)
            bt = jnp.where(jrow == sub, 1.0, 0.0).astype(BF16)
            w_scr[pl.ds(pl.multiple_of(n * W_PITCH, 8), PEER_KEYS), :] = _dot_nt(a, bt)
            return carry

        lax.fori_loop(0, tm, body, 0, unroll=8)
        acc_scr[...] = jnp.zeros(acc_scr.shape, F32)

    h = _dot_nt(xn_ref[...], u_ref[...])
    ni = te // PEER_KEYS
    wt = jnp.concatenate([w_scr[pl.ds(e_idx * ni + ii, tm, stride=W_PITCH), :] for ii in range(ni)], axis=1)
    act = (jax.nn.gelu(h) * wt).astype(BF16)
    acc_scr[...] += _dot(act, v_ref[...])

    @pl.when(e_idx == pl.num_programs(1) - 1)
    def _finish():
        y = x1_ref[...] + acc_scr[...]
        ms = jnp.mean(y * y, axis=-1, keepdims=True)
        out_ref[...] = (y * lax.rsqrt(ms + RMS_EPS)) * nf_ref[...]


def _peer_dense(xn, ik, jk, gk, u, v, x1, norm_final, tm, te):
    n = xn.shape[0]
    n_exp = u.shape[0]
    row = lambda w: pl.BlockSpec((tm, w), lambda i, e: (i, 0))
    exp_spec = pl.BlockSpec((te, D_MODEL), lambda i, e: (e, 0))
    return pl.pallas_call(
        functools.partial(_peer_dense_kernel, tm=tm, te=te),
        grid=(n // tm, n_exp // te),
        in_specs=[row(D_MODEL), row(LANES), row(LANES), row(LANES), exp_spec, exp_spec, row(D_MODEL),
                  pl.BlockSpec((1, D_MODEL), lambda i, e: (0, 0))],
        out_specs=row(D_MODEL),
        out_shape=jax.ShapeDtypeStruct((n, D_MODEL), F32),
        scratch_shapes=[pltpu.VMEM((tm * W_PITCH, LANES), F32), pltpu.VMEM((tm, D_MODEL), F32)],
        compiler_params=_params("parallel", "arbitrary"),
        name="peer_dense",
    )(xn, ik, jk, gk, u, v, x1, norm_final)


def _prep_w_in(w_in):
    w_main = jnp.concatenate([w_in[:, :COL_KV + 768], w_in[:, COL_KV + 768 + 3 * NSA_HEADS:]], axis=1)
    w_gate = jnp.pad(w_in[:, COL_KV + 768:COL_KV + 768 + 3 * NSA_HEADS], ((0, 0), (0, LANES - 3 * NSA_HEADS)))
    return jnp.concatenate([w_main, w_gate], axis=1).astype(BF16)


def _prep_cmp(cmp_wk, cmp_wv, cmp_pek, cmp_pev):
    def bd(w):
        z = jnp.zeros_like(w)
        return jnp.concatenate([jnp.concatenate([w, z], axis=2), jnp.concatenate([z, w], axis=2)], axis=1).astype(BF16)
    dup = lambda pe: jnp.concatenate([pe, pe], axis=1).astype(F32)
    return bd(cmp_wk), bd(cmp_wv), dup(cmp_pek), dup(cmp_pev)


def _ffn_tail(x, attn, ohg, w_out, norm_ffn, wq, keys, u, v, norm_final, tm, tms, tmd, te):
    x1, xn, pq = _outproj(x, attn, ohg, w_out, norm_ffn, wq, tm)
    ik, jk, gk = _peer_select(pq, keys, tms)
    return _peer_dense(xn, ik.T, jk.T, gk.T, u, v, x1, norm_final, tmd, te)


def kernel(x_prompt, x_sample, cache_cmp, cache_slc, cache_win, state_hgrn, page_table, norm_mix, w_in, cmp_wk, cmp_wv, cmp_pek, cmp_pev, hg_lb, hg_norm, w_out, norm_ffn, peer_wq, peer_keys, peer_u, peer_v, norm_final):
    b, t, d = x_prompt.shape
    db, tq, _ = x_sample.shape
    n_pool = cache_cmp.shape[1]
    n_pages = page_table.shape[1]
    past = n_pages * PAGE_ROWS
    wb = cache_win.shape[2]
    row = lambda a: a.reshape(1, -1)

    w_all = _prep_w_in(w_in[0])
    cw = _prep_cmp(cmp_wk[0], cmp_wv[0], cmp_pek[0], cmp_pev[0])
    w_out_b = w_out[0].astype(BF16)
    wq_b = peer_wq[0].astype(BF16)
    u_b = peer_u[0].astype(BF16)
    v_b = peer_v[0].astype(BF16)
    tail = lambda x, attn, ohg, tmd: _ffn_tail(x, attn, ohg, w_out_b, row(norm_ffn[0]), wq_b, peer_keys[0], u_b, v_b,
                                               row(norm_final), 512, 256, tmd, 512)

    cos, sin = _rope_tables(jnp.arange(t, dtype=jnp.int32))
    xp = x_prompt.reshape(b * t, d)
    q, qr, cmp_p, slc_p, win_p, slcb, winb, gates, hp = _inproj(xp, row(norm_mix[0]), w_all, cos, sin, 512)
    kcvc = _compress_prompt(cmp_p.reshape(b, t, 256), cw)
    ncp = t // CMP_STRIDE
    attn_p = _nsa_prompt(q.reshape(b, t, -1), qr.reshape(b, t, -1), gates.reshape(b, t, -1), kcvc,
                         slcb.reshape(b, t, 256), winb.reshape(b, t, 256),
                         _overlap_matrix(ncp, ncp - 1), _block_expand_matrix(t), 128)
    ohg_p, s_p = _hgrn(hp, jnp.zeros((b, HG_HEADS, HG_DK, HG_DV), F32), hg_lb, row(hg_norm[0]), b, t, 4, HG_ROWS)
    y_p = tail(xp, attn_p.reshape(b * t, -1), ohg_p, 256)

    pos_s = past + (jnp.arange(db * tq, dtype=jnp.int32) % tq)
    cos_s, sin_s = _rope_tables(pos_s)
    xs = x_sample.reshape(db * tq, d)
    q_s, qr_s, cmp_s, slc_s, win_s, _, _, gates_s, hs = _inproj(xs, row(norm_mix[0]), w_all, cos_s, sin_s, 256)
    kcvc_s = _compress_paged(cache_cmp[0].reshape(n_pool, PAGE_ROWS, 256), page_table, cw)
    ncs = past // CMP_STRIDE
    attn_s, nwin_s = _nsa_sample(q_s, qr_s, gates_s, kcvc_s, cache_slc[0].reshape(n_pool, PAGE_ROWS, 256), slc_s,
                                 cache_win[0].reshape(db, wb, 256), win_s, page_table,
                                 _overlap_matrix(ncs, ncs - 1), _block_expand_matrix(past + KEY_PAD), tq)
    ohg_s, s_s = _hgrn(hs, state_hgrn[0], hg_lb, row(hg_norm[0]), db, tq, 1, tq)
    y_s = tail(xs, attn_s, ohg_s, 256)

    kv5 = lambda a, nb, nt: a.reshape(1, nb, nt, 2, NSA_KV_HEADS, HEAD_DIM)
    keep = min(WINDOW, t)
    return (y_p.reshape(b, t, d), y_s.reshape(db, tq, d),
            kv5(cmp_p, b, t), kv5(slc_p, b, t), kv5(win_p.reshape(b, t, 256)[:, t - keep:], b, keep), s_p[None],
            kv5(cmp_s, db, tq), kv5(slc_s, db, tq), kv5(nwin_s, db, wb), s_s[None])
```

```python
import functools

import numpy as np
import jax
import jax.numpy as jnp
from jax import lax
from jax.experimental import pallas as pl
from jax.experimental.pallas import tpu as pltpu

F32 = jnp.float32
BF16 = jnp.bfloat16

D_MODEL = 1024
HEAD_DIM = 64
NSA_HEADS = 8
NSA_KV_HEADS = 2
NSA_GROUP = NSA_HEADS // NSA_KV_HEADS
CMP_BLOCK = 32
CMP_STRIDE = 16
SLC_BLOCK = 64
SLC_TOPN = 16
WINDOW = 512
ROPE_THETA = 10000.0
HG_HEADS = 4
HG_DK = 128
HG_DV = 128
NSA_WIDTH = NSA_HEADS * HEAD_DIM
HG_WIDTH = HG_HEADS * HG_DV
KV_WIDTH = NSA_KV_HEADS * HEAD_DIM
PEER_HEADS = 8
PEER_KEYS = 128
PEER_QDIM = 256
PEER_TOPK = 16
RMS_EPS = 1e-6
NEG_INF = -1e30
FORCED_SCORE = 1e6
LOWEST = -3e38

LANES = 128
HALF = 64
VMEM_LIMIT = 56 * 1024 * 1024

COL_Q = 0
COL_KV = 512
COL_H = 1280
COL_G = 3328
PROJ_PAD = 3456


def _dot(a, b):
    return jnp.dot(a, b, preferred_element_type=F32)


def _dot_nt(a, b):
    return lax.dot_general(a, b, (((1,), (1,)), ((), ())), preferred_element_type=F32)


def _iota(shape, dim):
    return lax.broadcasted_iota(jnp.int32, shape, dim)


def _params(*sem):
    return pltpu.CompilerParams(dimension_semantics=sem, vmem_limit_bytes=VMEM_LIMIT)


def _rope_tile(x, cos, sin_signed, first_half):
    partner = jnp.where(first_half, pltpu.roll(x, LANES - 32, 1), pltpu.roll(x, 32, 1))
    return x * cos + partner * sin_signed


def _inproj_kernel(x_ref, g_ref, w_ref, cos_ref, sin_ref,
                   q_ref, qr_ref, cmp_ref, slc_ref, win_ref, slcb_ref, winb_ref, gates_ref, h_ref):
    x = x_ref[...]
    ms = jnp.mean(x * x, axis=-1, keepdims=True)
    xn = (x * lax.rsqrt(ms + RMS_EPS)) * g_ref[...]
    proj = _dot(xn.astype(BF16), w_ref[...])
    cos = cos_ref[...]
    sin = sin_ref[...]
    first_half = (_iota(cos.shape, 1) & (HALF - 1)) < 32
    rope = lambda t: _rope_tile(t, cos, sin, first_half)
    q_ref[...] = proj[:, COL_Q:COL_Q + NSA_WIDTH]
    for c in range(NSA_WIDTH // LANES):
        qr_ref[:, c * LANES:(c + 1) * LANES] = rope(proj[:, COL_Q + c * LANES:COL_Q + (c + 1) * LANES])
    cmp_ref[...] = proj[:, COL_KV:COL_KV + 256]
    ks = rope(proj[:, COL_KV + 256:COL_KV + 384])
    vs = proj[:, COL_KV + 384:COL_KV + 512]
    kw = rope(proj[:, COL_KV + 512:COL_KV + 640])
    vw = proj[:, COL_KV + 640:COL_KV + 768]
    slc_ref[:, 0:LANES] = ks
    slc_ref[:, LANES:2 * LANES] = vs
    win_ref[:, 0:LANES] = kw
    win_ref[:, LANES:2 * LANES] = vw
    slcb_ref[:, 0:LANES] = ks.astype(BF16)
    slcb_ref[:, LANES:2 * LANES] = vs.astype(BF16)
    winb_ref[:, 0:LANES] = kw.astype(BF16)
    winb_ref[:, LANES:2 * LANES] = vw.astype(BF16)
    gates_ref[...] = jax.nn.sigmoid(proj[:, COL_G:COL_G + LANES])
    h_ref[...] = proj[:, COL_H:COL_H + 4 * HG_WIDTH]


def _inproj(x, norm_g, w_all, cos_t, sin_t, tm):
    n = x.shape[0]
    ntab = cos_t.shape[0] // tm
    row = lambda w: pl.BlockSpec((tm, w), lambda i: (i, 0))
    tab = pl.BlockSpec((tm, LANES), lambda i: (i % ntab, 0))
    out_shapes = [
        jax.ShapeDtypeStruct((n, NSA_WIDTH), F32), jax.ShapeDtypeStruct((n, NSA_WIDTH), F32),
        jax.ShapeDtypeStruct((n, 256), F32), jax.ShapeDtypeStruct((n, 256), F32),
        jax.ShapeDtypeStruct((n, 256), F32), jax.ShapeDtypeStruct((n, 256), BF16),
        jax.ShapeDtypeStruct((n, 256), BF16), jax.ShapeDtypeStruct((n, LANES), F32),
        jax.ShapeDtypeStruct((n, 4 * HG_WIDTH), F32),
    ]
    return pl.pallas_call(
        _inproj_kernel,
        grid=(n // tm,),
        in_specs=[row(D_MODEL), pl.BlockSpec((1, D_MODEL), lambda i: (0, 0)),
                  pl.BlockSpec((D_MODEL, PROJ_PAD), lambda i: (0, 0)), tab, tab],
        out_specs=[row(NSA_WIDTH), row(NSA_WIDTH), row(256), row(256), row(256), row(256), row(256),
                   row(LANES), row(4 * HG_WIDTH)],
        out_shape=out_shapes,
        compiler_params=_params("parallel"),
        name="inproj",
    )(x, norm_g, w_all, cos_t, sin_t)


def _rope_tables(pos):
    half = HEAD_DIM // 2
    inv = ROPE_THETA ** (-jnp.arange(half, dtype=F32) / half)
    ang = pos.astype(F32)[:, None] * inv[None, :]
    cos = jnp.tile(jnp.cos(ang), (1, 4))
    sin = jnp.sin(ang)
    return cos, jnp.tile(jnp.concatenate([-sin, sin], axis=1), (1, 2))


def _compress_rows(rows_ref, n_out, w_ref, pe_ref):
    a = jnp.zeros((n_out, LANES), F32)
    b = jnp.zeros((n_out, LANES), F32)
    for j in range(CMP_STRIDE):
        xj = rows_ref[pl.ds(j, n_out, stride=CMP_STRIDE), :]
        a = a + _dot((xj + pe_ref[j:j + 1, :]).astype(BF16), w_ref[j])
        b = b + _dot((xj + pe_ref[CMP_STRIDE + j:CMP_STRIDE + j + 1, :]).astype(BF16), w_ref[CMP_STRIDE + j])
    return a + pltpu.roll(b, n_out - 1, 0)


def _compress_prompt_kernel(rk_ref, rv_ref, wk_ref, wv_ref, pek_ref, pev_ref, out_ref):
    n_out = out_ref.shape[0]
    out_ref[:, 0:LANES] = _compress_rows(rk_ref, n_out, wk_ref, pek_ref).astype(BF16)
    out_ref[:, LANES:2 * LANES] = _compress_rows(rv_ref, n_out, wv_ref, pev_ref).astype(BF16)


def _compress_prompt(rows, cw):
    b, t, _ = rows.shape
    n_out = t // CMP_STRIDE
    wspec = pl.BlockSpec((CMP_BLOCK, LANES, LANES), lambda i: (0, 0, 0))
    pspec = pl.BlockSpec((CMP_BLOCK, LANES), lambda i: (0, 0))
    return pl.pallas_call(
        _compress_prompt_kernel,
        grid=(b,),
        in_specs=[pl.BlockSpec((None, t, LANES), lambda i: (i, 0, 0)),
                  pl.BlockSpec((None, t, LANES), lambda i: (i, 0, 1)), wspec, wspec, pspec, pspec],
        out_specs=pl.BlockSpec((None, n_out, 256), lambda i: (i, 0, 0)),
        out_shape=jax.ShapeDtypeStruct((b, n_out, 256), BF16),
        compiler_params=_params("parallel"),
        name="compress_prompt",
    )(rows, rows, *cw)


def _stack_heads(ref, lo_half, nq):
    hi_half = jnp.logical_not(lo_half)
    parts = []
    for h in range(NSA_HEADS):
        g = h // NSA_GROUP
        tile = ref[:, LANES * (h // 2):LANES * (h // 2 + 1)]
        if h % 2 != g:
            tile = pltpu.roll(tile, HALF, 1)
        parts.append(jnp.where(lo_half if g == 0 else hi_half, tile, 0.0))
    return (jnp.concatenate(parts, axis=0) * (HEAD_DIM ** -0.5)).astype(BF16)


def _per_head(x, nq):
    return jnp.concatenate([x[0:nq]] * NSA_GROUP + [x[nq:2 * nq]] * NSA_GROUP, axis=0)


def _masked_softmax_rows(s, valid):
    sm = jnp.where(valid, s, NEG_INF)
    m = jnp.max(sm, axis=-1, keepdims=True)
    e = jnp.where(valid, jnp.exp(sm - m), 0.0)
    l = jnp.sum(e, axis=-1, keepdims=True)
    return e / jnp.where(l > 0.0, l, 1.0)


def _select_blocks(imp, pos_col):
    blk = _iota(imp.shape, 1)
    cur = pos_col >> 6
    forced = (blk == 0) | (blk == cur) | (blk == cur - 1)
    causal = (blk << 6) <= pos_col
    score = jnp.where(causal, jnp.where(forced, FORCED_SCORE, imp), -FORCED_SCORE)
    return score, causal


def _topk_mask_t(score_t, n_sel):
    bi = _iota(score_t.shape, 0)
    sel = jnp.zeros(score_t.shape, F32)
    sc = score_t
    for _ in range(n_sel):
        m = jnp.max(sc, axis=0, keepdims=True)
        idx = jnp.min(jnp.where(sc == m, bi, LANES), axis=0, keepdims=True)
        hit = bi == idx
        sel = jnp.where(hit, 1.0, sel)
        sc = jnp.where(hit, LOWEST, sc)
    return sel


def _cmp_branch(qn, kc, vc, ov, pos_rows, pos_grp, nq):
    s = _dot_nt(qn, kc)
    valid = (_iota(s.shape, 1) * CMP_STRIDE + (CMP_BLOCK - 1)) <= pos_rows
    p = _masked_softmax_rows(s, valid)
    o_c = _dot(p.astype(BF16), vc)
    slab = lambda h: p[h * nq:(h + 1) * nq]
    psum = jnp.concatenate([slab(0) + slab(1) + slab(2) + slab(3), slab(4) + slab(5) + slab(6) + slab(7)], axis=0)
    hi = psum.astype(BF16)
    lo = (psum - hi.astype(F32)).astype(BF16)
    imp = _dot(hi, ov) + _dot(lo, ov)
    score, causal = _select_blocks(imp, pos_grp)
    return o_c, score, causal


def _combine_heads(out_ref, gates, lo_half, o_c, o_s, o_w, nq):
    combs = []
    for h in range(NSA_HEADS):
        rows = slice(h * nq, (h + 1) * nq)
        comb = (gates[:, h:h + 1] * o_c[rows] + gates[:, NSA_HEADS + h:NSA_HEADS + h + 1] * o_s[rows]
                + gates[:, 2 * NSA_HEADS + h:2 * NSA_HEADS + h + 1] * o_w[rows])
        if h % 2 != h // NSA_GROUP:
            comb = pltpu.roll(comb, HALF, 1)
        combs.append(comb)
    for r in range(NSA_HEADS // 2):
        out_ref[:, LANES * r:LANES * (r + 1)] = jnp.where(lo_half, combs[2 * r], combs[2 * r + 1])


KEY_TILE = 256


def _masked_softmax_cols(s, valid):
    sm = jnp.where(valid, s, NEG_INF)
    m = jnp.max(sm, axis=0, keepdims=True)
    e = jnp.where(valid, jnp.exp(sm - m), 0.0)
    l = jnp.sum(e, axis=0, keepdims=True)
    return e / jnp.where(l > 0.0, l, 1.0)


def _nsa_prompt_kernel(q_ref, qr_ref, gates_ref, cmp_ref, vct_ref, slc_ref, vst_ref, win_ref, vwt_ref, ovt_ref, et_ref,
                       out_ref, *, tq):
    t0 = pl.program_id(1) * tq
    cols = NSA_HEADS * tq
    grp = NSA_KV_HEADS * tq
    pos_cols = t0 + (_iota((1, cols), 1) & (tq - 1))
    pos_grp = t0 + (_iota((1, grp), 1) & (tq - 1))
    n_tiles = (t0 + tq + KEY_TILE - 1) // KEY_TILE
    w_chunk = jnp.maximum(t0 - WINDOW, 0) // LANES
    w_start = pl.multiple_of(w_chunk * LANES, LANES)
    w_len = WINDOW + tq
    per_head = lambda x: jnp.concatenate([x[:, 0:tq]] * NSA_GROUP + [x[:, tq:grp]] * NSA_GROUP, axis=1)

    lo_half = _iota((tq, LANES), 1) < HALF
    qn = _stack_heads(q_ref, lo_half, tq)
    qr = _stack_heads(qr_ref, lo_half, tq)

    all_heads = lambda x: jnp.concatenate([x] * NSA_HEADS, axis=1)
    pos_tok = t0 + _iota((1, tq), 1)

    sc = _dot_nt(cmp_ref[:, 0:LANES], qn)
    n_i = _iota((sc.shape[0], tq), 0)
    sc = sc + all_heads(jnp.where(n_i * CMP_STRIDE + (CMP_BLOCK - 1) <= pos_tok, 0.0, NEG_INF))
    ec = jnp.exp(sc - jnp.max(sc, axis=0, keepdims=True))
    norm_c = jnp.where(pos_cols >= CMP_BLOCK - 1, 1.0 / jnp.sum(ec, axis=0, keepdims=True), 0.0)
    o_c = _dot(vct_ref[...], ec.astype(BF16)) * norm_c
    pc = ec * norm_c
    slab = lambda h: pc[:, h * tq:(h + 1) * tq]
    psum = jnp.concatenate([slab(0) + slab(1) + slab(2) + slab(3), slab(4) + slab(5) + slab(6) + slab(7)], axis=1)
    hi = psum.astype(BF16)
    lo = (psum - hi.astype(F32)).astype(BF16)
    ovt = ovt_ref[...]
    imp = _dot(ovt, hi) + _dot(ovt, lo)
    blk = _iota(imp.shape, 0)
    cur = pos_grp >> 6
    forced = (blk == 0) | (blk == cur) | (blk == cur - 1)
    causal = (blk << 6) <= pos_grp
    score = jnp.where(causal, jnp.where(forced, FORCED_SCORE, imp), -FORCED_SCORE)
    sel = jnp.where(causal, _topk_mask_t(score, SLC_TOPN), 0.0)
    notsel = (1.0 - sel).astype(BF16)

    sw = _dot_nt(win_ref[pl.ds(w_start, w_len), 0:LANES], qr)
    kp = w_start + _iota((w_len, tq), 0)
    sw = sw + all_heads(jnp.where((kp <= pos_tok) & (kp >= pos_tok - WINDOW), 0.0, NEG_INF))
    ew = jnp.exp(sw - jnp.max(sw, axis=0, keepdims=True))
    vwt = jnp.concatenate([vwt_ref[w_chunk + c] for c in range(w_len // LANES)], axis=1)
    o_w = _dot(vwt, ew.astype(BF16)) / jnp.sum(ew, axis=0, keepdims=True)

    def tile_step(kt, carry, diag):
        m, l, acc = carry
        k0 = pl.multiple_of(kt * KEY_TILE, KEY_TILE)
        s = _dot_nt(slc_ref[pl.ds(k0, KEY_TILE), 0:LANES], qr)
        s = s + per_head(_dot(et_ref[pl.ds(k0, KEY_TILE), :], notsel) * NEG_INF)
        if diag:
            s = s + all_heads(jnp.where(k0 + _iota((KEY_TILE, tq), 0) <= pos_tok, 0.0, NEG_INF))
        m_new = jnp.maximum(m, jnp.max(s, axis=0, keepdims=True))
        alpha = jnp.exp(m - m_new)
        p = jnp.exp(s - m_new)
        l = alpha * l + jnp.sum(p, axis=0, keepdims=True)
        acc = alpha * acc + _dot(vst_ref[kt], p.astype(BF16))
        return m_new, l, acc

    init = (jnp.full((1, cols), NEG_INF, F32), jnp.zeros((1, cols), F32), jnp.zeros((LANES, cols), F32))
    carry = lax.fori_loop(0, n_tiles - 1, lambda kt, c: tile_step(kt, c, False), init)
    _, l, acc = tile_step(n_tiles - 1, carry, True)
    o_s = acc / l

    gt = gates_ref[...].T
    for r in range(NSA_HEADS // 2):
        halves = []
        for h in (2 * r, 2 * r + 1):
            c = slice(h * tq, (h + 1) * tq)
            comb = (gt[h:h + 1] * o_c[:, c] + gt[NSA_HEADS + h:NSA_HEADS + h + 1] * o_s[:, c]
                    + gt[2 * NSA_HEADS + h:2 * NSA_HEADS + h + 1] * o_w[:, c])
            g = h // NSA_GROUP
            halves.append(comb[HALF * g:HALF * (g + 1)])
        out_ref[:, LANES * r:LANES * (r + 1)] = jnp.concatenate(halves, axis=0).T


def _nsa_prompt(q, qr, gates, kcvc, slcb, winb, ovt, et, tq):
    b, t, _ = q.shape
    ncp = kcvc.shape[1]
    to_t = lambda v, w: v.reshape(b, t // w, w, LANES).transpose(0, 1, 3, 2)
    vct = kcvc[:, :, LANES:].transpose(0, 2, 1)
    vst = to_t(slcb[:, :, LANES:], KEY_TILE)
    vwt = to_t(winb[:, :, LANES:], LANES)
    blk = lambda w: pl.BlockSpec((None, tq, w), lambda bi, i: (bi, i, 0))
    full = lambda r, w: pl.BlockSpec((None, r, w), lambda bi, i: (bi, 0, 0))
    full4 = lambda a: pl.BlockSpec((None,) + a.shape[1:], lambda bi, i: (bi, 0, 0, 0))
    return pl.pallas_call(
        functools.partial(_nsa_prompt_kernel, tq=tq),
        grid=(b, t // tq),
        in_specs=[blk(NSA_WIDTH), blk(NSA_WIDTH), blk(LANES), full(ncp, 256), full(LANES, ncp), full(t, 256),
                  full4(vst), full(t, 256), full4(vwt),
                  pl.BlockSpec((LANES, ncp), lambda bi, i: (0, 0)),
                  pl.BlockSpec((t, LANES), lambda bi, i: (0, 0))],
        out_specs=blk(NSA_WIDTH),
        out_shape=jax.ShapeDtypeStruct((b, t, NSA_WIDTH), F32),
        compiler_params=_params("parallel", "arbitrary"),
        name="nsa_prompt",
    )(q, qr, gates, kcvc, vct, slcb, vst, winb, vwt, ovt, et)


def _overlap_matrix(n_cmp_pad, n_cmp):
    c0 = jnp.arange(n_cmp_pad, dtype=jnp.int32)[:, None] * CMP_STRIDE
    s0 = jnp.arange(LANES, dtype=jnp.int32)[None, :] * SLC_BLOCK
    real = jnp.arange(n_cmp_pad, dtype=jnp.int32)[:, None] < n_cmp
    return ((c0 < s0 + SLC_BLOCK) & (c0 + CMP_BLOCK > s0) & real).astype(BF16)


def _block_expand_matrix(n_keys):
    r = jnp.arange(n_keys, dtype=jnp.int32)[:, None] // SLC_BLOCK
    return (r == jnp.arange(LANES, dtype=jnp.int32)[None, :]).astype(BF16)


PAGE_ROWS = 128


def _compress_paged_kernel(pt_ref, *refs, n_pages):
    pages = refs[:n_pages]
    wk_ref, wv_ref, pek_ref, pev_ref, out_ref, rk_scr, rv_scr = refs[n_pages:]
    for p in range(n_pages):
        rk_scr[p * PAGE_ROWS:(p + 1) * PAGE_ROWS, :] = pages[p][:, 0:LANES]
        rv_scr[p * PAGE_ROWS:(p + 1) * PAGE_ROWS, :] = pages[p][:, LANES:2 * LANES]
    n_out = out_ref.shape[0]
    out_ref[:, 0:LANES] = _compress_rows(rk_scr, n_out, wk_ref, pek_ref).astype(BF16)
    out_ref[:, LANES:2 * LANES] = _compress_rows(rv_scr, n_out, wv_ref, pev_ref).astype(BF16)


def _page_specs(n_pages):
    return [pl.BlockSpec((None, PAGE_ROWS, 256), functools.partial(lambda i, pt, p: (pt[i * n_pages + p], 0, 0), p=p))
            for p in range(n_pages)]


def _compress_paged(cache, page_table, cw):
    db, n_pages = page_table.shape
    past = n_pages * PAGE_ROWS
    n_out = past // CMP_STRIDE
    wspec = pl.BlockSpec((CMP_BLOCK, LANES, LANES), lambda i, pt: (0, 0, 0))
    pspec = pl.BlockSpec((CMP_BLOCK, LANES), lambda i, pt: (0, 0))
    grid_spec = pltpu.PrefetchScalarGridSpec(
        num_scalar_prefetch=1, grid=(db,),
        in_specs=_page_specs(n_pages) + [wspec, wspec, pspec, pspec],
        out_specs=pl.BlockSpec((None, n_out, 256), lambda i, pt: (i, 0, 0)),
        scratch_shapes=[pltpu.VMEM((past, LANES), F32), pltpu.VMEM((past, LANES), F32)])
    return pl.pallas_call(
        functools.partial(_compress_paged_kernel, n_pages=n_pages),
        grid_spec=grid_spec,
        out_shape=jax.ShapeDtypeStruct((db, n_out, 256), BF16),
        compiler_params=_params("arbitrary"),
        name="compress_paged",
    )(page_table.reshape(-1), *([cache] * n_pages), *cw)


KEY_PAD = 64


def _nsa_sample_kernel(pt_ref, q_ref, qr_ref, gates_ref, cmp_ref, *refs, n_pages):
    pages = refs[:n_pages]
    snew_ref, cwin_ref, wnew_ref, ov_ref, et_ref, out_ref, nwin_ref, slc_scr, win_scr = refs[n_pages:]
    nq = q_ref.shape[0]
    past = n_pages * PAGE_ROWS
    wb = cwin_ref.shape[0]
    rows = NSA_HEADS * nq
    for p in range(n_pages):
        slc_scr[p * PAGE_ROWS:(p + 1) * PAGE_ROWS, :] = pages[p][...]
    slc_scr[past:past + nq, :] = snew_ref[...]
    slc_scr[past + nq:past + KEY_PAD, :] = jnp.zeros((KEY_PAD - nq, 256), F32)
    win_scr[0:wb, :] = cwin_ref[...]
    win_scr[wb:wb + nq, :] = wnew_ref[...]
    win_scr[wb + nq:wb + KEY_PAD, :] = jnp.zeros((KEY_PAD - nq, 256), F32)
    nwin_ref[0:wb - nq, :] = cwin_ref[nq:wb, :]
    nwin_ref[wb - nq:wb, :] = wnew_ref[...]

    lo_half = _iota((nq, LANES), 1) < HALF
    pos_rows = past + (_iota((rows, 1), 0) & (nq - 1))
    n_grp = NSA_KV_HEADS * nq
    pos_grp = past + (_iota((n_grp, 1), 0) & (nq - 1))
    qn = _stack_heads(q_ref, lo_half, nq)
    qr = _stack_heads(qr_ref, lo_half, nq)
    o_c, score, causal = _cmp_branch(qn, cmp_ref[:, 0:LANES], cmp_ref[:, LANES:2 * LANES], ov_ref[...],
                                     pos_rows, pos_grp, nq)
    score_sq = jnp.concatenate([score, jnp.full((LANES - n_grp, LANES), LOWEST, F32)], axis=0)
    sel = jnp.where(causal, _topk_mask_t(score_sq.T, SLC_TOPN).T[0:n_grp], 0.0)
    notsel = (1.0 - sel).astype(BF16)

    s = _dot_nt(qr, slc_scr[:, 0:LANES].astype(BF16))
    s = s + _per_head(_dot_nt(notsel, et_ref[...]) * NEG_INF, nq)
    s = jnp.where(_iota(s.shape, 1) <= pos_rows, s, NEG_INF)
    e = jnp.exp(s - jnp.max(s, axis=-1, keepdims=True))
    o_s = _dot(e.astype(BF16), slc_scr[:, LANES:2 * LANES].astype(BF16)) / jnp.sum(e, axis=-1, keepdims=True)

    sw = _dot_nt(qr, win_scr[:, 0:LANES].astype(BF16))
    kp = (past - wb) + _iota(sw.shape, 1)
    pw = _masked_softmax_rows(sw, (kp <= pos_rows) & (kp >= pos_rows - WINDOW) & (kp >= 0))
    o_w = _dot(pw.astype(BF16), win_scr[:, LANES:2 * LANES].astype(BF16))
    _combine_heads(out_ref, gates_ref[...], lo_half, o_c, o_s, o_w, nq)


def _nsa_sample(q, qr, gates, kcvc, cache_slc, slc_new, cache_win, win_new, page_table, ov, et, nq):
    db, n_pages = page_table.shape
    past = n_pages * PAGE_ROWS
    wb = cache_win.shape[1]
    ncp = kcvc.shape[1]
    blk = lambda w: pl.BlockSpec((nq, w), lambda i, pt: (i, 0))
    grid_spec = pltpu.PrefetchScalarGridSpec(
        num_scalar_prefetch=1, grid=(db,),
        in_specs=[blk(NSA_WIDTH), blk(NSA_WIDTH), blk(LANES),
                  pl.BlockSpec((None, ncp, 256), lambda i, pt: (i, 0, 0))] + _page_specs(n_pages) + [
                  blk(256), pl.BlockSpec((None, wb, 256), lambda i, pt: (i, 0, 0)), blk(256),
                  pl.BlockSpec((ncp, LANES), lambda i, pt: (0, 0)),
                  pl.BlockSpec((past + KEY_PAD, LANES), lambda i, pt: (0, 0))],
        out_specs=[blk(NSA_WIDTH), pl.BlockSpec((None, wb, 256), lambda i, pt: (i, 0, 0))],
        scratch_shapes=[pltpu.VMEM((past + KEY_PAD, 256), F32), pltpu.VMEM((wb + KEY_PAD, 256), F32)])
    return pl.pallas_call(
        functools.partial(_nsa_sample_kernel, n_pages=n_pages),
        grid_spec=grid_spec,
        out_shape=[jax.ShapeDtypeStruct((db * nq, NSA_WIDTH), F32), jax.ShapeDtypeStruct((db, wb, 256), F32)],
        compiler_params=_params("arbitrary"),
        name="nsa_sample",
    )(page_table.reshape(-1), q, qr, gates, kcvc, *([cache_slc] * n_pages), slc_new, cache_win, win_new, ov, et)


HG_ROWS = 128
HG_SUB = 16


def _dot_split3(m_bf16, x):
    a = x.astype(BF16)
    r = x - a.astype(F32)
    b = r.astype(BF16)
    c = (r - b.astype(F32)).astype(BF16)
    return _dot(m_bf16, a) + _dot(m_bf16, b) + _dot(m_bf16, c)


def _hgrn_chunk(q, fpre, v, gpre, lb, ng, s0, n_real):
    f = lb + (1.0 - lb) * jax.nn.sigmoid(fpre)
    logf = jnp.log(f)
    kk = 1.0 - f
    if n_real < HG_ROWS:
        pad = lambda a: jnp.concatenate([a, jnp.zeros((HG_ROWS - n_real, LANES), F32)], axis=0)
        q, logf, kk, v = pad(q), pad(logf), pad(kk), pad(v)
    r_i = _iota((HG_ROWS, HG_ROWS), 0)
    c_i = _iota((HG_ROWS, HG_ROWS), 1)
    tri_b = r_i >= c_i
    tri = jnp.where(tri_b, 1.0, 0.0).astype(BF16)
    cum = _dot_split3(tri, logf)
    cprev = cum - logf
    cum_last = cum[HG_ROWS - 1:HG_ROWS, :]
    s0b = s0.astype(BF16)
    o = _dot((q * jnp.exp(cum)).astype(BF16), s0b)
    row = _iota((HG_ROWS, LANES), 0)
    a_rows = []
    n_sub = -(-n_real // HG_SUB)
    for i in range(n_sub):
        r0 = i * HG_SUB
        c_ref = cprev[r0:r0 + 1, :]
        qt = q[r0:r0 + HG_SUB] * jnp.exp(cum[r0:r0 + HG_SUB] - c_ref)
        kt = kk * jnp.exp(jnp.where(row < r0 + HG_SUB, c_ref - cum, NEG_INF))
        a_rows.append(_dot_nt(qt.astype(BF16), kt.astype(BF16)))
    if n_sub * HG_SUB < HG_ROWS:
        a_rows.append(jnp.zeros((HG_ROWS - n_sub * HG_SUB, HG_ROWS), F32))
    a = jnp.where(tri_b, jnp.concatenate(a_rows, axis=0), 0.0)
    vb = v.astype(BF16)
    o = o + _dot(a.astype(BF16), vb)
    kdec = kk * jnp.exp(cum_last - cum)
    scale = jnp.broadcast_to(jnp.exp(cum_last), (HG_ROWS, LANES)).T
    s_new = scale * s0 + _dot(kdec.T.astype(BF16), vb)
    o = o[0:n_real]
    o = o * lax.rsqrt(jnp.mean(o * o, axis=-1, keepdims=True) + RMS_EPS) * ng
    return o * jax.nn.sigmoid(gpre), s_new


def _hgrn_kernel(q_ref, f_ref, v_ref, g_ref, s0_ref, lb_ref, ng_ref, o_ref, sfin_ref, s_scr, *, n_chunks, n_real):
    @pl.when(pl.program_id(1) == 0)
    def _init():
        s_scr[...] = s0_ref[...]

    lbp = lb_ref[...]
    e = jnp.exp(lbp - jnp.max(lbp, axis=0, keepdims=True))
    lb = e[0:1, :] / jnp.sum(e, axis=0, keepdims=True)
    ng = ng_ref[...]

    def chunk(c, carry):
        rows = slice(0, n_real) if n_chunks == 1 else pl.ds(pl.multiple_of(c * n_real, n_real), n_real)
        for hd in range(HG_HEADS):
            cols = slice(hd * LANES, (hd + 1) * LANES)
            o, s_new = _hgrn_chunk(q_ref[rows, cols], f_ref[rows, cols], v_ref[rows, cols], g_ref[rows, cols],
                                   lb[:, cols], ng[:, cols], s_scr[hd], n_real)
            o_ref[rows, cols] = o
            s_scr[hd] = s_new
        return carry

    if n_chunks == 1:
        chunk(0, 0)
    else:
        lax.fori_loop(0, n_chunks, chunk, 0)
    sfin_ref[...] = s_scr[...]


def _hgrn(h, s0, hg_lb, hg_norm, nb, t, n_chunks, n_real):
    rows = n_chunks * n_real
    steps = t // rows
    col = lambda sec: pl.BlockSpec((rows, HG_WIDTH), lambda b, j: (b * steps + j, sec))
    st = pl.BlockSpec((None, HG_HEADS, HG_DK, HG_DV), lambda b, j: (b, 0, 0, 0))
    return pl.pallas_call(
        functools.partial(_hgrn_kernel, n_chunks=n_chunks, n_real=n_real),
        grid=(nb, steps),
        in_specs=[col(0), col(1), col(2), col(3), st,
                  pl.BlockSpec(hg_lb.shape, lambda b, j: (0, 0)),
                  pl.BlockSpec((1, HG_WIDTH), lambda b, j: (0, 0))],
        out_specs=[pl.BlockSpec((rows, HG_WIDTH), lambda b, j: (b * steps + j, 0)), st],
        out_shape=[jax.ShapeDtypeStruct((nb * t, HG_WIDTH), F32),
                   jax.ShapeDtypeStruct((nb, HG_HEADS, HG_DK, HG_DV), F32)],
        scratch_shapes=[pltpu.VMEM((HG_HEADS, HG_DK, HG_DV), F32)],
        compiler_params=_params("parallel", "arbitrary"),
        name="hgrn",
    )(h, h, h, h, s0, hg_lb, hg_norm)


def _outproj_kernel(x_ref, a_ref, hg_ref, wo_ref, nf_ref, wq_ref, x1_ref, xn_ref, pq_ref):
    mix = jnp.concatenate([a_ref[...], hg_ref[...]], axis=1).astype(BF16)
    x1 = x_ref[...] + _dot(mix, wo_ref[...])
    x1_ref[...] = x1
    ms = jnp.mean(x1 * x1, axis=-1, keepdims=True)
    xb = ((x1 * lax.rsqrt(ms + RMS_EPS)) * nf_ref[...]).astype(BF16)
    xn_ref[...] = xb
    pq_ref[...] = _dot(xb, wq_ref[...])


def _outproj(x, attn, ohg, w_out, norm_ffn, wq, tm):
    n = x.shape[0]
    row = lambda w: pl.BlockSpec((tm, w), lambda i: (i, 0))
    full = lambda a: pl.BlockSpec(a.shape, lambda i: (0, 0))
    nq = wq.shape[1]
    return pl.pallas_call(
        _outproj_kernel,
        grid=(n // tm,),
        in_specs=[row(D_MODEL), row(NSA_WIDTH), row(HG_WIDTH), full(w_out), full(norm_ffn), full(wq)],
        out_specs=[row(D_MODEL), row(D_MODEL), row(nq)],
        out_shape=[jax.ShapeDtypeStruct((n, D_MODEL), F32), jax.ShapeDtypeStruct((n, D_MODEL), BF16),
                   jax.ShapeDtypeStruct((n, nq), F32)],
        compiler_params=_params("parallel"),
        name="outproj",
    )(x, attn, ohg, w_out, norm_ffn, wq)


def _top_rows(s, k, val_scr, idx_scr):
    bi = _iota(s.shape, 0)
    big = s.shape[0]
    for a in range(k):
        m = jnp.max(s, axis=0, keepdims=True)
        idx = jnp.min(jnp.where(s == m, bi, big), axis=0, keepdims=True)
        val_scr[a:a + 1, :] = m
        idx_scr[a:a + 1, :] = idx
        s = jnp.where(bi == idx, LOWEST, s)


def _peer_select_kernel(pq_ref, keys_ref, i_ref, j_ref, g_ref, v12, i12, sc, cd):
    tms = pq_ref.shape[0]
    half = PEER_QDIM // 2
    s1 = _dot_nt(keys_ref[0].astype(BF16), pq_ref[:, 0:half].astype(BF16))
    s2 = _dot_nt(keys_ref[1].astype(BF16), pq_ref[:, half:2 * half].astype(BF16))
    _top_rows(jnp.concatenate([s1, s2], axis=1), PEER_TOPK, v12, i12)
    a1 = v12[:, 0:tms]
    a2 = v12[:, tms:2 * tms]
    r16 = _iota((PEER_TOPK, tms), 0)
    r8 = _iota((8, tms), 0)
    parts = [a1[0:1] + a2]
    codes = [r16]
    for a in range(1, 8):
        parts.append(a1[a:a + 1] + a2[0:8])
        codes.append(r8 + PEER_TOPK * a)
    parts.append(a1[8:16] + a2[0:1])
    codes.append((r8 + 8) * PEER_TOPK)
    cand = jnp.concatenate(parts, axis=0)
    code = jnp.concatenate(codes, axis=0)
    for k in range(PEER_TOPK):
        m = jnp.max(cand, axis=0, keepdims=True)
        cs = jnp.min(jnp.where(cand == m, code, PEER_TOPK * PEER_TOPK), axis=0, keepdims=True)
        sc[k:k + 1, :] = m
        cd[k:k + 1, :] = cs
        cand = jnp.where(code == cs, LOWEST, cand)
    scv = sc[...]
    cdv = cd[...]
    ak = cdv >> 4
    bk = cdv & (PEER_TOPK - 1)
    idx1 = i12[:, 0:tms]
    idx2 = i12[:, tms:2 * tms]
    ik = jnp.zeros((PEER_TOPK, tms), jnp.int32)
    jk = jnp.zeros((PEER_TOPK, tms), jnp.int32)
    for a in range(PEER_TOPK):
        ik = jnp.where(ak == a, idx1[a:a + 1], ik)
        jk = jnp.where(bk == a, idx2[a:a + 1], jk)
    e = jnp.exp(scv - scv[0:1])
    i_ref[...] = ik.astype(F32)
    j_ref[...] = jk.astype(F32)
    g_ref[...] = e / jnp.sum(e, axis=0, keepdims=True)


def _peer_select(pq, keys, tms):
    n = pq.shape[0]
    out = pl.BlockSpec((PEER_TOPK, tms), lambda i, h: (h, i))
    shp = jax.ShapeDtypeStruct((PEER_HEADS * PEER_TOPK, n), F32)
    return pl.pallas_call(
        _peer_select_kernel,
        grid=(n // tms, PEER_HEADS),
        in_specs=[pl.BlockSpec((tms, PEER_QDIM), lambda i, h: (i, h)),
                  pl.BlockSpec((None, 2, PEER_KEYS, PEER_QDIM // 2), lambda i, h: (h, 0, 0, 0))],
        out_specs=[out, out, out],
        out_shape=[shp, shp, shp],
        scratch_shapes=[pltpu.VMEM((PEER_TOPK, 2 * tms), F32), pltpu.VMEM((PEER_TOPK, 2 * tms), jnp.int32),
                        pltpu.VMEM((PEER_TOPK, tms), F32), pltpu.VMEM((PEER_TOPK, tms), jnp.int32)],
        compiler_params=_params("parallel", "arbitrary"),
        name="peer_select",
    )(pq, keys)


W_PITCH = PEER_KEYS + 8


def _peer_dense_kernel(xn_ref, ik_ref, jk_ref, gk_ref, u_ref, v_ref, x1_ref, nf_ref, out_ref, w_scr, acc_scr, *, tm, te):
    e_idx = pl.program_id(1)

    @pl.when(e_idx == 0)
    def _build():
        sub = _iota((PEER_KEYS, LANES), 0).astype(F32)

        def body(n, carry):
            irow = ik_ref[pl.ds(n, 1), :]
            jrow = jk_ref[pl.ds(n, 1), :]
            grow = gk_ref[pl.ds(n, 1), :]
            a = jnp.where(irow == sub, grow, 0.0).astype(BF16)
            bt = jnp.where(jrow == sub, 1.0, 0.0).astype(BF16)
            w_scr[pl.ds(pl.multiple_of(n * W_PITCH, 8), PEER_KEYS), :] = _dot_nt(a, bt)
            return carry

        lax.fori_loop(0, tm, body, 0, unroll=8)
        acc_scr[...] = jnp.zeros(acc_scr.shape, F32)

    h = _dot_nt(xn_ref[...], u_ref[...])
    ni = te // PEER_KEYS
    wt = jnp.concatenate([w_scr[pl.ds(e_idx * ni + ii, tm, stride=W_PITCH), :] for ii in range(ni)], axis=1)
    act = (jax.nn.gelu(h) * wt).astype(BF16)
    acc_scr[...] += _dot(act, v_ref[...])

    @pl.when(e_idx == pl.num_programs(1) - 1)
    def _finish():
        y = x1_ref[...] + acc_scr[...]
        ms = jnp.mean(y * y, axis=-1, keepdims=True)
        out_ref[...] = (y * lax.rsqrt(ms + RMS_EPS)) * nf_ref[...]


def _peer_dense(xn, ik, jk, gk, u, v, x1, norm_final, tm, te):
    n = xn.shape[0]
    n_exp = u.shape[0]
    row = lambda w: pl.BlockSpec((tm, w), lambda i, e: (i, 0), pipeline_mode=pl.Buffered(1))
    exp_spec = pl.BlockSpec((te, D_MODEL), lambda i, e: (e, 0))
    return pl.pallas_call(
        functools.partial(_peer_dense_kernel, tm=tm, te=te),
        grid=(n // tm, n_exp // te),
        in_specs=[row(D_MODEL), row(LANES), row(LANES), row(LANES), exp_spec, exp_spec, row(D_MODEL),
                  pl.BlockSpec((1, D_MODEL), lambda i, e: (0, 0))],
        out_specs=pl.BlockSpec((tm, D_MODEL), lambda i, e: (i, 0)),
        out_shape=jax.ShapeDtypeStruct((n, D_MODEL), F32),
        scratch_shapes=[pltpu.VMEM((tm * W_PITCH, LANES), F32), pltpu.VMEM((tm, D_MODEL), F32)],
        compiler_params=_params("parallel", "arbitrary"),
        name="peer_dense",
    )(xn, ik, jk, gk, u, v, x1, norm_final)


def _prep_w_in(w_in):
    w_main = jnp.concatenate([w_in[:, :COL_KV + 768], w_in[:, COL_KV + 768 + 3 * NSA_HEADS:]], axis=1)
    w_gate = jnp.pad(w_in[:, COL_KV + 768:COL_KV + 768 + 3 * NSA_HEADS], ((0, 0), (0, LANES - 3 * NSA_HEADS)))
    return jnp.concatenate([w_main, w_gate], axis=1).astype(BF16)


def _prep_cmp(cmp_wk, cmp_wv, cmp_pek, cmp_pev):
    def bd(w):
        z = jnp.zeros_like(w)
        return jnp.concatenate([jnp.concatenate([w, z], axis=2), jnp.concatenate([z, w], axis=2)], axis=1).astype(BF16)
    dup = lambda pe: jnp.concatenate([pe, pe], axis=1).astype(F32)
    return bd(cmp_wk), bd(cmp_wv), dup(cmp_pek), dup(cmp_pev)


def _ffn_tail(x, attn, ohg, w_out, norm_ffn, wq, keys, u, v, norm_final, tm, tms, tmd, te):
    x1, xn, pq = _outproj(x, attn, ohg, w_out, norm_ffn, wq, tm)
    ik, jk, gk = _peer_select(pq, keys, tms)
    return _peer_dense(xn, ik.T, jk.T, gk.T, u, v, x1, norm_final, tmd, te)


def kernel(x_prompt, x_sample, cache_cmp, cache_slc, cache_win, state_hgrn, page_table, norm_mix, w_in, cmp_wk, cmp_wv, cmp_pek, cmp_pev, hg_lb, hg_norm, w_out, norm_ffn, peer_wq, peer_keys, peer_u, peer_v, norm_final):
    b, t, d = x_prompt.shape
    db, tq, _ = x_sample.shape
    n_pool = cache_cmp.shape[1]
    n_pages = page_table.shape[1]
    past = n_pages * PAGE_ROWS
    wb = cache_win.shape[2]
    row = lambda a: a.reshape(1, -1)

    w_all = _prep_w_in(w_in[0])
    cw = _prep_cmp(cmp_wk[0], cmp_wv[0], cmp_pek[0], cmp_pev[0])
    w_out_b = w_out[0].astype(BF16)
    wq_b = peer_wq[0].astype(BF16)
    u_b = peer_u[0].astype(BF16)
    v_b = peer_v[0].astype(BF16)
    tail = lambda x, attn, ohg, tmd: _ffn_tail(x, attn, ohg, w_out_b, row(norm_ffn[0]), wq_b, peer_keys[0], u_b, v_b,
                                               row(norm_final), 512, 256, tmd, 512)

    cos, sin = _rope_tables(jnp.arange(t, dtype=jnp.int32))
    xp = x_prompt.reshape(b * t, d)
    q, qr, cmp_p, slc_p, win_p, slcb, winb, gates, hp = _inproj(xp, row(norm_mix[0]), w_all, cos, sin, 512)
    kcvc = _compress_prompt(cmp_p.reshape(b, t, 256), cw)
    ncp = t // CMP_STRIDE
    attn_p = _nsa_prompt(q.reshape(b, t, -1), qr.reshape(b, t, -1), gates.reshape(b, t, -1), kcvc,
                         slcb.reshape(b, t, 256), winb.reshape(b, t, 256),
                         _overlap_matrix(ncp, ncp - 1).T, _block_expand_matrix(t), 128)
    ohg_p, s_p = _hgrn(hp, jnp.zeros((b, HG_HEADS, HG_DK, HG_DV), F32), hg_lb, row(hg_norm[0]), b, t, 4, HG_ROWS)
    y_p = tail(xp, attn_p.reshape(b * t, -1), ohg_p, 512)

    pos_s = past + (jnp.arange(db * tq, dtype=jnp.int32) % tq)
    cos_s, sin_s = _rope_tables(pos_s)
    xs = x_sample.reshape(db * tq, d)
    q_s, qr_s, cmp_s, slc_s, win_s, _, _, gates_s, hs = _inproj(xs, row(norm_mix[0]), w_all, cos_s, sin_s, 256)
    kcvc_s = _compress_paged(cache_cmp[0].reshape(n_pool, PAGE_ROWS, 256), page_table, cw)
    ncs = past // CMP_STRIDE
    attn_s, nwin_s = _nsa_sample(q_s, qr_s, gates_s, kcvc_s, cache_slc[0].reshape(n_pool, PAGE_ROWS, 256), slc_s,
                                 cache_win[0].reshape(db, wb, 256), win_s, page_table,
                                 _overlap_matrix(ncs, ncs - 1), _block_expand_matrix(past + KEY_PAD), tq)
    ohg_s, s_s = _hgrn(hs, state_hgrn[0], hg_lb, row(hg_norm[0]), db, tq, 1, tq)
    y_s = tail(xs, attn_s, ohg_s, 512)

    kv5 = lambda a, nb, nt: a.reshape(1, nb, nt, 2, NSA_KV_HEADS, HEAD_DIM)
    keep = min(WINDOW, t)
    return (y_p.reshape(b, t, d), y_s.reshape(db, tq, d),
            kv5(cmp_p, b, t), kv5(slc_p, b, t), kv5(win_p.reshape(b, t, 256)[:, t - keep:], b, keep), s_p[None],
            kv5(cmp_s, db, tq), kv5(slc_s, db, tq), kv5(nwin_s, db, wb), s_s[None])
```

```python
import functools

import numpy as np
import jax
import jax.numpy as jnp
from jax import lax
from jax.experimental import pallas as pl
from jax.experimental.pallas import tpu as pltpu

F32 = jnp.float32
BF16 = jnp.bfloat16

D_MODEL = 1024
HEAD_DIM = 64
NSA_HEADS = 8
NSA_KV_HEADS = 2
NSA_GROUP = NSA_HEADS // NSA_KV_HEADS
CMP_BLOCK = 32
CMP_STRIDE = 16
SLC_BLOCK = 64
SLC_TOPN = 16
WINDOW = 512
ROPE_THETA = 10000.0
HG_HEADS = 4
HG_DK = 128
HG_DV = 128
NSA_WIDTH = NSA_HEADS * HEAD_DIM
HG_WIDTH = HG_HEADS * HG_DV
KV_WIDTH = NSA_KV_HEADS * HEAD_DIM
PEER_HEADS = 8
PEER_KEYS = 128
PEER_QDIM = 256
PEER_TOPK = 16
RMS_EPS = 1e-6
NEG_INF = -1e30
FORCED_SCORE = 1e6
LOWEST = -3e38

LANES = 128
HALF = 64
VMEM_LIMIT = 56 * 1024 * 1024

COL_Q = 0
COL_KV = 512
COL_H = 1280
COL_G = 3328
PROJ_PAD = 3456


def _dot(a, b):
    return jnp.dot(a, b, preferred_element_type=F32)


def _dot_nt(a, b):
    return lax.dot_general(a, b, (((1,), (1,)), ((), ())), preferred_element_type=F32)


def _iota(shape, dim):
    return lax.broadcasted_iota(jnp.int32, shape, dim)


def _params(*sem):
    return pltpu.CompilerParams(dimension_semantics=sem, vmem_limit_bytes=VMEM_LIMIT)


def _rope_tile(x, cos, sin_signed, first_half):
    partner = jnp.where(first_half, pltpu.roll(x, LANES - 32, 1), pltpu.roll(x, 32, 1))
    return x * cos + partner * sin_signed


def _inproj_seq_kernel(x_ref, g_ref, w_ref, cos_ref, sin_ref,
                       q_ref, qr_ref, cmp_ref, cmpt_ref, slct_ref, wint_ref, ksb_ref, kwb_ref, vst_ref, vwt_ref,
                       gates_ref, h_ref):
    x = x_ref[...]
    ms = jnp.mean(x * x, axis=-1, keepdims=True)
    xn = (x * lax.rsqrt(ms + RMS_EPS)) * g_ref[...]
    proj = _dot(xn.astype(BF16), w_ref[...])
    cos = cos_ref[...]
    sin = sin_ref[...]
    first_half = (_iota(cos.shape, 1) & (HALF - 1)) < 32
    rope = lambda t: _rope_tile(t, cos, sin, first_half)
    q_ref[...] = proj[:, COL_Q:COL_Q + NSA_WIDTH]
    for c in range(NSA_WIDTH // LANES):
        qr_ref[:, c * LANES:(c + 1) * LANES] = rope(proj[:, COL_Q + c * LANES:COL_Q + (c + 1) * LANES])
    cmp = proj[:, COL_KV:COL_KV + 256]
    cmp_ref[...] = cmp
    cmpt_ref[...] = cmp.T
    ks = rope(proj[:, COL_KV + 256:COL_KV + 384])
    kw = rope(proj[:, COL_KV + 512:COL_KV + 640])
    vs_t = proj[:, COL_KV + 384:COL_KV + 512].T
    vw_t = proj[:, COL_KV + 640:COL_KV + 768].T
    slct_ref[0:LANES, :] = ks.T
    slct_ref[LANES:2 * LANES, :] = vs_t
    wint_ref[0:LANES, :] = kw.T
    wint_ref[LANES:2 * LANES, :] = vw_t
    ksb_ref[...] = ks.astype(BF16)
    kwb_ref[...] = kw.astype(BF16)
    for c in range(vst_ref.shape[0]):
        vst_ref[c] = vs_t[:, c * KEY_TILE:(c + 1) * KEY_TILE].astype(BF16)
    for c in range(vwt_ref.shape[0]):
        vwt_ref[c] = vw_t[:, c * LANES:(c + 1) * LANES].astype(BF16)
    gates_ref[...] = jax.nn.sigmoid(proj[:, COL_G:COL_G + LANES])
    h_ref[...] = proj[:, COL_H:COL_H + 4 * HG_WIDTH]


def _inproj_seq(x, norm_g, w_all, cos_t, sin_t, nb, t, tm):
    n = nb * t
    steps = t // tm
    row = lambda w: pl.BlockSpec((tm, w), lambda i: (i, 0))
    tab = pl.BlockSpec((tm, LANES), lambda i: (i % steps, 0))
    feat = pl.BlockSpec((None, 256, tm), lambda i: (i // steps, 0, i % steps))
    tiles = lambda w: pl.BlockSpec((tm // w, LANES, w), lambda i: (i, 0, 0))
    f32 = lambda *s: jax.ShapeDtypeStruct(s, F32)
    bf = lambda *s: jax.ShapeDtypeStruct(s, BF16)
    return pl.pallas_call(
        _inproj_seq_kernel,
        grid=(n // tm,),
        in_specs=[row(D_MODEL), pl.BlockSpec((1, D_MODEL), lambda i: (0, 0)),
                  pl.BlockSpec((D_MODEL, PROJ_PAD), lambda i: (0, 0)), tab, tab],
        out_specs=[row(NSA_WIDTH), row(NSA_WIDTH), row(256), feat, feat, feat, row(LANES), row(LANES),
                   tiles(KEY_TILE), tiles(LANES), row(LANES), row(4 * HG_WIDTH)],
        out_shape=[f32(n, NSA_WIDTH), f32(n, NSA_WIDTH), f32(n, 256), f32(nb, 256, t), f32(nb, 256, t),
                   f32(nb, 256, t), bf(n, LANES), bf(n, LANES), bf(n // KEY_TILE, LANES, KEY_TILE),
                   bf(n // LANES, LANES, LANES), f32(n, LANES), f32(n, 4 * HG_WIDTH)],
        compiler_params=_params("parallel"),
        name="inproj_seq",
    )(x, norm_g, w_all, cos_t, sin_t)


def _inproj_kernel(x_ref, g_ref, w_ref, cos_ref, sin_ref,
                   q_ref, qr_ref, cmp_ref, slc_ref, win_ref, slcb_ref, winb_ref, gates_ref, h_ref):
    x = x_ref[...]
    ms = jnp.mean(x * x, axis=-1, keepdims=True)
    xn = (x * lax.rsqrt(ms + RMS_EPS)) * g_ref[...]
    proj = _dot(xn.astype(BF16), w_ref[...])
    cos = cos_ref[...]
    sin = sin_ref[...]
    first_half = (_iota(cos.shape, 1) & (HALF - 1)) < 32
    rope = lambda t: _rope_tile(t, cos, sin, first_half)
    q_ref[...] = proj[:, COL_Q:COL_Q + NSA_WIDTH]
    for c in range(NSA_WIDTH // LANES):
        qr_ref[:, c * LANES:(c + 1) * LANES] = rope(proj[:, COL_Q + c * LANES:COL_Q + (c + 1) * LANES])
    cmp_ref[...] = proj[:, COL_KV:COL_KV + 256]
    ks = rope(proj[:, COL_KV + 256:COL_KV + 384])
    vs = proj[:, COL_KV + 384:COL_KV + 512]
    kw = rope(proj[:, COL_KV + 512:COL_KV + 640])
    vw = proj[:, COL_KV + 640:COL_KV + 768]
    slc_ref[:, 0:LANES] = ks
    slc_ref[:, LANES:2 * LANES] = vs
    win_ref[:, 0:LANES] = kw
    win_ref[:, LANES:2 * LANES] = vw
    slcb_ref[:, 0:LANES] = ks.astype(BF16)
    slcb_ref[:, LANES:2 * LANES] = vs.astype(BF16)
    winb_ref[:, 0:LANES] = kw.astype(BF16)
    winb_ref[:, LANES:2 * LANES] = vw.astype(BF16)
    gates_ref[...] = jax.nn.sigmoid(proj[:, COL_G:COL_G + LANES])
    h_ref[...] = proj[:, COL_H:COL_H + 4 * HG_WIDTH]


def _inproj(x, norm_g, w_all, cos_t, sin_t, tm):
    n = x.shape[0]
    ntab = cos_t.shape[0] // tm
    row = lambda w: pl.BlockSpec((tm, w), lambda i: (i, 0))
    tab = pl.BlockSpec((tm, LANES), lambda i: (i % ntab, 0))
    out_shapes = [
        jax.ShapeDtypeStruct((n, NSA_WIDTH), F32), jax.ShapeDtypeStruct((n, NSA_WIDTH), F32),
        jax.ShapeDtypeStruct((n, 256), F32), jax.ShapeDtypeStruct((n, 256), F32),
        jax.ShapeDtypeStruct((n, 256), F32), jax.ShapeDtypeStruct((n, 256), BF16),
        jax.ShapeDtypeStruct((n, 256), BF16), jax.ShapeDtypeStruct((n, LANES), F32),
        jax.ShapeDtypeStruct((n, 4 * HG_WIDTH), F32),
    ]
    return pl.pallas_call(
        _inproj_kernel,
        grid=(n // tm,),
        in_specs=[row(D_MODEL), pl.BlockSpec((1, D_MODEL), lambda i: (0, 0)),
                  pl.BlockSpec((D_MODEL, PROJ_PAD), lambda i: (0, 0)), tab, tab],
        out_specs=[row(NSA_WIDTH), row(NSA_WIDTH), row(256), row(256), row(256), row(256), row(256),
                   row(LANES), row(4 * HG_WIDTH)],
        out_shape=out_shapes,
        compiler_params=_params("parallel"),
        name="inproj",
    )(x, norm_g, w_all, cos_t, sin_t)


def _rope_tables(pos):
    half = HEAD_DIM // 2
    inv = ROPE_THETA ** (-jnp.arange(half, dtype=F32) / half)
    ang = pos.astype(F32)[:, None] * inv[None, :]
    cos = jnp.tile(jnp.cos(ang), (1, 4))
    sin = jnp.sin(ang)
    return cos, jnp.tile(jnp.concatenate([-sin, sin], axis=1), (1, 2))


def _compress_rows(rows_ref, n_out, w_ref, pe_ref):
    a = jnp.zeros((n_out, LANES), F32)
    b = jnp.zeros((n_out, LANES), F32)
    for j in range(CMP_STRIDE):
        xj = rows_ref[pl.ds(j, n_out, stride=CMP_STRIDE), :]
        a = a + _dot((xj + pe_ref[j:j + 1, :]).astype(BF16), w_ref[j])
        b = b + _dot((xj + pe_ref[CMP_STRIDE + j:CMP_STRIDE + j + 1, :]).astype(BF16), w_ref[CMP_STRIDE + j])
    return a + pltpu.roll(b, n_out - 1, 0)


def _compress_prompt_kernel(rk_ref, rv_ref, wk_ref, wv_ref, pek_ref, pev_ref, out_ref):
    n_out = out_ref.shape[0]
    out_ref[:, 0:LANES] = _compress_rows(rk_ref, n_out, wk_ref, pek_ref).astype(BF16)
    out_ref[:, LANES:2 * LANES] = _compress_rows(rv_ref, n_out, wv_ref, pev_ref).astype(BF16)


def _compress_prompt(rows, cw):
    b, t, _ = rows.shape
    n_out = t // CMP_STRIDE
    wspec = pl.BlockSpec((CMP_BLOCK, LANES, LANES), lambda i: (0, 0, 0))
    pspec = pl.BlockSpec((CMP_BLOCK, LANES), lambda i: (0, 0))
    return pl.pallas_call(
        _compress_prompt_kernel,
        grid=(b,),
        in_specs=[pl.BlockSpec((None, t, LANES), lambda i: (i, 0, 0)),
                  pl.BlockSpec((None, t, LANES), lambda i: (i, 0, 1)), wspec, wspec, pspec, pspec],
        out_specs=pl.BlockSpec((None, n_out, 256), lambda i: (i, 0, 0)),
        out_shape=jax.ShapeDtypeStruct((b, n_out, 256), BF16),
        compiler_params=_params("parallel"),
        name="compress_prompt",
    )(rows, rows, *cw)


def _stack_heads(ref, lo_half, nq):
    hi_half = jnp.logical_not(lo_half)
    parts = []
    for h in range(NSA_HEADS):
        g = h // NSA_GROUP
        tile = ref[:, LANES * (h // 2):LANES * (h // 2 + 1)]
        if h % 2 != g:
            tile = pltpu.roll(tile, HALF, 1)
        parts.append(jnp.where(lo_half if g == 0 else hi_half, tile, 0.0))
    return (jnp.concatenate(parts, axis=0) * (HEAD_DIM ** -0.5)).astype(BF16)


def _per_head(x, nq):
    return jnp.concatenate([x[0:nq]] * NSA_GROUP + [x[nq:2 * nq]] * NSA_GROUP, axis=0)


def _masked_softmax_rows(s, valid):
    sm = jnp.where(valid, s, NEG_INF)
    m = jnp.max(sm, axis=-1, keepdims=True)
    e = jnp.where(valid, jnp.exp(sm - m), 0.0)
    l = jnp.sum(e, axis=-1, keepdims=True)
    return e / jnp.where(l > 0.0, l, 1.0)


def _select_blocks(imp, pos_col):
    blk = _iota(imp.shape, 1)
    cur = pos_col >> 6
    forced = (blk == 0) | (blk == cur) | (blk == cur - 1)
    causal = (blk << 6) <= pos_col
    score = jnp.where(causal, jnp.where(forced, FORCED_SCORE, imp), -FORCED_SCORE)
    return score, causal


def _topk_mask_t(score_t, n_sel):
    bi = _iota(score_t.shape, 0)
    sel = jnp.zeros(score_t.shape, F32)
    sc = score_t
    for _ in range(n_sel):
        m = jnp.max(sc, axis=0, keepdims=True)
        idx = jnp.min(jnp.where(sc == m, bi, LANES), axis=0, keepdims=True)
        hit = bi == idx
        sel = jnp.where(hit, 1.0, sel)
        sc = jnp.where(hit, LOWEST, sc)
    return sel


def _cmp_branch(qn, kc, vc, ov, pos_rows, pos_grp, nq):
    s = _dot_nt(qn, kc)
    valid = (_iota(s.shape, 1) * CMP_STRIDE + (CMP_BLOCK - 1)) <= pos_rows
    p = _masked_softmax_rows(s, valid)
    o_c = _dot(p.astype(BF16), vc)
    slab = lambda h: p[h * nq:(h + 1) * nq]
    psum = jnp.concatenate([slab(0) + slab(1) + slab(2) + slab(3), slab(4) + slab(5) + slab(6) + slab(7)], axis=0)
    hi = psum.astype(BF16)
    lo = (psum - hi.astype(F32)).astype(BF16)
    imp = _dot(hi, ov) + _dot(lo, ov)
    score, causal = _select_blocks(imp, pos_grp)
    return o_c, score, causal


def _combine_heads(out_ref, gates, lo_half, o_c, o_s, o_w, nq):
    combs = []
    for h in range(NSA_HEADS):
        rows = slice(h * nq, (h + 1) * nq)
        comb = (gates[:, h:h + 1] * o_c[rows] + gates[:, NSA_HEADS + h:NSA_HEADS + h + 1] * o_s[rows]
                + gates[:, 2 * NSA_HEADS + h:2 * NSA_HEADS + h + 1] * o_w[rows])
        if h % 2 != h // NSA_GROUP:
            comb = pltpu.roll(comb, HALF, 1)
        combs.append(comb)
    for r in range(NSA_HEADS // 2):
        out_ref[:, LANES * r:LANES * (r + 1)] = jnp.where(lo_half, combs[2 * r], combs[2 * r + 1])


KEY_TILE = 256


def _masked_softmax_cols(s, valid):
    sm = jnp.where(valid, s, NEG_INF)
    m = jnp.max(sm, axis=0, keepdims=True)
    e = jnp.where(valid, jnp.exp(sm - m), 0.0)
    l = jnp.sum(e, axis=0, keepdims=True)
    return e / jnp.where(l > 0.0, l, 1.0)


def _nsa_prompt_kernel(q_ref, qr_ref, gates_ref, cmp_ref, vct_ref, ks_ref, vst_ref, kw_ref, vwt_ref, ovt_ref, et_ref,
                       out_ref, *, tq):
    t0 = pl.program_id(1) * tq
    cols = NSA_HEADS * tq
    grp = NSA_KV_HEADS * tq
    pos_cols = t0 + (_iota((1, cols), 1) & (tq - 1))
    pos_grp = t0 + (_iota((1, grp), 1) & (tq - 1))
    n_tiles = (t0 + tq + KEY_TILE - 1) // KEY_TILE
    w_chunk = jnp.maximum(t0 - WINDOW, 0) // LANES
    w_start = pl.multiple_of(w_chunk * LANES, LANES)
    w_len = WINDOW + tq
    per_head = lambda x: jnp.concatenate([x[:, 0:tq]] * NSA_GROUP + [x[:, tq:grp]] * NSA_GROUP, axis=1)

    lo_half = _iota((tq, LANES), 1) < HALF
    qn = _stack_heads(q_ref, lo_half, tq)
    qr = _stack_heads(qr_ref, lo_half, tq)

    all_heads = lambda x: jnp.concatenate([x] * NSA_HEADS, axis=1)
    pos_tok = t0 + _iota((1, tq), 1)

    sc = _dot_nt(cmp_ref[:, 0:LANES], qn)
    n_i = _iota((sc.shape[0], tq), 0)
    sc = sc + all_heads(jnp.where(n_i * CMP_STRIDE + (CMP_BLOCK - 1) <= pos_tok, 0.0, NEG_INF))
    ec = jnp.exp(sc - jnp.max(sc, axis=0, keepdims=True))
    norm_c = jnp.where(pos_cols >= CMP_BLOCK - 1, 1.0 / jnp.sum(ec, axis=0, keepdims=True), 0.0)
    o_c = _dot(vct_ref[...], ec.astype(BF16)) * norm_c
    pc = ec * norm_c
    slab = lambda h: pc[:, h * tq:(h + 1) * tq]
    psum = jnp.concatenate([slab(0) + slab(1) + slab(2) + slab(3), slab(4) + slab(5) + slab(6) + slab(7)], axis=1)
    hi = psum.astype(BF16)
    lo = (psum - hi.astype(F32)).astype(BF16)
    ovt = ovt_ref[...]
    imp = _dot(ovt, hi) + _dot(ovt, lo)
    blk = _iota(imp.shape, 0)
    cur = pos_grp >> 6
    forced = (blk == 0) | (blk == cur) | (blk == cur - 1)
    causal = (blk << 6) <= pos_grp
    score = jnp.where(causal, jnp.where(forced, FORCED_SCORE, imp), -FORCED_SCORE)
    sel = jnp.where(causal, _topk_mask_t(score, SLC_TOPN), 0.0)
    notsel = (1.0 - sel).astype(BF16)

    sw = _dot_nt(kw_ref[pl.ds(w_start, w_len), :], qr)
    kp = w_start + _iota((w_len, tq), 0)
    sw = sw + all_heads(jnp.where((kp <= pos_tok) & (kp >= pos_tok - WINDOW), 0.0, NEG_INF))
    ew = jnp.exp(sw - jnp.max(sw, axis=0, keepdims=True))
    vwt = jnp.concatenate([vwt_ref[w_chunk + c] for c in range(w_len // LANES)], axis=1)
    o_w = _dot(vwt, ew.astype(BF16)) / jnp.sum(ew, axis=0, keepdims=True)

    def tile_step(kt, carry, diag):
        m, l, acc = carry
        k0 = pl.multiple_of(kt * KEY_TILE, KEY_TILE)
        s = _dot_nt(ks_ref[pl.ds(k0, KEY_TILE), :], qr)
        s = s + per_head(_dot(et_ref[pl.ds(k0, KEY_TILE), :], notsel) * NEG_INF)
        if diag:
            s = s + all_heads(jnp.where(k0 + _iota((KEY_TILE, tq), 0) <= pos_tok, 0.0, NEG_INF))
        m_new = jnp.maximum(m, jnp.max(s, axis=0, keepdims=True))
        alpha = jnp.exp(m - m_new)
        p = jnp.exp(s - m_new)
        l = alpha * l + jnp.sum(p, axis=0, keepdims=True)
        acc = alpha * acc + _dot(vst_ref[kt], p.astype(BF16))
        return m_new, l, acc

    init = (jnp.full((1, cols), NEG_INF, F32), jnp.zeros((1, cols), F32), jnp.zeros((LANES, cols), F32))
    carry = lax.fori_loop(0, n_tiles - 1, lambda kt, c: tile_step(kt, c, False), init)
    _, l, acc = tile_step(n_tiles - 1, carry, True)
    o_s = acc / l

    gt = gates_ref[...].T
    for r in range(NSA_HEADS // 2):
        halves = []
        for h in (2 * r, 2 * r + 1):
            c = slice(h * tq, (h + 1) * tq)
            comb = (gt[h:h + 1] * o_c[:, c] + gt[NSA_HEADS + h:NSA_HEADS + h + 1] * o_s[:, c]
                    + gt[2 * NSA_HEADS + h:2 * NSA_HEADS + h + 1] * o_w[:, c])
            g = h // NSA_GROUP
            halves.append(comb[HALF * g:HALF * (g + 1)])
        out_ref[:, LANES * r:LANES * (r + 1)] = jnp.concatenate(halves, axis=0).T


def _nsa_prompt(q, qr, gates, kcvc, ksb, vst, kwb, vwt, ovt, et, tq):
    b, t, _ = q.shape
    ncp = kcvc.shape[1]
    vct = kcvc[:, :, LANES:].transpose(0, 2, 1)
    blk = lambda w: pl.BlockSpec((None, tq, w), lambda bi, i: (bi, i, 0))
    full = lambda r, w: pl.BlockSpec((None, r, w), lambda bi, i: (bi, 0, 0))
    full4 = lambda a: pl.BlockSpec((None,) + a.shape[1:], lambda bi, i: (bi, 0, 0, 0))
    return pl.pallas_call(
        functools.partial(_nsa_prompt_kernel, tq=tq),
        grid=(b, t // tq),
        in_specs=[blk(NSA_WIDTH), blk(NSA_WIDTH), blk(LANES), full(ncp, 256), full(LANES, ncp), full(t, LANES),
                  full4(vst), full(t, LANES), full4(vwt),
                  pl.BlockSpec((LANES, ncp), lambda bi, i: (0, 0)),
                  pl.BlockSpec((t, LANES), lambda bi, i: (0, 0))],
        out_specs=blk(NSA_WIDTH),
        out_shape=jax.ShapeDtypeStruct((b, t, NSA_WIDTH), F32),
        compiler_params=_params("parallel", "arbitrary"),
        name="nsa_prompt",
    )(q, qr, gates, kcvc, vct, ksb, vst, kwb, vwt, ovt, et)


def _overlap_matrix(n_cmp_pad, n_cmp):
    c0 = jnp.arange(n_cmp_pad, dtype=jnp.int32)[:, None] * CMP_STRIDE
    s0 = jnp.arange(LANES, dtype=jnp.int32)[None, :] * SLC_BLOCK
    real = jnp.arange(n_cmp_pad, dtype=jnp.int32)[:, None] < n_cmp
    return ((c0 < s0 + SLC_BLOCK) & (c0 + CMP_BLOCK > s0) & real).astype(BF16)


def _block_expand_matrix(n_keys):
    r = jnp.arange(n_keys, dtype=jnp.int32)[:, None] // SLC_BLOCK
    return (r == jnp.arange(LANES, dtype=jnp.int32)[None, :]).astype(BF16)


PAGE_ROWS = 128


def _compress_paged_kernel(pt_ref, *refs, n_pages):
    pages = refs[:n_pages]
    wk_ref, wv_ref, pek_ref, pev_ref, out_ref, rk_scr, rv_scr = refs[n_pages:]
    for p in range(n_pages):
        rk_scr[p * PAGE_ROWS:(p + 1) * PAGE_ROWS, :] = pages[p][0:LANES, :].T
        rv_scr[p * PAGE_ROWS:(p + 1) * PAGE_ROWS, :] = pages[p][LANES:2 * LANES, :].T
    n_out = out_ref.shape[0]
    out_ref[:, 0:LANES] = _compress_rows(rk_scr, n_out, wk_ref, pek_ref).astype(BF16)
    out_ref[:, LANES:2 * LANES] = _compress_rows(rv_scr, n_out, wv_ref, pev_ref).astype(BF16)


def _page_specs(n_pages):
    return [pl.BlockSpec((None, 256, PAGE_ROWS), functools.partial(lambda i, pt, p: (pt[i * n_pages + p], 0, 0), p=p))
            for p in range(n_pages)]


def _feature_major(cache, lead):
    return cache[0].reshape(lead, cache.shape[2], 256).transpose(0, 2, 1)


def _compress_paged(cache, page_table, cw):
    db, n_pages = page_table.shape
    past = n_pages * PAGE_ROWS
    n_out = past // CMP_STRIDE
    wspec = pl.BlockSpec((CMP_BLOCK, LANES, LANES), lambda i, pt: (0, 0, 0))
    pspec = pl.BlockSpec((CMP_BLOCK, LANES), lambda i, pt: (0, 0))
    grid_spec = pltpu.PrefetchScalarGridSpec(
        num_scalar_prefetch=1, grid=(db,),
        in_specs=_page_specs(n_pages) + [wspec, wspec, pspec, pspec],
        out_specs=pl.BlockSpec((None, n_out, 256), lambda i, pt: (i, 0, 0)),
        scratch_shapes=[pltpu.VMEM((past, LANES), F32), pltpu.VMEM((past, LANES), F32)])
    return pl.pallas_call(
        functools.partial(_compress_paged_kernel, n_pages=n_pages),
        grid_spec=grid_spec,
        out_shape=jax.ShapeDtypeStruct((db, n_out, 256), BF16),
        compiler_params=_params("arbitrary"),
        name="compress_paged",
    )(page_table.reshape(-1), *([cache] * n_pages), *cw)


KEY_PAD = LANES


def _nsa_sample_kernel(pt_ref, q_ref, qr_ref, gates_ref, cmp_ref, *refs, n_pages):
    pages = refs[:n_pages]
    snew_ref, cwin_ref, wnew_ref, ov_ref, et_ref, out_ref, nwin_ref = refs[n_pages:]
    nq = q_ref.shape[0]
    past = n_pages * PAGE_ROWS
    wb = cwin_ref.shape[1]
    rows = NSA_HEADS * nq
    pad_rows = lambda x: jnp.concatenate([x, jnp.zeros((KEY_PAD - nq, x.shape[1]), F32)], axis=0)
    snew = pad_rows(snew_ref[...])
    wnew = pad_rows(wnew_ref[...])

    shifted = pltpu.roll(cwin_ref[...], wb - nq, 1)
    new_t = jnp.concatenate([wnew[:, 0:LANES].T, wnew[:, LANES:2 * LANES].T], axis=0)
    tail = jnp.where(_iota((256, LANES), 1) >= LANES - nq, pltpu.roll(new_t, LANES - nq, 1), shifted[:, wb - LANES:wb])
    nwin_ref[:, 0:wb - LANES] = shifted[:, 0:wb - LANES]
    nwin_ref[:, wb - LANES:wb] = tail

    lo_half = _iota((nq, LANES), 1) < HALF
    pos_rows = past + (_iota((rows, 1), 0) & (nq - 1))
    n_grp = NSA_KV_HEADS * nq
    pos_grp = past + (_iota((n_grp, 1), 0) & (nq - 1))
    qn = _stack_heads(q_ref, lo_half, nq)
    qr = _stack_heads(qr_ref, lo_half, nq)
    o_c, score, causal = _cmp_branch(qn, cmp_ref[:, 0:LANES], cmp_ref[:, LANES:2 * LANES], ov_ref[...],
                                     pos_rows, pos_grp, nq)
    score_sq = jnp.concatenate([score, jnp.full((LANES - n_grp, LANES), LOWEST, F32)], axis=0)
    sel = jnp.where(causal, _topk_mask_t(score_sq.T, SLC_TOPN).T[0:n_grp], 0.0)
    notsel = (1.0 - sel).astype(BF16)

    s = jnp.concatenate([_dot(qr, pages[p][0:LANES, :].astype(BF16)) for p in range(n_pages)]
                        + [_dot_nt(qr, snew[:, 0:LANES].astype(BF16))], axis=1)
    s = s + _per_head(_dot_nt(notsel, et_ref[...]) * NEG_INF, nq)
    s = jnp.where(_iota(s.shape, 1) <= pos_rows, s, NEG_INF)
    e = jnp.exp(s - jnp.max(s, axis=-1, keepdims=True))
    eb = e.astype(BF16)
    o_s = _dot(eb[:, past:past + KEY_PAD], snew[:, LANES:2 * LANES].astype(BF16))
    for p in range(n_pages):
        o_s = o_s + _dot_nt(eb[:, p * PAGE_ROWS:(p + 1) * PAGE_ROWS], pages[p][LANES:2 * LANES, :].astype(BF16))
    o_s = o_s / jnp.sum(e, axis=-1, keepdims=True)

    sw = jnp.concatenate([_dot(qr, cwin_ref[0:LANES, :].astype(BF16)), _dot_nt(qr, wnew[:, 0:LANES].astype(BF16))], axis=1)
    kp = (past - wb) + _iota(sw.shape, 1)
    pw = _masked_softmax_rows(sw, (kp <= pos_rows) & (kp >= pos_rows - WINDOW) & (kp >= 0)).astype(BF16)
    o_w = (_dot_nt(pw[:, 0:wb], cwin_ref[LANES:2 * LANES, :].astype(BF16))
           + _dot(pw[:, wb:wb + KEY_PAD], wnew[:, LANES:2 * LANES].astype(BF16)))
    _combine_heads(out_ref, gates_ref[...], lo_half, o_c, o_s, o_w, nq)


def _nsa_sample(q, qr, gates, kcvc, cache_slc, slc_new, cache_win, win_new, page_table, ov, et, nq):
    db, n_pages = page_table.shape
    past = n_pages * PAGE_ROWS
    wb = cache_win.shape[2]
    ncp = kcvc.shape[1]
    blk = lambda w: pl.BlockSpec((nq, w), lambda i, pt: (i, 0))
    win_spec = pl.BlockSpec((None, 256, wb), lambda i, pt: (i, 0, 0))
    grid_spec = pltpu.PrefetchScalarGridSpec(
        num_scalar_prefetch=1, grid=(db,),
        in_specs=[blk(NSA_WIDTH), blk(NSA_WIDTH), blk(LANES),
                  pl.BlockSpec((None, ncp, 256), lambda i, pt: (i, 0, 0))] + _page_specs(n_pages) + [
                  blk(256), win_spec, blk(256),
                  pl.BlockSpec((ncp, LANES), lambda i, pt: (0, 0)),
                  pl.BlockSpec((past + KEY_PAD, LANES), lambda i, pt: (0, 0))],
        out_specs=[blk(NSA_WIDTH), win_spec])
    return pl.pallas_call(
        functools.partial(_nsa_sample_kernel, n_pages=n_pages),
        grid_spec=grid_spec,
        out_shape=[jax.ShapeDtypeStruct((db * nq, NSA_WIDTH), F32), jax.ShapeDtypeStruct((db, 256, wb), F32)],
        compiler_params=_params("arbitrary"),
        name="nsa_sample",
    )(page_table.reshape(-1), q, qr, gates, kcvc, *([cache_slc] * n_pages), slc_new, cache_win, win_new, ov, et)


HG_ROWS = 128
HG_SUB = 16


def _dot_split3(m_bf16, x):
    a = x.astype(BF16)
    r = x - a.astype(F32)
    b = r.astype(BF16)
    c = (r - b.astype(F32)).astype(BF16)
    return _dot(m_bf16, a) + _dot(m_bf16, b) + _dot(m_bf16, c)


def _hgrn_chunk(q, fpre, v, gpre, lb, ng, s0, n_real):
    f = lb + (1.0 - lb) * jax.nn.sigmoid(fpre)
    logf = jnp.log(f)
    kk = 1.0 - f
    if n_real < HG_ROWS:
        pad = lambda a: jnp.concatenate([a, jnp.zeros((HG_ROWS - n_real, LANES), F32)], axis=0)
        q, logf, kk, v = pad(q), pad(logf), pad(kk), pad(v)
    r_i = _iota((HG_ROWS, HG_ROWS), 0)
    c_i = _iota((HG_ROWS, HG_ROWS), 1)
    tri_b = r_i >= c_i
    tri = jnp.where(tri_b, 1.0, 0.0).astype(BF16)
    cum = _dot_split3(tri, logf)
    cprev = cum - logf
    cum_last = cum[HG_ROWS - 1:HG_ROWS, :]
    s0b = s0.astype(BF16)
    o = _dot((q * jnp.exp(cum)).astype(BF16), s0b)
    row = _iota((HG_ROWS, LANES), 0)
    a_rows = []
    n_sub = -(-n_real // HG_SUB)
    for i in range(n_sub):
        r0 = i * HG_SUB
        c_ref = cprev[r0:r0 + 1, :]
        qt = q[r0:r0 + HG_SUB] * jnp.exp(cum[r0:r0 + HG_SUB] - c_ref)
        kt = kk * jnp.exp(jnp.where(row < r0 + HG_SUB, c_ref - cum, NEG_INF))
        a_rows.append(_dot_nt(qt.astype(BF16), kt.astype(BF16)))
    if n_sub * HG_SUB < HG_ROWS:
        a_rows.append(jnp.zeros((HG_ROWS - n_sub * HG_SUB, HG_ROWS), F32))
    a = jnp.where(tri_b, jnp.concatenate(a_rows, axis=0), 0.0)
    vb = v.astype(BF16)
    o = o + _dot(a.astype(BF16), vb)
    kdec = kk * jnp.exp(cum_last - cum)
    scale = jnp.broadcast_to(jnp.exp(cum_last), (HG_ROWS, LANES)).T
    s_new = scale * s0 + _dot(kdec.T.astype(BF16), vb)
    o = o[0:n_real]
    o = o * lax.rsqrt(jnp.mean(o * o, axis=-1, keepdims=True) + RMS_EPS) * ng
    return o * jax.nn.sigmoid(gpre), s_new


def _hgrn_kernel(q_ref, f_ref, v_ref, g_ref, s0_ref, lb_ref, ng_ref, o_ref, sfin_ref, s_scr, *, n_chunks, n_real):
    @pl.when(pl.program_id(1) == 0)
    def _init():
        s_scr[...] = s0_ref[...]

    lbp = lb_ref[...]
    e = jnp.exp(lbp - jnp.max(lbp, axis=0, keepdims=True))
    lb = e[0:1, :] / jnp.sum(e, axis=0, keepdims=True)
    ng = ng_ref[...]

    def chunk(c, carry):
        rows = slice(0, n_real) if n_chunks == 1 else pl.ds(pl.multiple_of(c * n_real, n_real), n_real)
        for hd in range(HG_HEADS):
            cols = slice(hd * LANES, (hd + 1) * LANES)
            o, s_new = _hgrn_chunk(q_ref[rows, cols], f_ref[rows, cols], v_ref[rows, cols], g_ref[rows, cols],
                                   lb[:, cols], ng[:, cols], s_scr[hd], n_real)
            o_ref[rows, cols] = o
            s_scr[hd] = s_new
        return carry

    if n_chunks == 1:
        chunk(0, 0)
    else:
        lax.fori_loop(0, n_chunks, chunk, 0)
    sfin_ref[...] = s_scr[...]


def _hgrn(h, s0, hg_lb, hg_norm, nb, t, n_chunks, n_real):
    rows = n_chunks * n_real
    steps = t // rows
    col = lambda sec: pl.BlockSpec((rows, HG_WIDTH), lambda b, j: (b * steps + j, sec))
    st = pl.BlockSpec((None, HG_HEADS, HG_DK, HG_DV), lambda b, j: (b, 0, 0, 0))
    return pl.pallas_call(
        functools.partial(_hgrn_kernel, n_chunks=n_chunks, n_real=n_real),
        grid=(nb, steps),
        in_specs=[col(0), col(1), col(2), col(3), st,
                  pl.BlockSpec(hg_lb.shape, lambda b, j: (0, 0)),
                  pl.BlockSpec((1, HG_WIDTH), lambda b, j: (0, 0))],
        out_specs=[pl.BlockSpec((rows, HG_WIDTH), lambda b, j: (b * steps + j, 0)), st],
        out_shape=[jax.ShapeDtypeStruct((nb * t, HG_WIDTH), F32),
                   jax.ShapeDtypeStruct((nb, HG_HEADS, HG_DK, HG_DV), F32)],
        scratch_shapes=[pltpu.VMEM((HG_HEADS, HG_DK, HG_DV), F32)],
        compiler_params=_params("parallel", "arbitrary"),
        name="hgrn",
    )(h, h, h, h, s0, hg_lb, hg_norm)


def _outproj_kernel(x_ref, a_ref, hg_ref, wo_ref, nf_ref, wq_ref, x1_ref, xn_ref, pq_ref):
    mix = jnp.concatenate([a_ref[...], hg_ref[...]], axis=1).astype(BF16)
    x1 = x_ref[...] + _dot(mix, wo_ref[...])
    x1_ref[...] = x1
    ms = jnp.mean(x1 * x1, axis=-1, keepdims=True)
    xb = ((x1 * lax.rsqrt(ms + RMS_EPS)) * nf_ref[...]).astype(BF16)
    xn_ref[...] = xb
    pq_ref[...] = _dot(xb, wq_ref[...])


def _outproj(x, attn, ohg, w_out, norm_ffn, wq, tm):
    n = x.shape[0]
    row = lambda w: pl.BlockSpec((tm, w), lambda i: (i, 0))
    full = lambda a: pl.BlockSpec(a.shape, lambda i: (0, 0))
    nq = wq.shape[1]
    return pl.pallas_call(
        _outproj_kernel,
        grid=(n // tm,),
        in_specs=[row(D_MODEL), row(NSA_WIDTH), row(HG_WIDTH), full(w_out), full(norm_ffn), full(wq)],
        out_specs=[row(D_MODEL), row(D_MODEL), row(nq)],
        out_shape=[jax.ShapeDtypeStruct((n, D_MODEL), F32), jax.ShapeDtypeStruct((n, D_MODEL), BF16),
                   jax.ShapeDtypeStruct((n, nq), F32)],
        compiler_params=_params("parallel"),
        name="outproj",
    )(x, attn, ohg, w_out, norm_ffn, wq)


def _top_rows(s, k, val_scr, idx_scr):
    bi = _iota(s.shape, 0)
    big = s.shape[0]
    for a in range(k):
        m = jnp.max(s, axis=0, keepdims=True)
        idx = jnp.min(jnp.where(s == m, bi, big), axis=0, keepdims=True)
        val_scr[a:a + 1, :] = m
        idx_scr[a:a + 1, :] = idx
        s = jnp.where(bi == idx, LOWEST, s)


def _peer_select_kernel(pq_ref, keys_ref, i_ref, j_ref, g_ref, v12, i12, sc, cd):
    tms = pq_ref.shape[0]
    half = PEER_QDIM // 2
    s1 = _dot_nt(keys_ref[0].astype(BF16), pq_ref[:, 0:half].astype(BF16))
    s2 = _dot_nt(keys_ref[1].astype(BF16), pq_ref[:, half:2 * half].astype(BF16))
    _top_rows(jnp.concatenate([s1, s2], axis=1), PEER_TOPK, v12, i12)
    a1 = v12[:, 0:tms]
    a2 = v12[:, tms:2 * tms]
    r16 = _iota((PEER_TOPK, tms), 0)
    r8 = _iota((8, tms), 0)
    parts = [a1[0:1] + a2]
    codes = [r16]
    for a in range(1, 8):
        parts.append(a1[a:a + 1] + a2[0:8])
        codes.append(r8 + PEER_TOPK * a)
    parts.append(a1[8:16] + a2[0:1])
    codes.append((r8 + 8) * PEER_TOPK)
    cand = jnp.concatenate(parts, axis=0)
    code = jnp.concatenate(codes, axis=0)
    for k in range(PEER_TOPK):
        m = jnp.max(cand, axis=0, keepdims=True)
        cs = jnp.min(jnp.where(cand == m, code, PEER_TOPK * PEER_TOPK), axis=0, keepdims=True)
        sc[k:k + 1, :] = m
        cd[k:k + 1, :] = cs
        cand = jnp.where(code == cs, LOWEST, cand)
    scv = sc[...]
    cdv = cd[...]
    ak = cdv >> 4
    bk = cdv & (PEER_TOPK - 1)
    idx1 = i12[:, 0:tms]
    idx2 = i12[:, tms:2 * tms]
    ik = jnp.zeros((PEER_TOPK, tms), jnp.int32)
    jk = jnp.zeros((PEER_TOPK, tms), jnp.int32)
    for a in range(PEER_TOPK):
        ik = jnp.where(ak == a, idx1[a:a + 1], ik)
        jk = jnp.where(bk == a, idx2[a:a + 1], jk)
    e = jnp.exp(scv - scv[0:1])
    i_ref[...] = ik.astype(F32)
    j_ref[...] = jk.astype(F32)
    g_ref[...] = e / jnp.sum(e, axis=0, keepdims=True)


def _peer_select(pq, keys, tms):
    n = pq.shape[0]
    out = pl.BlockSpec((PEER_TOPK, tms), lambda i, h: (h, i))
    shp = jax.ShapeDtypeStruct((PEER_HEADS * PEER_TOPK, n), F32)
    return pl.pallas_call(
        _peer_select_kernel,
        grid=(n // tms, PEER_HEADS),
        in_specs=[pl.BlockSpec((tms, PEER_QDIM), lambda i, h: (i, h)),
                  pl.BlockSpec((None, 2, PEER_KEYS, PEER_QDIM // 2), lambda i, h: (h, 0, 0, 0))],
        out_specs=[out, out, out],
        out_shape=[shp, shp, shp],
        scratch_shapes=[pltpu.VMEM((PEER_TOPK, 2 * tms), F32), pltpu.VMEM((PEER_TOPK, 2 * tms), jnp.int32),
                        pltpu.VMEM((PEER_TOPK, tms), F32), pltpu.VMEM((PEER_TOPK, tms), jnp.int32)],
        compiler_params=_params("parallel", "arbitrary"),
        name="peer_select",
    )(pq, keys)


W_PITCH = PEER_KEYS + 8


def _peer_dense_kernel(xn_ref, ik_ref, jk_ref, gk_ref, u_ref, v_ref, x1_ref, nf_ref, out_ref, w_scr, acc_scr, *, tm, te):
    e_idx = pl.program_id(1)

    @pl.when(e_idx == 0)
    def _build():
        sub = _iota((PEER_KEYS, LANES), 0).astype(F32)

        def body(n, carry):
            irow = ik_ref[pl.ds(n, 1), :]
            jrow = jk_ref[pl.ds(n, 1), :]
            grow = gk_ref[pl.ds(n, 1), :]
            a = jnp.where(irow == sub, grow, 0.0).astype(BF16)
            bt = jnp.where(jrow == sub, 1.0, 0.0).astype(BF16)
            w_scr[pl.ds(pl.multiple_of(n * W_PITCH, 8), PEER_KEYS), :] = _dot_nt(a, bt)
            return carry

        lax.fori_loop(0, tm, body, 0, unroll=8)
        acc_scr[...] = jnp.zeros(acc_scr.shape, F32)

    h = _dot_nt(xn_ref[...], u_ref[...])
    ni = te // PEER_KEYS
    wt = jnp.concatenate([w_scr[pl.ds(e_idx * ni + ii, tm, stride=W_PITCH), :] for ii in range(ni)], axis=1)
    act = (jax.nn.gelu(h) * wt).astype(BF16)
    acc_scr[...] += _dot(act, v_ref[...])

    @pl.when(e_idx == pl.num_programs(1) - 1)
    def _finish():
        y = x1_ref[...] + acc_scr[...]
        ms = jnp.mean(y * y, axis=-1, keepdims=True)
        out_ref[...] = (y * lax.rsqrt(ms + RMS_EPS)) * nf_ref[...]


def _peer_dense(xn, ik, jk, gk, u, v, x1, norm_final, tm, te):
    n = xn.shape[0]
    n_exp = u.shape[0]
    row = lambda w: pl.BlockSpec((tm, w), lambda i, e: (i, 0), pipeline_mode=pl.Buffered(1))
    exp_spec = pl.BlockSpec((te, D_MODEL), lambda i, e: (e, 0))
    return pl.pallas_call(
        functools.partial(_peer_dense_kernel, tm=tm, te=te),
        grid=(n // tm, n_exp // te),
        in_specs=[row(D_MODEL), row(LANES), row(LANES), row(LANES), exp_spec, exp_spec, row(D_MODEL),
                  pl.BlockSpec((1, D_MODEL), lambda i, e: (0, 0))],
        out_specs=pl.BlockSpec((tm, D_MODEL), lambda i, e: (i, 0)),
        out_shape=jax.ShapeDtypeStruct((n, D_MODEL), F32),
        scratch_shapes=[pltpu.VMEM((tm * W_PITCH, LANES), F32), pltpu.VMEM((tm, D_MODEL), F32)],
        compiler_params=_params("parallel", "arbitrary"),
        name="peer_dense",
    )(xn, ik, jk, gk, u, v, x1, norm_final)


def _prep_w_in(w_in):
    w_main = jnp.concatenate([w_in[:, :COL_KV + 768], w_in[:, COL_KV + 768 + 3 * NSA_HEADS:]], axis=1)
    w_gate = jnp.pad(w_in[:, COL_KV + 768:COL_KV + 768 + 3 * NSA_HEADS], ((0, 0), (0, LANES - 3 * NSA_HEADS)))
    return jnp.concatenate([w_main, w_gate], axis=1).astype(BF16)


def _prep_cmp(cmp_wk, cmp_wv, cmp_pek, cmp_pev):
    def bd(w):
        z = jnp.zeros_like(w)
        return jnp.concatenate([jnp.concatenate([w, z], axis=2), jnp.concatenate([z, w], axis=2)], axis=1).astype(BF16)
    dup = lambda pe: jnp.concatenate([pe, pe], axis=1).astype(F32)
    return bd(cmp_wk), bd(cmp_wv), dup(cmp_pek), dup(cmp_pev)


def _ffn_tail(x, attn, ohg, w_out, norm_ffn, wq, keys, u, v, norm_final, tm, tms, tmd, te):
    x1, xn, pq = _outproj(x, attn, ohg, w_out, norm_ffn, wq, tm)
    ik, jk, gk = _peer_select(pq, keys, tms)
    return _peer_dense(xn, ik.T, jk.T, gk.T, u, v, x1, norm_final, tmd, te)


def kernel(x_prompt, x_sample, cache_cmp, cache_slc, cache_win, state_hgrn, page_table, norm_mix, w_in, cmp_wk, cmp_wv, cmp_pek, cmp_pev, hg_lb, hg_norm, w_out, norm_ffn, peer_wq, peer_keys, peer_u, peer_v, norm_final):
    b, t, d = x_prompt.shape
    db, tq, _ = x_sample.shape
    n_pool = cache_cmp.shape[1]
    n_pages = page_table.shape[1]
    past = n_pages * PAGE_ROWS
    wb = cache_win.shape[2]
    row = lambda a: a.reshape(1, -1)

    w_all = _prep_w_in(w_in[0])
    cw = _prep_cmp(cmp_wk[0], cmp_wv[0], cmp_pek[0], cmp_pev[0])
    w_out_b = w_out[0].astype(BF16)
    wq_b = peer_wq[0].astype(BF16)
    u_b = peer_u[0].astype(BF16)
    v_b = peer_v[0].astype(BF16)
    tail = lambda x, attn, ohg, tmd: _ffn_tail(x, attn, ohg, w_out_b, row(norm_ffn[0]), wq_b, peer_keys[0], u_b, v_b,
                                               row(norm_final), 512, 256, tmd, 512)

    cos, sin = _rope_tables(jnp.arange(t, dtype=jnp.int32))
    xp = x_prompt.reshape(b * t, d)
    q, qr, cmp_p, cmp_t, slc_t, win_t, ksb, kwb, vst, vwt, gates, hp = _inproj_seq(
        xp, row(norm_mix[0]), w_all, cos, sin, b, t, 512)
    kcvc = _compress_prompt(cmp_p.reshape(b, t, 256), cw)
    ncp = t // CMP_STRIDE
    attn_p = _nsa_prompt(q.reshape(b, t, -1), qr.reshape(b, t, -1), gates.reshape(b, t, -1), kcvc,
                         ksb.reshape(b, t, LANES), vst.reshape(b, t // KEY_TILE, LANES, KEY_TILE),
                         kwb.reshape(b, t, LANES), vwt.reshape(b, t // LANES, LANES, LANES),
                         _overlap_matrix(ncp, ncp - 1).T, _block_expand_matrix(t), 128)
    ohg_p, s_p = _hgrn(hp, jnp.zeros((b, HG_HEADS, HG_DK, HG_DV), F32), hg_lb, row(hg_norm[0]), b, t, 4, HG_ROWS)
    y_p = tail(xp, attn_p.reshape(b * t, -1), ohg_p, 512)

    pos_s = past + (jnp.arange(db * tq, dtype=jnp.int32) % tq)
    cos_s, sin_s = _rope_tables(pos_s)
    xs = x_sample.reshape(db * tq, d)
    q_s, qr_s, cmp_s, slc_s, win_s, _, _, gates_s, hs = _inproj(xs, row(norm_mix[0]), w_all, cos_s, sin_s, 256)
    kcvc_s = _compress_paged(_feature_major(cache_cmp, n_pool), page_table, cw)
    ncs = past // CMP_STRIDE
    attn_s, nwin_t = _nsa_sample(q_s, qr_s, gates_s, kcvc_s, _feature_major(cache_slc, n_pool), slc_s,
                                 _feature_major(cache_win, db), win_s, page_table,
                                 _overlap_matrix(ncs, ncs - 1), _block_expand_matrix(past + KEY_PAD), tq)
    ohg_s, s_s = _hgrn(hs, state_hgrn[0], hg_lb, row(hg_norm[0]), db, tq, 1, tq)
    y_s = tail(xs, attn_s, ohg_s, 512)

    kv5 = lambda a, nb, nt: a.reshape(1, nb, nt, 2, NSA_KV_HEADS, HEAD_DIM)
    kv5_t = lambda a_t, nb, nt: kv5(a_t.transpose(0, 2, 1), nb, nt)
    keep = min(WINDOW, t)
    return (y_p.reshape(b, t, d), y_s.reshape(db, tq, d),
            kv5_t(cmp_t, b, t), kv5_t(slc_t, b, t), kv5_t(win_t[:, :, t - keep:], b, keep), s_p[None],
            kv5(cmp_s, db, tq), kv5(slc_s, db, tq), kv5_t(nwin_t, db, wb), s_s[None])
```

```python
import functools

import numpy as np
import jax
import jax.numpy as jnp
from jax import lax
from jax.experimental import pallas as pl
from jax.experimental.pallas import tpu as pltpu

F32 = jnp.float32
BF16 = jnp.bfloat16

D_MODEL = 1024
HEAD_DIM = 64
NSA_HEADS = 8
NSA_KV_HEADS = 2
NSA_GROUP = NSA_HEADS // NSA_KV_HEADS
CMP_BLOCK = 32
CMP_STRIDE = 16
SLC_BLOCK = 64
SLC_TOPN = 16
WINDOW = 512
ROPE_THETA = 10000.0
HG_HEADS = 4
HG_DK = 128
HG_DV = 128
NSA_WIDTH = NSA_HEADS * HEAD_DIM
HG_WIDTH = HG_HEADS * HG_DV
KV_WIDTH = NSA_KV_HEADS * HEAD_DIM
PEER_HEADS = 8
PEER_KEYS = 128
PEER_QDIM = 256
PEER_TOPK = 16
RMS_EPS = 1e-6
NEG_INF = -1e30
FORCED_SCORE = 1e6
LOWEST = -3e38

LANES = 128
HALF = 64
VMEM_LIMIT = 56 * 1024 * 1024

COL_Q = 0
COL_KV = 512
COL_H = 1280
COL_G = 3328
PROJ_PAD = 3456


def _dot(a, b):
    return jnp.dot(a, b, preferred_element_type=F32)


def _dot_nt(a, b):
    return lax.dot_general(a, b, (((1,), (1,)), ((), ())), preferred_element_type=F32)


def _iota(shape, dim):
    return lax.broadcasted_iota(jnp.int32, shape, dim)


def _params(*sem):
    return pltpu.CompilerParams(dimension_semantics=sem, vmem_limit_bytes=VMEM_LIMIT)


def _rope_tile(x, cos, sin_signed, first_half):
    partner = jnp.where(first_half, pltpu.roll(x, LANES - 32, 1), pltpu.roll(x, 32, 1))
    return x * cos + partner * sin_signed


def _inproj_seq_kernel(x_ref, g_ref, w_ref, cos_ref, sin_ref,
                       q_ref, qr_ref, cmp_ref, cmpt_ref, slct_ref, wint_ref, ksb_ref, kwb_ref, vst_ref, vwt_ref,
                       gates_ref, h_ref):
    x = x_ref[...]
    ms = jnp.mean(x * x, axis=-1, keepdims=True)
    xn = (x * lax.rsqrt(ms + RMS_EPS)) * g_ref[...]
    proj = _dot(xn.astype(BF16), w_ref[...])
    cos = cos_ref[...]
    sin = sin_ref[...]
    first_half = (_iota(cos.shape, 1) & (HALF - 1)) < 32
    rope = lambda t: _rope_tile(t, cos, sin, first_half)
    q_ref[...] = proj[:, COL_Q:COL_Q + NSA_WIDTH]
    for c in range(NSA_WIDTH // LANES):
        qr_ref[:, c * LANES:(c + 1) * LANES] = rope(proj[:, COL_Q + c * LANES:COL_Q + (c + 1) * LANES])
    cmp = proj[:, COL_KV:COL_KV + 256]
    cmp_ref[...] = cmp
    cmpt_ref[...] = cmp.T
    ks = rope(proj[:, COL_KV + 256:COL_KV + 384])
    kw = rope(proj[:, COL_KV + 512:COL_KV + 640])
    vs_t = proj[:, COL_KV + 384:COL_KV + 512].T
    vw_t = proj[:, COL_KV + 640:COL_KV + 768].T
    slct_ref[0:LANES, :] = ks.T
    slct_ref[LANES:2 * LANES, :] = vs_t
    wint_ref[0:LANES, :] = kw.T
    wint_ref[LANES:2 * LANES, :] = vw_t
    ksb_ref[...] = ks.astype(BF16)
    kwb_ref[...] = kw.astype(BF16)
    for c in range(vst_ref.shape[0]):
        vst_ref[c] = vs_t[:, c * KEY_TILE:(c + 1) * KEY_TILE].astype(BF16)
    for c in range(vwt_ref.shape[0]):
        vwt_ref[c] = vw_t[:, c * LANES:(c + 1) * LANES].astype(BF16)
    gates_ref[...] = jax.nn.sigmoid(proj[:, COL_G:COL_G + LANES])
    h_ref[...] = proj[:, COL_H:COL_H + 4 * HG_WIDTH]


def _inproj_seq(x, norm_g, w_all, cos_t, sin_t, nb, t, tm):
    n = nb * t
    steps = t // tm
    row = lambda w: pl.BlockSpec((tm, w), lambda i: (i, 0))
    tab = pl.BlockSpec((tm, LANES), lambda i: (i % steps, 0))
    feat = pl.BlockSpec((None, 256, tm), lambda i: (i // steps, 0, i % steps))
    tiles = lambda w: pl.BlockSpec((tm // w, LANES, w), lambda i: (i, 0, 0))
    f32 = lambda *s: jax.ShapeDtypeStruct(s, F32)
    bf = lambda *s: jax.ShapeDtypeStruct(s, BF16)
    return pl.pallas_call(
        _inproj_seq_kernel,
        grid=(n // tm,),
        in_specs=[row(D_MODEL), pl.BlockSpec((1, D_MODEL), lambda i: (0, 0)),
                  pl.BlockSpec((D_MODEL, PROJ_PAD), lambda i: (0, 0)), tab, tab],
        out_specs=[row(NSA_WIDTH), row(NSA_WIDTH), row(256), feat, feat, feat, row(LANES), row(LANES),
                   tiles(KEY_TILE), tiles(LANES), row(LANES), row(4 * HG_WIDTH)],
        out_shape=[f32(n, NSA_WIDTH), f32(n, NSA_WIDTH), f32(n, 256), f32(nb, 256, t), f32(nb, 256, t),
                   f32(nb, 256, t), bf(n, LANES), bf(n, LANES), bf(n // KEY_TILE, LANES, KEY_TILE),
                   bf(n // LANES, LANES, LANES), f32(n, LANES), f32(n, 4 * HG_WIDTH)],
        compiler_params=_params("parallel"),
        name="inproj_seq",
    )(x, norm_g, w_all, cos_t, sin_t)


def _inproj_kernel(x_ref, g_ref, w_ref, cos_ref, sin_ref,
                   q_ref, qr_ref, cmp_ref, slc_ref, win_ref, slcb_ref, winb_ref, gates_ref, h_ref):
    x = x_ref[...]
    ms = jnp.mean(x * x, axis=-1, keepdims=True)
    xn = (x * lax.rsqrt(ms + RMS_EPS)) * g_ref[...]
    proj = _dot(xn.astype(BF16), w_ref[...])
    cos = cos_ref[...]
    sin = sin_ref[...]
    first_half = (_iota(cos.shape, 1) & (HALF - 1)) < 32
    rope = lambda t: _rope_tile(t, cos, sin, first_half)
    q_ref[...] = proj[:, COL_Q:COL_Q + NSA_WIDTH]
    for c in range(NSA_WIDTH // LANES):
        qr_ref[:, c * LANES:(c + 1) * LANES] = rope(proj[:, COL_Q + c * LANES:COL_Q + (c + 1) * LANES])
    cmp_ref[...] = proj[:, COL_KV:COL_KV + 256]
    ks = rope(proj[:, COL_KV + 256:COL_KV + 384])
    vs = proj[:, COL_KV + 384:COL_KV + 512]
    kw = rope(proj[:, COL_KV + 512:COL_KV + 640])
    vw = proj[:, COL_KV + 640:COL_KV + 768]
    slc_ref[:, 0:LANES] = ks
    slc_ref[:, LANES:2 * LANES] = vs
    win_ref[:, 0:LANES] = kw
    win_ref[:, LANES:2 * LANES] = vw
    slcb_ref[:, 0:LANES] = ks.astype(BF16)
    slcb_ref[:, LANES:2 * LANES] = vs.astype(BF16)
    winb_ref[:, 0:LANES] = kw.astype(BF16)
    winb_ref[:, LANES:2 * LANES] = vw.astype(BF16)
    gates_ref[...] = jax.nn.sigmoid(proj[:, COL_G:COL_G + LANES])
    h_ref[...] = proj[:, COL_H:COL_H + 4 * HG_WIDTH]


def _inproj(x, norm_g, w_all, cos_t, sin_t, tm):
    n = x.shape[0]
    ntab = cos_t.shape[0] // tm
    row = lambda w: pl.BlockSpec((tm, w), lambda i: (i, 0))
    tab = pl.BlockSpec((tm, LANES), lambda i: (i % ntab, 0))
    out_shapes = [
        jax.ShapeDtypeStruct((n, NSA_WIDTH), F32), jax.ShapeDtypeStruct((n, NSA_WIDTH), F32),
        jax.ShapeDtypeStruct((n, 256), F32), jax.ShapeDtypeStruct((n, 256), F32),
        jax.ShapeDtypeStruct((n, 256), F32), jax.ShapeDtypeStruct((n, 256), BF16),
        jax.ShapeDtypeStruct((n, 256), BF16), jax.ShapeDtypeStruct((n, LANES), F32),
        jax.ShapeDtypeStruct((n, 4 * HG_WIDTH), F32),
    ]
    return pl.pallas_call(
        _inproj_kernel,
        grid=(n // tm,),
        in_specs=[row(D_MODEL), pl.BlockSpec((1, D_MODEL), lambda i: (0, 0)),
                  pl.BlockSpec((D_MODEL, PROJ_PAD), lambda i: (0, 0)), tab, tab],
        out_specs=[row(NSA_WIDTH), row(NSA_WIDTH), row(256), row(256), row(256), row(256), row(256),
                   row(LANES), row(4 * HG_WIDTH)],
        out_shape=out_shapes,
        compiler_params=_params("parallel"),
        name="inproj",
    )(x, norm_g, w_all, cos_t, sin_t)


def _rope_tables(pos):
    half = HEAD_DIM // 2
    inv = ROPE_THETA ** (-jnp.arange(half, dtype=F32) / half)
    ang = pos.astype(F32)[:, None] * inv[None, :]
    cos = jnp.tile(jnp.cos(ang), (1, 4))
    sin = jnp.sin(ang)
    return cos, jnp.tile(jnp.concatenate([-sin, sin], axis=1), (1, 2))


def _compress_rows(rows_ref, n_out, w_ref, pe_ref):
    a = jnp.zeros((n_out, LANES), F32)
    b = jnp.zeros((n_out, LANES), F32)
    for j in range(CMP_STRIDE):
        xj = rows_ref[pl.ds(j, n_out, stride=CMP_STRIDE), :]
        a = a + _dot((xj + pe_ref[j:j + 1, :]).astype(BF16), w_ref[j])
        b = b + _dot((xj + pe_ref[CMP_STRIDE + j:CMP_STRIDE + j + 1, :]).astype(BF16), w_ref[CMP_STRIDE + j])
    return a + pltpu.roll(b, n_out - 1, 0)


def _compress_prompt_kernel(rk_ref, rv_ref, wk_ref, wv_ref, pek_ref, pev_ref, out_ref):
    n_out = out_ref.shape[0]
    out_ref[:, 0:LANES] = _compress_rows(rk_ref, n_out, wk_ref, pek_ref).astype(BF16)
    out_ref[:, LANES:2 * LANES] = _compress_rows(rv_ref, n_out, wv_ref, pev_ref).astype(BF16)


def _compress_prompt(rows, cw):
    b, t, _ = rows.shape
    n_out = t // CMP_STRIDE
    wspec = pl.BlockSpec((CMP_BLOCK, LANES, LANES), lambda i: (0, 0, 0))
    pspec = pl.BlockSpec((CMP_BLOCK, LANES), lambda i: (0, 0))
    return pl.pallas_call(
        _compress_prompt_kernel,
        grid=(b,),
        in_specs=[pl.BlockSpec((None, t, LANES), lambda i: (i, 0, 0)),
                  pl.BlockSpec((None, t, LANES), lambda i: (i, 0, 1)), wspec, wspec, pspec, pspec],
        out_specs=pl.BlockSpec((None, n_out, 256), lambda i: (i, 0, 0)),
        out_shape=jax.ShapeDtypeStruct((b, n_out, 256), BF16),
        compiler_params=_params("parallel"),
        name="compress_prompt",
    )(rows, rows, *cw)


def _stack_heads(ref, lo_half, nq):
    hi_half = jnp.logical_not(lo_half)
    parts = []
    for h in range(NSA_HEADS):
        g = h // NSA_GROUP
        tile = ref[:, LANES * (h // 2):LANES * (h // 2 + 1)]
        if h % 2 != g:
            tile = pltpu.roll(tile, HALF, 1)
        parts.append(jnp.where(lo_half if g == 0 else hi_half, tile, 0.0))
    return (jnp.concatenate(parts, axis=0) * (HEAD_DIM ** -0.5)).astype(BF16)


def _per_head(x, nq):
    return jnp.concatenate([x[0:nq]] * NSA_GROUP + [x[nq:2 * nq]] * NSA_GROUP, axis=0)


def _masked_softmax_rows(s, valid):
    sm = jnp.where(valid, s, NEG_INF)
    m = jnp.max(sm, axis=-1, keepdims=True)
    e = jnp.where(valid, jnp.exp(sm - m), 0.0)
    l = jnp.sum(e, axis=-1, keepdims=True)
    return e / jnp.where(l > 0.0, l, 1.0)


def _select_blocks(imp, pos_col):
    blk = _iota(imp.shape, 1)
    cur = pos_col >> 6
    forced = (blk == 0) | (blk == cur) | (blk == cur - 1)
    causal = (blk << 6) <= pos_col
    score = jnp.where(causal, jnp.where(forced, FORCED_SCORE, imp), -FORCED_SCORE)
    return score, causal


def _topk_mask_t(score_t, n_sel):
    bi = _iota(score_t.shape, 0)
    sel = jnp.zeros(score_t.shape, F32)
    sc = score_t
    for _ in range(n_sel):
        m = jnp.max(sc, axis=0, keepdims=True)
        idx = jnp.min(jnp.where(sc == m, bi, LANES), axis=0, keepdims=True)
        hit = bi == idx
        sel = jnp.where(hit, 1.0, sel)
        sc = jnp.where(hit, LOWEST, sc)
    return sel


def _cmp_branch(qn, kc, vc, ov, pos_rows, pos_grp, nq):
    s = _dot_nt(qn, kc)
    valid = (_iota(s.shape, 1) * CMP_STRIDE + (CMP_BLOCK - 1)) <= pos_rows
    p = _masked_softmax_rows(s, valid)
    o_c = _dot(p.astype(BF16), vc)
    slab = lambda h: p[h * nq:(h + 1) * nq]
    psum = jnp.concatenate([slab(0) + slab(1) + slab(2) + slab(3), slab(4) + slab(5) + slab(6) + slab(7)], axis=0)
    hi = psum.astype(BF16)
    lo = (psum - hi.astype(F32)).astype(BF16)
    imp = _dot(hi, ov) + _dot(lo, ov)
    score, causal = _select_blocks(imp, pos_grp)
    return o_c, score, causal


def _combine_heads(out_ref, gates, lo_half, o_c, o_s, o_w, nq):
    combs = []
    for h in range(NSA_HEADS):
        rows = slice(h * nq, (h + 1) * nq)
        comb = (gates[:, h:h + 1] * o_c[rows] + gates[:, NSA_HEADS + h:NSA_HEADS + h + 1] * o_s[rows]
                + gates[:, 2 * NSA_HEADS + h:2 * NSA_HEADS + h + 1] * o_w[rows])
        if h % 2 != h // NSA_GROUP:
            comb = pltpu.roll(comb, HALF, 1)
        combs.append(comb)
    for r in range(NSA_HEADS // 2):
        out_ref[:, LANES * r:LANES * (r + 1)] = jnp.where(lo_half, combs[2 * r], combs[2 * r + 1])


KEY_TILE = 512


def _masked_softmax_cols(s, valid):
    sm = jnp.where(valid, s, NEG_INF)
    m = jnp.max(sm, axis=0, keepdims=True)
    e = jnp.where(valid, jnp.exp(sm - m), 0.0)
    l = jnp.sum(e, axis=0, keepdims=True)
    return e / jnp.where(l > 0.0, l, 1.0)


def _nsa_prompt_kernel(q_ref, qr_ref, gates_ref, cmp_ref, vct_ref, ks_ref, vst_ref, kw_ref, vwt_ref, ovt_ref, et_ref,
                       out_ref, *, tq):
    t0 = pl.program_id(1) * tq
    cols = NSA_HEADS * tq
    grp = NSA_KV_HEADS * tq
    pos_cols = t0 + (_iota((1, cols), 1) & (tq - 1))
    pos_grp = t0 + (_iota((1, grp), 1) & (tq - 1))
    n_tiles = (t0 + tq + KEY_TILE - 1) // KEY_TILE
    w_chunk = jnp.maximum(t0 - WINDOW, 0) // LANES
    w_start = pl.multiple_of(w_chunk * LANES, LANES)
    w_len = WINDOW + tq
    per_head = lambda x: jnp.concatenate([x[:, 0:tq]] * NSA_GROUP + [x[:, tq:grp]] * NSA_GROUP, axis=1)

    lo_half = _iota((tq, LANES), 1) < HALF
    qn = _stack_heads(q_ref, lo_half, tq)
    qr = _stack_heads(qr_ref, lo_half, tq)

    all_heads = lambda x: jnp.concatenate([x] * NSA_HEADS, axis=1)
    pos_tok = t0 + _iota((1, tq), 1)

    sc = _dot_nt(cmp_ref[:, 0:LANES], qn)
    n_i = _iota((sc.shape[0], tq), 0)
    sc = sc + all_heads(jnp.where(n_i * CMP_STRIDE + (CMP_BLOCK - 1) <= pos_tok, 0.0, NEG_INF))
    ec = jnp.exp(sc - jnp.max(sc, axis=0, keepdims=True))
    norm_c = jnp.where(pos_cols >= CMP_BLOCK - 1, 1.0 / jnp.sum(ec, axis=0, keepdims=True), 0.0)
    o_c = _dot(vct_ref[...], ec.astype(BF16)) * norm_c
    pc = ec * norm_c
    slab = lambda h: pc[:, h * tq:(h + 1) * tq]
    psum = jnp.concatenate([slab(0) + slab(1) + slab(2) + slab(3), slab(4) + slab(5) + slab(6) + slab(7)], axis=1)
    hi = psum.astype(BF16)
    lo = (psum - hi.astype(F32)).astype(BF16)
    ovt = ovt_ref[...]
    imp = _dot(ovt, hi) + _dot(ovt, lo)
    blk = _iota(imp.shape, 0)
    cur = pos_grp >> 6
    forced = (blk == 0) | (blk == cur) | (blk == cur - 1)
    causal = (blk << 6) <= pos_grp
    score = jnp.where(causal, jnp.where(forced, FORCED_SCORE, imp), -FORCED_SCORE)
    sel = jnp.where(causal, _topk_mask_t(score, SLC_TOPN), 0.0)
    notsel = (1.0 - sel).astype(BF16)

    sw = _dot_nt(kw_ref[pl.ds(w_start, w_len), :], qr)
    kp = w_start + _iota((w_len, tq), 0)
    sw = sw + all_heads(jnp.where((kp <= pos_tok) & (kp >= pos_tok - WINDOW), 0.0, NEG_INF))
    ew = jnp.exp(sw - jnp.max(sw, axis=0, keepdims=True))
    vwt = jnp.concatenate([vwt_ref[w_chunk + c] for c in range(w_len // LANES)], axis=1)
    o_w = _dot(vwt, ew.astype(BF16)) / jnp.sum(ew, axis=0, keepdims=True)

    def tile_step(kt, carry, diag):
        m, l, acc = carry
        k0 = pl.multiple_of(kt * KEY_TILE, KEY_TILE)
        s = _dot_nt(ks_ref[pl.ds(k0, KEY_TILE), :], qr)
        s = s + per_head(_dot(et_ref[pl.ds(k0, KEY_TILE), :], notsel) * NEG_INF)
        if diag:
            s = s + all_heads(jnp.where(k0 + _iota((KEY_TILE, tq), 0) <= pos_tok, 0.0, NEG_INF))
        m_new = jnp.maximum(m, jnp.max(s, axis=0, keepdims=True))
        alpha = jnp.exp(m - m_new)
        p = jnp.exp(s - m_new)
        l = alpha * l + jnp.sum(p, axis=0, keepdims=True)
        acc = alpha * acc + _dot(vst_ref[kt], p.astype(BF16))
        return m_new, l, acc

    init = (jnp.full((1, cols), NEG_INF, F32), jnp.zeros((1, cols), F32), jnp.zeros((LANES, cols), F32))
    carry = lax.fori_loop(0, n_tiles - 1, lambda kt, c: tile_step(kt, c, False), init)
    _, l, acc = tile_step(n_tiles - 1, carry, True)
    o_s = acc / l

    gt = gates_ref[...].T
    for r in range(NSA_HEADS // 2):
        halves = []
        for h in (2 * r, 2 * r + 1):
            c = slice(h * tq, (h + 1) * tq)
            comb = (gt[h:h + 1] * o_c[:, c] + gt[NSA_HEADS + h:NSA_HEADS + h + 1] * o_s[:, c]
                    + gt[2 * NSA_HEADS + h:2 * NSA_HEADS + h + 1] * o_w[:, c])
            g = h // NSA_GROUP
            halves.append(comb[HALF * g:HALF * (g + 1)])
        out_ref[:, LANES * r:LANES * (r + 1)] = jnp.concatenate(halves, axis=0).T


def _nsa_prompt(q, qr, gates, kcvc, ksb, vst, kwb, vwt, ovt, et, tq):
    b, t, _ = q.shape
    ncp = kcvc.shape[1]
    vct = kcvc[:, :, LANES:].transpose(0, 2, 1)
    blk = lambda w: pl.BlockSpec((None, tq, w), lambda bi, i: (bi, i, 0))
    full = lambda r, w: pl.BlockSpec((None, r, w), lambda bi, i: (bi, 0, 0))
    full4 = lambda a: pl.BlockSpec((None,) + a.shape[1:], lambda bi, i: (bi, 0, 0, 0))
    return pl.pallas_call(
        functools.partial(_nsa_prompt_kernel, tq=tq),
        grid=(b, t // tq),
        in_specs=[blk(NSA_WIDTH), blk(NSA_WIDTH), blk(LANES), full(ncp, 256), full(LANES, ncp), full(t, LANES),
                  full4(vst), full(t, LANES), full4(vwt),
                  pl.BlockSpec((LANES, ncp), lambda bi, i: (0, 0)),
                  pl.BlockSpec((t, LANES), lambda bi, i: (0, 0))],
        out_specs=blk(NSA_WIDTH),
        out_shape=jax.ShapeDtypeStruct((b, t, NSA_WIDTH), F32),
        compiler_params=_params("parallel", "arbitrary"),
        name="nsa_prompt",
    )(q, qr, gates, kcvc, vct, ksb, vst, kwb, vwt, ovt, et)


def _overlap_matrix(n_cmp_pad, n_cmp):
    c0 = jnp.arange(n_cmp_pad, dtype=jnp.int32)[:, None] * CMP_STRIDE
    s0 = jnp.arange(LANES, dtype=jnp.int32)[None, :] * SLC_BLOCK
    real = jnp.arange(n_cmp_pad, dtype=jnp.int32)[:, None] < n_cmp
    return ((c0 < s0 + SLC_BLOCK) & (c0 + CMP_BLOCK > s0) & real).astype(BF16)


def _block_expand_matrix(n_keys):
    r = jnp.arange(n_keys, dtype=jnp.int32)[:, None] // SLC_BLOCK
    return (r == jnp.arange(LANES, dtype=jnp.int32)[None, :]).astype(BF16)


PAGE_ROWS = 128


def _compress_paged_kernel(pt_ref, *refs, n_pages):
    pages = refs[:n_pages]
    wk_ref, wv_ref, pek_ref, pev_ref, out_ref, rk_scr, rv_scr = refs[n_pages:]
    for p in range(n_pages):
        rk_scr[p * PAGE_ROWS:(p + 1) * PAGE_ROWS, :] = pages[p][0:LANES, :].T
        rv_scr[p * PAGE_ROWS:(p + 1) * PAGE_ROWS, :] = pages[p][LANES:2 * LANES, :].T
    n_out = out_ref.shape[0]
    out_ref[:, 0:LANES] = _compress_rows(rk_scr, n_out, wk_ref, pek_ref).astype(BF16)
    out_ref[:, LANES:2 * LANES] = _compress_rows(rv_scr, n_out, wv_ref, pev_ref).astype(BF16)


def _page_specs(n_pages):
    return [pl.BlockSpec((None, 256, PAGE_ROWS), functools.partial(lambda i, pt, p: (pt[i * n_pages + p], 0, 0), p=p))
            for p in range(n_pages)]


def _feature_major(cache, lead):
    return cache[0].reshape(lead, cache.shape[2], 256).transpose(0, 2, 1)


def _compress_paged(cache, page_table, cw):
    db, n_pages = page_table.shape
    past = n_pages * PAGE_ROWS
    n_out = past // CMP_STRIDE
    wspec = pl.BlockSpec((CMP_BLOCK, LANES, LANES), lambda i, pt: (0, 0, 0))
    pspec = pl.BlockSpec((CMP_BLOCK, LANES), lambda i, pt: (0, 0))
    grid_spec = pltpu.PrefetchScalarGridSpec(
        num_scalar_prefetch=1, grid=(db,),
        in_specs=_page_specs(n_pages) + [wspec, wspec, pspec, pspec],
        out_specs=pl.BlockSpec((None, n_out, 256), lambda i, pt: (i, 0, 0)),
        scratch_shapes=[pltpu.VMEM((past, LANES), F32), pltpu.VMEM((past, LANES), F32)])
    return pl.pallas_call(
        functools.partial(_compress_paged_kernel, n_pages=n_pages),
        grid_spec=grid_spec,
        out_shape=jax.ShapeDtypeStruct((db, n_out, 256), BF16),
        compiler_params=_params("arbitrary"),
        name="compress_paged",
    )(page_table.reshape(-1), *([cache] * n_pages), *cw)


KEY_PAD = LANES


def _nsa_sample_kernel(pt_ref, q_ref, qr_ref, gates_ref, cmp_ref, *refs, n_pages):
    pages = refs[:n_pages]
    snew_ref, cwin_ref, wnew_ref, ov_ref, et_ref, out_ref, nwin_ref = refs[n_pages:]
    nq = q_ref.shape[0]
    past = n_pages * PAGE_ROWS
    wb = cwin_ref.shape[1]
    rows = NSA_HEADS * nq
    pad_rows = lambda x: jnp.concatenate([x, jnp.zeros((KEY_PAD - nq, x.shape[1]), F32)], axis=0)
    snew = pad_rows(snew_ref[...])
    wnew = pad_rows(wnew_ref[...])

    shifted = pltpu.roll(cwin_ref[...], wb - nq, 1)
    new_t = jnp.concatenate([wnew[:, 0:LANES].T, wnew[:, LANES:2 * LANES].T], axis=0)
    tail = jnp.where(_iota((256, LANES), 1) >= LANES - nq, pltpu.roll(new_t, LANES - nq, 1), shifted[:, wb - LANES:wb])
    nwin_ref[:, 0:wb - LANES] = shifted[:, 0:wb - LANES]
    nwin_ref[:, wb - LANES:wb] = tail

    lo_half = _iota((nq, LANES), 1) < HALF
    pos_rows = past + (_iota((rows, 1), 0) & (nq - 1))
    n_grp = NSA_KV_HEADS * nq
    pos_grp = past + (_iota((n_grp, 1), 0) & (nq - 1))
    qn = _stack_heads(q_ref, lo_half, nq)
    qr = _stack_heads(qr_ref, lo_half, nq)
    o_c, score, causal = _cmp_branch(qn, cmp_ref[:, 0:LANES], cmp_ref[:, LANES:2 * LANES], ov_ref[...],
                                     pos_rows, pos_grp, nq)
    score_sq = jnp.concatenate([score, jnp.full((LANES - n_grp, LANES), LOWEST, F32)], axis=0)
    sel = jnp.where(causal, _topk_mask_t(score_sq.T, SLC_TOPN).T[0:n_grp], 0.0)
    notsel = (1.0 - sel).astype(BF16)

    s = jnp.concatenate([_dot(qr, pages[p][0:LANES, :].astype(BF16)) for p in range(n_pages)]
                        + [_dot_nt(qr, snew[:, 0:LANES].astype(BF16))], axis=1)
    s = s + _per_head(_dot_nt(notsel, et_ref[...]) * NEG_INF, nq)
    s = jnp.where(_iota(s.shape, 1) <= pos_rows, s, NEG_INF)
    e = jnp.exp(s - jnp.max(s, axis=-1, keepdims=True))
    eb = e.astype(BF16)
    o_s = _dot(eb[:, past:past + KEY_PAD], snew[:, LANES:2 * LANES].astype(BF16))
    for p in range(n_pages):
        o_s = o_s + _dot_nt(eb[:, p * PAGE_ROWS:(p + 1) * PAGE_ROWS], pages[p][LANES:2 * LANES, :].astype(BF16))
    o_s = o_s / jnp.sum(e, axis=-1, keepdims=True)

    sw = jnp.concatenate([_dot(qr, cwin_ref[0:LANES, :].astype(BF16)), _dot_nt(qr, wnew[:, 0:LANES].astype(BF16))], axis=1)
    kp = (past - wb) + _iota(sw.shape, 1)
    pw = _masked_softmax_rows(sw, (kp <= pos_rows) & (kp >= pos_rows - WINDOW) & (kp >= 0)).astype(BF16)
    o_w = (_dot_nt(pw[:, 0:wb], cwin_ref[LANES:2 * LANES, :].astype(BF16))
           + _dot(pw[:, wb:wb + KEY_PAD], wnew[:, LANES:2 * LANES].astype(BF16)))
    _combine_heads(out_ref, gates_ref[...], lo_half, o_c, o_s, o_w, nq)


def _nsa_sample(q, qr, gates, kcvc, cache_slc, slc_new, cache_win, win_new, page_table, ov, et, nq):
    db, n_pages = page_table.shape
    past = n_pages * PAGE_ROWS
    wb = cache_win.shape[2]
    ncp = kcvc.shape[1]
    blk = lambda w: pl.BlockSpec((nq, w), lambda i, pt: (i, 0))
    win_spec = pl.BlockSpec((None, 256, wb), lambda i, pt: (i, 0, 0))
    grid_spec = pltpu.PrefetchScalarGridSpec(
        num_scalar_prefetch=1, grid=(db,),
        in_specs=[blk(NSA_WIDTH), blk(NSA_WIDTH), blk(LANES),
                  pl.BlockSpec((None, ncp, 256), lambda i, pt: (i, 0, 0))] + _page_specs(n_pages) + [
                  blk(256), win_spec, blk(256),
                  pl.BlockSpec((ncp, LANES), lambda i, pt: (0, 0)),
                  pl.BlockSpec((past + KEY_PAD, LANES), lambda i, pt: (0, 0))],
        out_specs=[blk(NSA_WIDTH), win_spec])
    return pl.pallas_call(
        functools.partial(_nsa_sample_kernel, n_pages=n_pages),
        grid_spec=grid_spec,
        out_shape=[jax.ShapeDtypeStruct((db * nq, NSA_WIDTH), F32), jax.ShapeDtypeStruct((db, 256, wb), F32)],
        compiler_params=_params("arbitrary"),
        name="nsa_sample",
    )(page_table.reshape(-1), q, qr, gates, kcvc, *([cache_slc] * n_pages), slc_new, cache_win, win_new, ov, et)


HG_ROWS = 128
HG_SUB = 16


def _dot_split3(m_bf16, x):
    a = x.astype(BF16)
    r = x - a.astype(F32)
    b = r.astype(BF16)
    c = (r - b.astype(F32)).astype(BF16)
    return _dot(m_bf16, a) + _dot(m_bf16, b) + _dot(m_bf16, c)


def _hgrn_chunk(q, fpre, v, gpre, lb, ng, s0, n_real):
    f = lb + (1.0 - lb) * jax.nn.sigmoid(fpre)
    logf = jnp.log(f)
    kk = 1.0 - f
    if n_real < HG_ROWS:
        pad = lambda a: jnp.concatenate([a, jnp.zeros((HG_ROWS - n_real, LANES), F32)], axis=0)
        q, logf, kk, v = pad(q), pad(logf), pad(kk), pad(v)
    r_i = _iota((HG_ROWS, HG_ROWS), 0)
    c_i = _iota((HG_ROWS, HG_ROWS), 1)
    tri_b = r_i >= c_i
    tri = jnp.where(tri_b, 1.0, 0.0).astype(BF16)
    cum = _dot_split3(tri, logf)
    cprev = cum - logf
    cum_last = cum[HG_ROWS - 1:HG_ROWS, :]
    s0b = s0.astype(BF16)
    o = _dot((q * jnp.exp(cum)).astype(BF16), s0b)
    row = _iota((HG_ROWS, LANES), 0)
    a_rows = []
    n_sub = -(-n_real // HG_SUB)
    for i in range(n_sub):
        r0 = i * HG_SUB
        c_ref = cprev[r0:r0 + 1, :]
        qt = q[r0:r0 + HG_SUB] * jnp.exp(cum[r0:r0 + HG_SUB] - c_ref)
        kt = kk * jnp.exp(jnp.where(row < r0 + HG_SUB, c_ref - cum, NEG_INF))
        a_rows.append(_dot_nt(qt.astype(BF16), kt.astype(BF16)))
    if n_sub * HG_SUB < HG_ROWS:
        a_rows.append(jnp.zeros((HG_ROWS - n_sub * HG_SUB, HG_ROWS), F32))
    a = jnp.where(tri_b, jnp.concatenate(a_rows, axis=0), 0.0)
    vb = v.astype(BF16)
    o = o + _dot(a.astype(BF16), vb)
    kdec = kk * jnp.exp(cum_last - cum)
    scale = jnp.broadcast_to(jnp.exp(cum_last), (HG_ROWS, LANES)).T
    s_new = scale * s0 + _dot(kdec.T.astype(BF16), vb)
    o = o[0:n_real]
    o = o * lax.rsqrt(jnp.mean(o * o, axis=-1, keepdims=True) + RMS_EPS) * ng
    return o * jax.nn.sigmoid(gpre), s_new


def _hgrn_kernel(q_ref, f_ref, v_ref, g_ref, s0_ref, lb_ref, ng_ref, o_ref, sfin_ref, s_scr, *, n_chunks, n_real):
    @pl.when(pl.program_id(1) == 0)
    def _init():
        s_scr[...] = s0_ref[...]

    lbp = lb_ref[...]
    e = jnp.exp(lbp - jnp.max(lbp, axis=0, keepdims=True))
    lb = e[0:1, :] / jnp.sum(e, axis=0, keepdims=True)
    ng = ng_ref[...]

    def chunk(c, carry):
        rows = slice(0, n_real) if n_chunks == 1 else pl.ds(pl.multiple_of(c * n_real, n_real), n_real)
        for hd in range(HG_HEADS):
            cols = slice(hd * LANES, (hd + 1) * LANES)
            o, s_new = _hgrn_chunk(q_ref[rows, cols], f_ref[rows, cols], v_ref[rows, cols], g_ref[rows, cols],
                                   lb[:, cols], ng[:, cols], s_scr[hd], n_real)
            o_ref[rows, cols] = o
            s_scr[hd] = s_new
        return carry

    if n_chunks == 1:
        chunk(0, 0)
    else:
        lax.fori_loop(0, n_chunks, chunk, 0)
    sfin_ref[...] = s_scr[...]


def _hgrn(h, s0, hg_lb, hg_norm, nb, t, n_chunks, n_real):
    rows = n_chunks * n_real
    steps = t // rows
    col = lambda sec: pl.BlockSpec((rows, HG_WIDTH), lambda b, j: (b * steps + j, sec))
    st = pl.BlockSpec((None, HG_HEADS, HG_DK, HG_DV), lambda b, j: (b, 0, 0, 0))
    return pl.pallas_call(
        functools.partial(_hgrn_kernel, n_chunks=n_chunks, n_real=n_real),
        grid=(nb, steps),
        in_specs=[col(0), col(1), col(2), col(3), st,
                  pl.BlockSpec(hg_lb.shape, lambda b, j: (0, 0)),
                  pl.BlockSpec((1, HG_WIDTH), lambda b, j: (0, 0))],
        out_specs=[pl.BlockSpec((rows, HG_WIDTH), lambda b, j: (b * steps + j, 0)), st],
        out_shape=[jax.ShapeDtypeStruct((nb * t, HG_WIDTH), F32),
                   jax.ShapeDtypeStruct((nb, HG_HEADS, HG_DK, HG_DV), F32)],
        scratch_shapes=[pltpu.VMEM((HG_HEADS, HG_DK, HG_DV), F32)],
        compiler_params=_params("parallel", "arbitrary"),
        name="hgrn",
    )(h, h, h, h, s0, hg_lb, hg_norm)


def _outproj_kernel(x_ref, a_ref, hg_ref, wo_ref, nf_ref, wq_ref, x1_ref, xn_ref, pq_ref):
    mix = jnp.concatenate([a_ref[...], hg_ref[...]], axis=1).astype(BF16)
    x1 = x_ref[...] + _dot(mix, wo_ref[...])
    x1_ref[...] = x1
    ms = jnp.mean(x1 * x1, axis=-1, keepdims=True)
    xb = ((x1 * lax.rsqrt(ms + RMS_EPS)) * nf_ref[...]).astype(BF16)
    xn_ref[...] = xb
    pq_ref[...] = _dot(xb, wq_ref[...])


def _outproj(x, attn, ohg, w_out, norm_ffn, wq, tm):
    n = x.shape[0]
    row = lambda w: pl.BlockSpec((tm, w), lambda i: (i, 0))
    full = lambda a: pl.BlockSpec(a.shape, lambda i: (0, 0))
    nq = wq.shape[1]
    return pl.pallas_call(
        _outproj_kernel,
        grid=(n // tm,),
        in_specs=[row(D_MODEL), row(NSA_WIDTH), row(HG_WIDTH), full(w_out), full(norm_ffn), full(wq)],
        out_specs=[row(D_MODEL), row(D_MODEL), row(nq)],
        out_shape=[jax.ShapeDtypeStruct((n, D_MODEL), F32), jax.ShapeDtypeStruct((n, D_MODEL), BF16),
                   jax.ShapeDtypeStruct((n, nq), F32)],
        compiler_params=_params("parallel"),
        name="outproj",
    )(x, attn, ohg, w_out, norm_ffn, wq)


def _top_rows(s, k, val_scr, idx_scr):
    bi = _iota(s.shape, 0)
    big = s.shape[0]
    for a in range(k):
        m = jnp.max(s, axis=0, keepdims=True)
        idx = jnp.min(jnp.where(s == m, bi, big), axis=0, keepdims=True)
        val_scr[a:a + 1, :] = m
        idx_scr[a:a + 1, :] = idx
        s = jnp.where(bi == idx, LOWEST, s)


def _peer_select_kernel(pq_ref, keys_ref, i_ref, j_ref, g_ref, v12, i12, sc, cd):
    tms = pq_ref.shape[0]
    half = PEER_QDIM // 2
    s1 = _dot_nt(keys_ref[0].astype(BF16), pq_ref[:, 0:half].astype(BF16))
    s2 = _dot_nt(keys_ref[1].astype(BF16), pq_ref[:, half:2 * half].astype(BF16))
    _top_rows(jnp.concatenate([s1, s2], axis=1), PEER_TOPK, v12, i12)
    a1 = v12[:, 0:tms]
    a2 = v12[:, tms:2 * tms]
    r16 = _iota((PEER_TOPK, tms), 0)
    r8 = _iota((8, tms), 0)
    parts = [a1[0:1] + a2]
    codes = [r16]
    for a in range(1, 8):
        parts.append(a1[a:a + 1] + a2[0:8])
        codes.append(r8 + PEER_TOPK * a)
    parts.append(a1[8:16] + a2[0:1])
    codes.append((r8 + 8) * PEER_TOPK)
    cand = jnp.concatenate(parts, axis=0)
    code = jnp.concatenate(codes, axis=0)
    for k in range(PEER_TOPK):
        m = jnp.max(cand, axis=0, keepdims=True)
        cs = jnp.min(jnp.where(cand == m, code, PEER_TOPK * PEER_TOPK), axis=0, keepdims=True)
        sc[k:k + 1, :] = m
        cd[k:k + 1, :] = cs
        cand = jnp.where(code == cs, LOWEST, cand)
    scv = sc[...]
    cdv = cd[...]
    ak = cdv >> 4
    bk = cdv & (PEER_TOPK - 1)
    idx1 = i12[:, 0:tms]
    idx2 = i12[:, tms:2 * tms]
    ik = jnp.zeros((PEER_TOPK, tms), jnp.int32)
    jk = jnp.zeros((PEER_TOPK, tms), jnp.int32)
    for a in range(PEER_TOPK):
        ik = jnp.where(ak == a, idx1[a:a + 1], ik)
        jk = jnp.where(bk == a, idx2[a:a + 1], jk)
    e = jnp.exp(scv - scv[0:1])
    i_ref[...] = ik.astype(F32)
    j_ref[...] = jk.astype(F32)
    g_ref[...] = e / jnp.sum(e, axis=0, keepdims=True)


def _peer_select(pq, keys, tms):
    n = pq.shape[0]
    out = pl.BlockSpec((PEER_TOPK, tms), lambda i, h: (h, i))
    shp = jax.ShapeDtypeStruct((PEER_HEADS * PEER_TOPK, n), F32)
    return pl.pallas_call(
        _peer_select_kernel,
        grid=(n // tms, PEER_HEADS),
        in_specs=[pl.BlockSpec((tms, PEER_QDIM), lambda i, h: (i, h)),
                  pl.BlockSpec((None, 2, PEER_KEYS, PEER_QDIM // 2), lambda i, h: (h, 0, 0, 0))],
        out_specs=[out, out, out],
        out_shape=[shp, shp, shp],
        scratch_shapes=[pltpu.VMEM((PEER_TOPK, 2 * tms), F32), pltpu.VMEM((PEER_TOPK, 2 * tms), jnp.int32),
                        pltpu.VMEM((PEER_TOPK, tms), F32), pltpu.VMEM((PEER_TOPK, tms), jnp.int32)],
        compiler_params=_params("parallel", "arbitrary"),
        name="peer_select",
    )(pq, keys)


W_PITCH = PEER_KEYS + 8
EXPERT_CHUNK = 256


def _peer_dense_kernel(xn_ref, ik_ref, jk_ref, gk_ref, u_ref, v_ref, x1_ref, nf_ref, out_ref, w_scr, acc_scr, *, tm, te):
    e_idx = pl.program_id(1)

    @pl.when(e_idx == 0)
    def _build():
        sub = _iota((PEER_KEYS, LANES), 0).astype(F32)

        def body(n, carry):
            irow = ik_ref[pl.ds(n, 1), :]
            jrow = jk_ref[pl.ds(n, 1), :]
            grow = gk_ref[pl.ds(n, 1), :]
            a = jnp.where(irow == sub, grow, 0.0).astype(BF16)
            bt = jnp.where(jrow == sub, 1.0, 0.0).astype(BF16)
            w_scr[pl.ds(pl.multiple_of(n * W_PITCH, 8), PEER_KEYS), :] = _dot_nt(a, bt)
            return carry

        lax.fori_loop(0, tm, body, 0, unroll=32)
        acc_scr[...] = jnp.zeros(acc_scr.shape, F32)

    xn = xn_ref[...]
    acc = None
    for c in range(te // EXPERT_CHUNK):
        rows = slice(c * EXPERT_CHUNK, (c + 1) * EXPERT_CHUNK)
        h = _dot_nt(xn, u_ref[rows, :])
        i0 = (e_idx * te + c * EXPERT_CHUNK) // PEER_KEYS
        wt = jnp.concatenate([w_scr[pl.ds(i0 + ii, tm, stride=W_PITCH), :] for ii in range(EXPERT_CHUNK // PEER_KEYS)],
                             axis=1)
        d = _dot((jax.nn.gelu(h) * wt).astype(BF16), v_ref[rows, :])
        acc = d if acc is None else acc + d
    acc_scr[...] += acc

    @pl.when(e_idx == pl.num_programs(1) - 1)
    def _finish():
        y = x1_ref[...] + acc_scr[...]
        ms = jnp.mean(y * y, axis=-1, keepdims=True)
        out_ref[...] = (y * lax.rsqrt(ms + RMS_EPS)) * nf_ref[...]


def _peer_dense(xn, ik, jk, gk, u, v, x1, norm_final, tm, te):
    n = xn.shape[0]
    n_exp = u.shape[0]
    row = lambda w: pl.BlockSpec((tm, w), lambda i, e: (i, 0), pipeline_mode=pl.Buffered(1))
    exp_spec = pl.BlockSpec((te, D_MODEL), lambda i, e: (e, 0))
    return pl.pallas_call(
        functools.partial(_peer_dense_kernel, tm=tm, te=te),
        grid=(n // tm, n_exp // te),
        in_specs=[row(D_MODEL), row(LANES), row(LANES), row(LANES), exp_spec, exp_spec, row(D_MODEL),
                  pl.BlockSpec((1, D_MODEL), lambda i, e: (0, 0))],
        out_specs=pl.BlockSpec((tm, D_MODEL), lambda i, e: (i, 0)),
        out_shape=jax.ShapeDtypeStruct((n, D_MODEL), F32),
        scratch_shapes=[pltpu.VMEM((tm * W_PITCH, LANES), F32), pltpu.VMEM((tm, D_MODEL), F32)],
        compiler_params=_params("parallel", "arbitrary"),
        name="peer_dense",
    )(xn, ik, jk, gk, u, v, x1, norm_final)


def _prep_w_in(w_in):
    w_main = jnp.concatenate([w_in[:, :COL_KV + 768], w_in[:, COL_KV + 768 + 3 * NSA_HEADS:]], axis=1)
    w_gate = jnp.pad(w_in[:, COL_KV + 768:COL_KV + 768 + 3 * NSA_HEADS], ((0, 0), (0, LANES - 3 * NSA_HEADS)))
    return jnp.concatenate([w_main, w_gate], axis=1).astype(BF16)


def _prep_cmp(cmp_wk, cmp_wv, cmp_pek, cmp_pev):
    def bd(w):
        z = jnp.zeros_like(w)
        return jnp.concatenate([jnp.concatenate([w, z], axis=2), jnp.concatenate([z, w], axis=2)], axis=1).astype(BF16)
    dup = lambda pe: jnp.concatenate([pe, pe], axis=1).astype(F32)
    return bd(cmp_wk), bd(cmp_wv), dup(cmp_pek), dup(cmp_pev)


def _ffn_tail(x, attn, ohg, w_out, norm_ffn, wq, keys, u, v, norm_final, tm, tms, tmd, te):
    x1, xn, pq = _outproj(x, attn, ohg, w_out, norm_ffn, wq, tm)
    ik, jk, gk = _peer_select(pq, keys, tms)
    return _peer_dense(xn, ik.T, jk.T, gk.T, u, v, x1, norm_final, tmd, te)


def kernel(x_prompt, x_sample, cache_cmp, cache_slc, cache_win, state_hgrn, page_table, norm_mix, w_in, cmp_wk, cmp_wv, cmp_pek, cmp_pev, hg_lb, hg_norm, w_out, norm_ffn, peer_wq, peer_keys, peer_u, peer_v, norm_final):
    b, t, d = x_prompt.shape
    db, tq, _ = x_sample.shape
    n_pool = cache_cmp.shape[1]
    n_pages = page_table.shape[1]
    past = n_pages * PAGE_ROWS
    wb = cache_win.shape[2]
    row = lambda a: a.reshape(1, -1)

    w_all = _prep_w_in(w_in[0])
    cw = _prep_cmp(cmp_wk[0], cmp_wv[0], cmp_pek[0], cmp_pev[0])
    w_out_b = w_out[0].astype(BF16)
    wq_b = peer_wq[0].astype(BF16)
    u_b = peer_u[0].astype(BF16)
    v_b = peer_v[0].astype(BF16)
    tail = lambda x, attn, ohg, tmd: _ffn_tail(x, attn, ohg, w_out_b, row(norm_ffn[0]), wq_b, peer_keys[0], u_b, v_b,
                                               row(norm_final), 512, 512, tmd, 512)

    cos, sin = _rope_tables(jnp.arange(t, dtype=jnp.int32))
    xp = x_prompt.reshape(b * t, d)
    q, qr, cmp_p, cmp_t, slc_t, win_t, ksb, kwb, vst, vwt, gates, hp = _inproj_seq(
        xp, row(norm_mix[0]), w_all, cos, sin, b, t, 512)
    kcvc = _compress_prompt(cmp_p.reshape(b, t, 256), cw)
    ncp = t // CMP_STRIDE
    attn_p = _nsa_prompt(q.reshape(b, t, -1), qr.reshape(b, t, -1), gates.reshape(b, t, -1), kcvc,
                         ksb.reshape(b, t, LANES), vst.reshape(b, t // KEY_TILE, LANES, KEY_TILE),
                         kwb.reshape(b, t, LANES), vwt.reshape(b, t // LANES, LANES, LANES),
                         _overlap_matrix(ncp, ncp - 1).T, _block_expand_matrix(t), 128)
    ohg_p, s_p = _hgrn(hp, jnp.zeros((b, HG_HEADS, HG_DK, HG_DV), F32), hg_lb, row(hg_norm[0]), b, t, 4, HG_ROWS)
    y_p = tail(xp, attn_p.reshape(b * t, -1), ohg_p, 512)

    pos_s = past + (jnp.arange(db * tq, dtype=jnp.int32) % tq)
    cos_s, sin_s = _rope_tables(pos_s)
    xs = x_sample.reshape(db * tq, d)
    q_s, qr_s, cmp_s, slc_s, win_s, _, _, gates_s, hs = _inproj(xs, row(norm_mix[0]), w_all, cos_s, sin_s, 256)
    kcvc_s = _compress_paged(_feature_major(cache_cmp, n_pool), page_table, cw)
    ncs = past // CMP_STRIDE
    attn_s, nwin_t = _nsa_sample(q_s, qr_s, gates_s, kcvc_s, _feature_major(cache_slc, n_pool), slc_s,
                                 _feature_major(cache_win, db), win_s, page_table,
                                 _overlap_matrix(ncs, ncs - 1), _block_expand_matrix(past + KEY_PAD), tq)
    ohg_s, s_s = _hgrn(hs, state_hgrn[0], hg_lb, row(hg_norm[0]), db, tq, 1, tq)
    y_s = tail(xs, attn_s, ohg_s, 512)

    kv5 = lambda a, nb, nt: a.reshape(1, nb, nt, 2, NSA_KV_HEADS, HEAD_DIM)
    kv5_t = lambda a_t, nb, nt: kv5(a_t.transpose(0, 2, 1), nb, nt)
    keep = min(WINDOW, t)
    return (y_p.reshape(b, t, d), y_s.reshape(db, tq, d),
            kv5_t(cmp_t, b, t), kv5_t(slc_t, b, t), kv5_t(win_t[:, :, t - keep:], b, keep), s_p[None],
            kv5(cmp_s, db, tq), kv5(slc_s, db, tq), kv5_t(nwin_t, db, wb), s_s[None])
```

```python
import functools

import numpy as np
import jax
import jax.numpy as jnp
from jax import lax
from jax.experimental import pallas as pl
from jax.experimental.pallas import tpu as pltpu

F32 = jnp.float32
BF16 = jnp.bfloat16

D_MODEL = 1024
HEAD_DIM = 64
NSA_HEADS = 8
NSA_KV_HEADS = 2
NSA_GROUP = NSA_HEADS // NSA_KV_HEADS
CMP_BLOCK = 32
CMP_STRIDE = 16
SLC_BLOCK = 64
SLC_TOPN = 16
WINDOW = 512
ROPE_THETA = 10000.0
HG_HEADS = 4
HG_DK = 128
HG_DV = 128
NSA_WIDTH = NSA_HEADS * HEAD_DIM
HG_WIDTH = HG_HEADS * HG_DV
KV_WIDTH = NSA_KV_HEADS * HEAD_DIM
PEER_HEADS = 8
PEER_KEYS = 128
PEER_QDIM = 256
PEER_TOPK = 16
RMS_EPS = 1e-6
NEG_INF = -1e30
FORCED_SCORE = 1e6
LOWEST = -3e38

LANES = 128
HALF = 64
VMEM_LIMIT = 56 * 1024 * 1024

COL_Q = 0
COL_KV = 512
COL_H = 1280
COL_G = 3328
PROJ_PAD = 3456


def _dot(a, b):
    return jnp.dot(a, b, preferred_element_type=F32)


def _dot_nt(a, b):
    return lax.dot_general(a, b, (((1,), (1,)), ((), ())), preferred_element_type=F32)


def _iota(shape, dim):
    return lax.broadcasted_iota(jnp.int32, shape, dim)


def _params(*sem):
    return pltpu.CompilerParams(dimension_semantics=sem, vmem_limit_bytes=VMEM_LIMIT)


def _rope_tile(x, cos, sin_signed, first_half):
    partner = jnp.where(first_half, pltpu.roll(x, LANES - 32, 1), pltpu.roll(x, 32, 1))
    return x * cos + partner * sin_signed


def _inproj_seq_kernel(x_ref, g_ref, w_ref, cos_ref, sin_ref,
                       q_ref, qr_ref, cmp_ref, cmpt_ref, slct_ref, wint_ref, ksb_ref, kwb_ref, vst_ref, vwt_ref,
                       gates_ref, h_ref):
    x = x_ref[...]
    ms = jnp.mean(x * x, axis=-1, keepdims=True)
    xn = (x * lax.rsqrt(ms + RMS_EPS)) * g_ref[...]
    proj = _dot(xn.astype(BF16), w_ref[...])
    cos = cos_ref[...]
    sin = sin_ref[...]
    first_half = (_iota(cos.shape, 1) & (HALF - 1)) < 32
    rope = lambda t: _rope_tile(t, cos, sin, first_half)
    q_ref[...] = proj[:, COL_Q:COL_Q + NSA_WIDTH]
    for c in range(NSA_WIDTH // LANES):
        qr_ref[:, c * LANES:(c + 1) * LANES] = rope(proj[:, COL_Q + c * LANES:COL_Q + (c + 1) * LANES])
    cmp = proj[:, COL_KV:COL_KV + 256]
    cmp_ref[...] = cmp
    cmpt_ref[...] = cmp.T
    ks = rope(proj[:, COL_KV + 256:COL_KV + 384])
    kw = rope(proj[:, COL_KV + 512:COL_KV + 640])
    vs_t = proj[:, COL_KV + 384:COL_KV + 512].T
    vw_t = proj[:, COL_KV + 640:COL_KV + 768].T
    slct_ref[0:LANES, :] = ks.T
    slct_ref[LANES:2 * LANES, :] = vs_t
    wint_ref[0:LANES, :] = kw.T
    wint_ref[LANES:2 * LANES, :] = vw_t
    ksb_ref[...] = ks.astype(BF16)
    kwb_ref[...] = kw.astype(BF16)
    for c in range(vst_ref.shape[0]):
        vst_ref[c] = vs_t[:, c * KEY_TILE:(c + 1) * KEY_TILE].astype(BF16)
    for c in range(vwt_ref.shape[0]):
        vwt_ref[c] = vw_t[:, c * LANES:(c + 1) * LANES].astype(BF16)
    gates_ref[...] = jax.nn.sigmoid(proj[:, COL_G:COL_G + LANES])
    h_ref[...] = proj[:, COL_H:COL_H + 4 * HG_WIDTH]


def _inproj_seq(x, norm_g, w_all, cos_t, sin_t, nb, t, tm):
    n = nb * t
    steps = t // tm
    row = lambda w: pl.BlockSpec((tm, w), lambda i: (i, 0))
    tab = pl.BlockSpec((tm, LANES), lambda i: (i % steps, 0))
    feat = pl.BlockSpec((None, 256, tm), lambda i: (i // steps, 0, i % steps))
    tiles = lambda w: pl.BlockSpec((tm // w, LANES, w), lambda i: (i, 0, 0))
    f32 = lambda *s: jax.ShapeDtypeStruct(s, F32)
    bf = lambda *s: jax.ShapeDtypeStruct(s, BF16)
    return pl.pallas_call(
        _inproj_seq_kernel,
        grid=(n // tm,),
        in_specs=[row(D_MODEL), pl.BlockSpec((1, D_MODEL), lambda i: (0, 0)),
                  pl.BlockSpec((D_MODEL, PROJ_PAD), lambda i: (0, 0)), tab, tab],
        out_specs=[row(NSA_WIDTH), row(NSA_WIDTH), row(256), feat, feat, feat, row(LANES), row(LANES),
                   tiles(KEY_TILE), tiles(LANES), row(LANES), row(4 * HG_WIDTH)],
        out_shape=[f32(n, NSA_WIDTH), f32(n, NSA_WIDTH), f32(n, 256), f32(nb, 256, t), f32(nb, 256, t),
                   f32(nb, 256, t), bf(n, LANES), bf(n, LANES), bf(n // KEY_TILE, LANES, KEY_TILE),
                   bf(n // LANES, LANES, LANES), f32(n, LANES), f32(n, 4 * HG_WIDTH)],
        compiler_params=_params("parallel"),
        name="inproj_seq",
    )(x, norm_g, w_all, cos_t, sin_t)


def _inproj_kernel(x_ref, g_ref, w_ref, cos_ref, sin_ref,
                   q_ref, qr_ref, cmp_ref, slc_ref, win_ref, slcb_ref, winb_ref, gates_ref, h_ref):
    x = x_ref[...]
    ms = jnp.mean(x * x, axis=-1, keepdims=True)
    xn = (x * lax.rsqrt(ms + RMS_EPS)) * g_ref[...]
    proj = _dot(xn.astype(BF16), w_ref[...])
    cos = cos_ref[...]
    sin = sin_ref[...]
    first_half = (_iota(cos.shape, 1) & (HALF - 1)) < 32
    rope = lambda t: _rope_tile(t, cos, sin, first_half)
    q_ref[...] = proj[:, COL_Q:COL_Q + NSA_WIDTH]
    for c in range(NSA_WIDTH // LANES):
        qr_ref[:, c * LANES:(c + 1) * LANES] = rope(proj[:, COL_Q + c * LANES:COL_Q + (c + 1) * LANES])
    cmp_ref[...] = proj[:, COL_KV:COL_KV + 256]
    ks = rope(proj[:, COL_KV + 256:COL_KV + 384])
    vs = proj[:, COL_KV + 384:COL_KV + 512]
    kw = rope(proj[:, COL_KV + 512:COL_KV + 640])
    vw = proj[:, COL_KV + 640:COL_KV + 768]
    slc_ref[:, 0:LANES] = ks
    slc_ref[:, LANES:2 * LANES] = vs
    win_ref[:, 0:LANES] = kw
    win_ref[:, LANES:2 * LANES] = vw
    slcb_ref[:, 0:LANES] = ks.astype(BF16)
    slcb_ref[:, LANES:2 * LANES] = vs.astype(BF16)
    winb_ref[:, 0:LANES] = kw.astype(BF16)
    winb_ref[:, LANES:2 * LANES] = vw.astype(BF16)
    gates_ref[...] = jax.nn.sigmoid(proj[:, COL_G:COL_G + LANES])
    h_ref[...] = proj[:, COL_H:COL_H + 4 * HG_WIDTH]


def _inproj(x, norm_g, w_all, cos_t, sin_t, tm):
    n = x.shape[0]
    ntab = cos_t.shape[0] // tm
    row = lambda w: pl.BlockSpec((tm, w), lambda i: (i, 0))
    tab = pl.BlockSpec((tm, LANES), lambda i: (i % ntab, 0))
    out_shapes = [
        jax.ShapeDtypeStruct((n, NSA_WIDTH), F32), jax.ShapeDtypeStruct((n, NSA_WIDTH), F32),
        jax.ShapeDtypeStruct((n, 256), F32), jax.ShapeDtypeStruct((n, 256), F32),
        jax.ShapeDtypeStruct((n, 256), F32), jax.ShapeDtypeStruct((n, 256), BF16),
        jax.ShapeDtypeStruct((n, 256), BF16), jax.ShapeDtypeStruct((n, LANES), F32),
        jax.ShapeDtypeStruct((n, 4 * HG_WIDTH), F32),
    ]
    return pl.pallas_call(
        _inproj_kernel,
        grid=(n // tm,),
        in_specs=[row(D_MODEL), pl.BlockSpec((1, D_MODEL), lambda i: (0, 0)),
                  pl.BlockSpec((D_MODEL, PROJ_PAD), lambda i: (0, 0)), tab, tab],
        out_specs=[row(NSA_WIDTH), row(NSA_WIDTH), row(256), row(256), row(256), row(256), row(256),
                   row(LANES), row(4 * HG_WIDTH)],
        out_shape=out_shapes,
        compiler_params=_params("parallel"),
        name="inproj",
    )(x, norm_g, w_all, cos_t, sin_t)


def _rope_tables(pos):
    half = HEAD_DIM // 2
    inv = ROPE_THETA ** (-jnp.arange(half, dtype=F32) / half)
    ang = pos.astype(F32)[:, None] * inv[None, :]
    cos = jnp.tile(jnp.cos(ang), (1, 4))
    sin = jnp.sin(ang)
    return cos, jnp.tile(jnp.concatenate([-sin, sin], axis=1), (1, 2))


def _compress_rows(rows_ref, n_out, w_ref, pe_ref, pitch=CMP_STRIDE):
    a = jnp.zeros((n_out, LANES), F32)
    b = jnp.zeros((n_out, LANES), F32)
    for j in range(CMP_STRIDE):
        xj = rows_ref[pl.ds(j, n_out, stride=pitch), :]
        a = a + _dot((xj + pe_ref[j:j + 1, :]).astype(BF16), w_ref[j])
        b = b + _dot((xj + pe_ref[CMP_STRIDE + j:CMP_STRIDE + j + 1, :]).astype(BF16), w_ref[CMP_STRIDE + j])
    return a + pltpu.roll(b, n_out - 1, 0)


def _compress_prompt_kernel(rk_ref, rv_ref, wk_ref, wv_ref, pek_ref, pev_ref, out_ref):
    n_out = out_ref.shape[0]
    out_ref[:, 0:LANES] = _compress_rows(rk_ref, n_out, wk_ref, pek_ref).astype(BF16)
    out_ref[:, LANES:2 * LANES] = _compress_rows(rv_ref, n_out, wv_ref, pev_ref).astype(BF16)


def _compress_prompt(rows, cw):
    b, t, _ = rows.shape
    n_out = t // CMP_STRIDE
    wspec = pl.BlockSpec((CMP_BLOCK, LANES, LANES), lambda i: (0, 0, 0))
    pspec = pl.BlockSpec((CMP_BLOCK, LANES), lambda i: (0, 0))
    return pl.pallas_call(
        _compress_prompt_kernel,
        grid=(b,),
        in_specs=[pl.BlockSpec((None, t, LANES), lambda i: (i, 0, 0)),
                  pl.BlockSpec((None, t, LANES), lambda i: (i, 0, 1)), wspec, wspec, pspec, pspec],
        out_specs=pl.BlockSpec((None, n_out, 256), lambda i: (i, 0, 0)),
        out_shape=jax.ShapeDtypeStruct((b, n_out, 256), BF16),
        compiler_params=_params("parallel"),
        name="compress_prompt",
    )(rows, rows, *cw)


def _stack_heads(ref, lo_half, nq):
    hi_half = jnp.logical_not(lo_half)
    parts = []
    for h in range(NSA_HEADS):
        g = h // NSA_GROUP
        tile = ref[:, LANES * (h // 2):LANES * (h // 2 + 1)]
        if h % 2 != g:
            tile = pltpu.roll(tile, HALF, 1)
        parts.append(jnp.where(lo_half if g == 0 else hi_half, tile, 0.0))
    return (jnp.concatenate(parts, axis=0) * (HEAD_DIM ** -0.5)).astype(BF16)


def _per_head(x, nq):
    return jnp.concatenate([x[0:nq]] * NSA_GROUP + [x[nq:2 * nq]] * NSA_GROUP, axis=0)


def _masked_softmax_rows(s, valid):
    sm = jnp.where(valid, s, NEG_INF)
    m = jnp.max(sm, axis=-1, keepdims=True)
    e = jnp.where(valid, jnp.exp(sm - m), 0.0)
    l = jnp.sum(e, axis=-1, keepdims=True)
    return e / jnp.where(l > 0.0, l, 1.0)


def _select_blocks(imp, pos_col):
    blk = _iota(imp.shape, 1)
    cur = pos_col >> 6
    forced = (blk == 0) | (blk == cur) | (blk == cur - 1)
    causal = (blk << 6) <= pos_col
    score = jnp.where(causal, jnp.where(forced, FORCED_SCORE, imp), -FORCED_SCORE)
    return score, causal


def _topk_mask_t(score_t, n_sel):
    bi = _iota(score_t.shape, 0)
    sel = jnp.zeros(score_t.shape, F32)
    sc = score_t
    for _ in range(n_sel):
        m = jnp.max(sc, axis=0, keepdims=True)
        idx = jnp.min(jnp.where(sc == m, bi, LANES), axis=0, keepdims=True)
        hit = bi == idx
        sel = jnp.where(hit, 1.0, sel)
        sc = jnp.where(hit, LOWEST, sc)
    return sel


def _cmp_branch(qn, kc, vc, ov, pos_rows, pos_grp, nq):
    s = _dot_nt(qn, kc)
    valid = (_iota(s.shape, 1) * CMP_STRIDE + (CMP_BLOCK - 1)) <= pos_rows
    p = _masked_softmax_rows(s, valid)
    o_c = _dot(p.astype(BF16), vc)
    slab = lambda h: p[h * nq:(h + 1) * nq]
    psum = jnp.concatenate([slab(0) + slab(1) + slab(2) + slab(3), slab(4) + slab(5) + slab(6) + slab(7)], axis=0)
    hi = psum.astype(BF16)
    lo = (psum - hi.astype(F32)).astype(BF16)
    imp = _dot(hi, ov) + _dot(lo, ov)
    score, causal = _select_blocks(imp, pos_grp)
    return o_c, score, causal


def _combine_heads(out_ref, gates, lo_half, o_c, o_s, o_w, nq):
    combs = []
    for h in range(NSA_HEADS):
        rows = slice(h * nq, (h + 1) * nq)
        comb = (gates[:, h:h + 1] * o_c[rows] + gates[:, NSA_HEADS + h:NSA_HEADS + h + 1] * o_s[rows]
                + gates[:, 2 * NSA_HEADS + h:2 * NSA_HEADS + h + 1] * o_w[rows])
        if h % 2 != h // NSA_GROUP:
            comb = pltpu.roll(comb, HALF, 1)
        combs.append(comb)
    for r in range(NSA_HEADS // 2):
        out_ref[:, LANES * r:LANES * (r + 1)] = jnp.where(lo_half, combs[2 * r], combs[2 * r + 1])


KEY_TILE = 512


def _masked_softmax_cols(s, valid):
    sm = jnp.where(valid, s, NEG_INF)
    m = jnp.max(sm, axis=0, keepdims=True)
    e = jnp.where(valid, jnp.exp(sm - m), 0.0)
    l = jnp.sum(e, axis=0, keepdims=True)
    return e / jnp.where(l > 0.0, l, 1.0)


def _nsa_prompt_kernel(q_ref, qr_ref, gates_ref, cmp_ref, vct_ref, ks_ref, vst_ref, kw_ref, vwt_ref, ovt_ref, et_ref,
                       out_ref, *, tq):
    t0 = pl.program_id(1) * tq
    cols = NSA_HEADS * tq
    grp = NSA_KV_HEADS * tq
    pos_cols = t0 + (_iota((1, cols), 1) & (tq - 1))
    pos_grp = t0 + (_iota((1, grp), 1) & (tq - 1))
    n_tiles = (t0 + tq + KEY_TILE - 1) // KEY_TILE
    w_chunk = jnp.maximum(t0 - WINDOW, 0) // LANES
    w_start = pl.multiple_of(w_chunk * LANES, LANES)
    w_len = WINDOW + tq
    per_head = lambda x: jnp.concatenate([x[:, 0:tq]] * NSA_GROUP + [x[:, tq:grp]] * NSA_GROUP, axis=1)

    lo_half = _iota((tq, LANES), 1) < HALF
    qn = _stack_heads(q_ref, lo_half, tq)
    qr = _stack_heads(qr_ref, lo_half, tq)

    all_heads = lambda x: jnp.concatenate([x] * NSA_HEADS, axis=1)
    pos_tok = t0 + _iota((1, tq), 1)

    sc = _dot_nt(cmp_ref[:, 0:LANES], qn)
    n_i = _iota((sc.shape[0], tq), 0)
    sc = sc + all_heads(jnp.where(n_i * CMP_STRIDE + (CMP_BLOCK - 1) <= pos_tok, 0.0, NEG_INF))
    ec = jnp.exp(sc - jnp.max(sc, axis=0, keepdims=True))
    norm_c = jnp.where(pos_cols >= CMP_BLOCK - 1, 1.0 / jnp.sum(ec, axis=0, keepdims=True), 0.0)
    o_c = _dot(vct_ref[...], ec.astype(BF16)) * norm_c
    pc = ec * norm_c
    slab = lambda h: pc[:, h * tq:(h + 1) * tq]
    psum = jnp.concatenate([slab(0) + slab(1) + slab(2) + slab(3), slab(4) + slab(5) + slab(6) + slab(7)], axis=1)
    hi = psum.astype(BF16)
    lo = (psum - hi.astype(F32)).astype(BF16)
    ovt = ovt_ref[...]
    imp = _dot(ovt, hi) + _dot(ovt, lo)
    blk = _iota(imp.shape, 0)
    cur = pos_grp >> 6
    forced = (blk == 0) | (blk == cur) | (blk == cur - 1)
    causal = (blk << 6) <= pos_grp
    score = jnp.where(causal, jnp.where(forced, FORCED_SCORE, imp), -FORCED_SCORE)
    sel = jnp.where(causal, _topk_mask_t(score, SLC_TOPN), 0.0)
    notsel = (1.0 - sel).astype(BF16)

    sw = _dot_nt(kw_ref[pl.ds(w_start, w_len), :], qr)
    kp = w_start + _iota((w_len, tq), 0)
    sw = sw + all_heads(jnp.where((kp <= pos_tok) & (kp >= pos_tok - WINDOW), 0.0, NEG_INF))
    ew = jnp.exp(sw - jnp.max(sw, axis=0, keepdims=True))
    vwt = jnp.concatenate([vwt_ref[w_chunk + c] for c in range(w_len // LANES)], axis=1)
    o_w = _dot(vwt, ew.astype(BF16)) / jnp.sum(ew, axis=0, keepdims=True)

    def tile_step(kt, carry, diag):
        m, l, acc = carry
        k0 = pl.multiple_of(kt * KEY_TILE, KEY_TILE)
        s = _dot_nt(ks_ref[pl.ds(k0, KEY_TILE), :], qr)
        s = s + per_head(_dot(et_ref[pl.ds(k0, KEY_TILE), :], notsel) * NEG_INF)
        if diag:
            s = s + all_heads(jnp.where(k0 + _iota((KEY_TILE, tq), 0) <= pos_tok, 0.0, NEG_INF))
        m_new = jnp.maximum(m, jnp.max(s, axis=0, keepdims=True))
        alpha = jnp.exp(m - m_new)
        p = jnp.exp(s - m_new)
        l = alpha * l + jnp.sum(p, axis=0, keepdims=True)
        acc = alpha * acc + _dot(vst_ref[kt], p.astype(BF16))
        return m_new, l, acc

    init = (jnp.full((1, cols), NEG_INF, F32), jnp.zeros((1, cols), F32), jnp.zeros((LANES, cols), F32))
    carry = lax.fori_loop(0, n_tiles - 1, lambda kt, c: tile_step(kt, c, False), init)
    _, l, acc = tile_step(n_tiles - 1, carry, True)
    o_s = acc / l

    gt = gates_ref[...].T
    for r in range(NSA_HEADS // 2):
        halves = []
        for h in (2 * r, 2 * r + 1):
            c = slice(h * tq, (h + 1) * tq)
            comb = (gt[h:h + 1] * o_c[:, c] + gt[NSA_HEADS + h:NSA_HEADS + h + 1] * o_s[:, c]
                    + gt[2 * NSA_HEADS + h:2 * NSA_HEADS + h + 1] * o_w[:, c])
            g = h // NSA_GROUP
            halves.append(comb[HALF * g:HALF * (g + 1)])
        out_ref[:, LANES * r:LANES * (r + 1)] = jnp.concatenate(halves, axis=0).T


def _nsa_prompt(q, qr, gates, kcvc, ksb, vst, kwb, vwt, ovt, et, tq):
    b, t, _ = q.shape
    ncp = kcvc.shape[1]
    vct = kcvc[:, :, LANES:].transpose(0, 2, 1)
    blk = lambda w: pl.BlockSpec((None, tq, w), lambda bi, i: (bi, i, 0))
    full = lambda r, w: pl.BlockSpec((None, r, w), lambda bi, i: (bi, 0, 0))
    full4 = lambda a: pl.BlockSpec((None,) + a.shape[1:], lambda bi, i: (bi, 0, 0, 0))
    return pl.pallas_call(
        functools.partial(_nsa_prompt_kernel, tq=tq),
        grid=(b, t // tq),
        in_specs=[blk(NSA_WIDTH), blk(NSA_WIDTH), blk(LANES), full(ncp, 256), full(LANES, ncp), full(t, LANES),
                  full4(vst), full(t, LANES), full4(vwt),
                  pl.BlockSpec((LANES, ncp), lambda bi, i: (0, 0)),
                  pl.BlockSpec((t, LANES), lambda bi, i: (0, 0))],
        out_specs=blk(NSA_WIDTH),
        out_shape=jax.ShapeDtypeStruct((b, t, NSA_WIDTH), F32),
        compiler_params=_params("parallel", "arbitrary"),
        name="nsa_prompt",
    )(q, qr, gates, kcvc, vct, ksb, vst, kwb, vwt, ovt, et)


def _overlap_matrix(n_cmp_pad, n_cmp):
    c0 = jnp.arange(n_cmp_pad, dtype=jnp.int32)[:, None] * CMP_STRIDE
    s0 = jnp.arange(LANES, dtype=jnp.int32)[None, :] * SLC_BLOCK
    real = jnp.arange(n_cmp_pad, dtype=jnp.int32)[:, None] < n_cmp
    return ((c0 < s0 + SLC_BLOCK) & (c0 + CMP_BLOCK > s0) & real).astype(BF16)


def _block_expand_matrix(n_keys):
    r = jnp.arange(n_keys, dtype=jnp.int32)[:, None] // SLC_BLOCK
    return (r == jnp.arange(LANES, dtype=jnp.int32)[None, :]).astype(BF16)


PAGE_ROWS = 128
CMP_SEQS_PER_STEP = 4
GROUP_PITCH = 24


def _compress_paged_kernel(pt_ref, *refs, n_pages):
    pages = refs[:n_pages]
    wk_ref, wv_ref, pek_ref, pev_ref, out_ref, rk_scr, rv_scr = refs[n_pages:]
    groups = PAGE_ROWS // CMP_STRIDE
    for p in range(n_pages):
        for scr, half in ((rk_scr, 0), (rv_scr, 1)):
            rows = pages[p][half * LANES:(half + 1) * LANES, :].T
            for m in range(groups):
                r0 = (p * groups + m) * GROUP_PITCH
                scr[r0:r0 + CMP_STRIDE, :] = rows[m * CMP_STRIDE:(m + 1) * CMP_STRIDE]
    n_out = out_ref.shape[0]
    out_ref[:, 0:LANES] = _compress_rows(rk_scr, n_out, wk_ref, pek_ref, GROUP_PITCH).astype(BF16)
    out_ref[:, LANES:2 * LANES] = _compress_rows(rv_scr, n_out, wv_ref, pev_ref, GROUP_PITCH).astype(BF16)


def _page_specs(n_pages):
    return [pl.BlockSpec((None, 256, PAGE_ROWS), functools.partial(lambda i, pt, p: (pt[i * n_pages + p], 0, 0), p=p))
            for p in range(n_pages)]


def _feature_major(cache, lead):
    return cache[0].reshape(lead, cache.shape[2], 256).transpose(0, 2, 1)


def _compress_paged(cache, page_table, cw):
    db, n_pages = page_table.shape
    past = n_pages * PAGE_ROWS
    n_out = past // CMP_STRIDE
    wspec = pl.BlockSpec((CMP_BLOCK, LANES, LANES), lambda i, pt: (0, 0, 0))
    pspec = pl.BlockSpec((CMP_BLOCK, LANES), lambda i, pt: (0, 0))
    grid_spec = pltpu.PrefetchScalarGridSpec(
        num_scalar_prefetch=1, grid=(db,),
        in_specs=_page_specs(n_pages) + [wspec, wspec, pspec, pspec],
        out_specs=pl.BlockSpec((None, n_out, 256), lambda i, pt: (i, 0, 0)),
        scratch_shapes=[pltpu.VMEM((n_out * GROUP_PITCH, LANES), F32), pltpu.VMEM((n_out * GROUP_PITCH, LANES), F32)])
    return pl.pallas_call(
        functools.partial(_compress_paged_kernel, n_pages=n_pages),
        grid_spec=grid_spec,
        out_shape=jax.ShapeDtypeStruct((db, n_out, 256), BF16),
        compiler_params=_params("arbitrary"),
        name="compress_paged",
    )(page_table.reshape(-1), *([cache] * n_pages), *cw)


KEY_PAD = LANES


def _nsa_sample_kernel(pt_ref, q_ref, qr_ref, gates_ref, cmp_ref, *refs, n_pages):
    pages = refs[:n_pages]
    snew_ref, cwin_ref, wnew_ref, ov_ref, et_ref, out_ref, nwin_ref = refs[n_pages:]
    nq = q_ref.shape[0]
    past = n_pages * PAGE_ROWS
    wb = cwin_ref.shape[1]
    rows = NSA_HEADS * nq
    pad_rows = lambda x: jnp.concatenate([x, jnp.zeros((KEY_PAD - nq, x.shape[1]), F32)], axis=0)
    snew = pad_rows(snew_ref[...])
    wnew = pad_rows(wnew_ref[...])

    shifted = pltpu.roll(cwin_ref[...], wb - nq, 1)
    new_t = jnp.concatenate([wnew[:, 0:LANES].T, wnew[:, LANES:2 * LANES].T], axis=0)
    tail = jnp.where(_iota((256, LANES), 1) >= LANES - nq, pltpu.roll(new_t, LANES - nq, 1), shifted[:, wb - LANES:wb])
    nwin_ref[:, 0:wb - LANES] = shifted[:, 0:wb - LANES]
    nwin_ref[:, wb - LANES:wb] = tail

    lo_half = _iota((nq, LANES), 1) < HALF
    pos_rows = past + (_iota((rows, 1), 0) & (nq - 1))
    n_grp = NSA_KV_HEADS * nq
    pos_grp = past + (_iota((n_grp, 1), 0) & (nq - 1))
    qn = _stack_heads(q_ref, lo_half, nq)
    qr = _stack_heads(qr_ref, lo_half, nq)
    o_c, score, causal = _cmp_branch(qn, cmp_ref[:, 0:LANES], cmp_ref[:, LANES:2 * LANES], ov_ref[...],
                                     pos_rows, pos_grp, nq)
    score_sq = jnp.concatenate([score, jnp.full((LANES - n_grp, LANES), LOWEST, F32)], axis=0)
    sel = jnp.where(causal, _topk_mask_t(score_sq.T, SLC_TOPN).T[0:n_grp], 0.0)
    notsel = (1.0 - sel).astype(BF16)

    s = jnp.concatenate([_dot(qr, pages[p][0:LANES, :].astype(BF16)) for p in range(n_pages)]
                        + [_dot_nt(qr, snew[:, 0:LANES].astype(BF16))], axis=1)
    s = s + _per_head(_dot_nt(notsel, et_ref[...]) * NEG_INF, nq)
    s = jnp.where(_iota(s.shape, 1) <= pos_rows, s, NEG_INF)
    e = jnp.exp(s - jnp.max(s, axis=-1, keepdims=True))
    eb = e.astype(BF16)
    o_s = _dot(eb[:, past:past + KEY_PAD], snew[:, LANES:2 * LANES].astype(BF16))
    for p in range(n_pages):
        o_s = o_s + _dot_nt(eb[:, p * PAGE_ROWS:(p + 1) * PAGE_ROWS], pages[p][LANES:2 * LANES, :].astype(BF16))
    o_s = o_s / jnp.sum(e, axis=-1, keepdims=True)

    sw = jnp.concatenate([_dot(qr, cwin_ref[0:LANES, :].astype(BF16)), _dot_nt(qr, wnew[:, 0:LANES].astype(BF16))], axis=1)
    kp = (past - wb) + _iota(sw.shape, 1)
    pw = _masked_softmax_rows(sw, (kp <= pos_rows) & (kp >= pos_rows - WINDOW) & (kp >= 0)).astype(BF16)
    o_w = (_dot_nt(pw[:, 0:wb], cwin_ref[LANES:2 * LANES, :].astype(BF16))
           + _dot(pw[:, wb:wb + KEY_PAD], wnew[:, LANES:2 * LANES].astype(BF16)))
    _combine_heads(out_ref, gates_ref[...], lo_half, o_c, o_s, o_w, nq)


def _nsa_sample(q, qr, gates, kcvc, cache_slc, slc_new, cache_win, win_new, page_table, ov, et, nq):
    db, n_pages = page_table.shape
    past = n_pages * PAGE_ROWS
    wb = cache_win.shape[2]
    ncp = kcvc.shape[1]
    blk = lambda w: pl.BlockSpec((nq, w), lambda i, pt: (i, 0))
    win_spec = pl.BlockSpec((None, 256, wb), lambda i, pt: (i, 0, 0))
    grid_spec = pltpu.PrefetchScalarGridSpec(
        num_scalar_prefetch=1, grid=(db,),
        in_specs=[blk(NSA_WIDTH), blk(NSA_WIDTH), blk(LANES),
                  pl.BlockSpec((None, ncp, 256), lambda i, pt: (i, 0, 0))] + _page_specs(n_pages) + [
                  blk(256), win_spec, blk(256),
                  pl.BlockSpec((ncp, LANES), lambda i, pt: (0, 0)),
                  pl.BlockSpec((past + KEY_PAD, LANES), lambda i, pt: (0, 0))],
        out_specs=[blk(NSA_WIDTH), win_spec])
    return pl.pallas_call(
        functools.partial(_nsa_sample_kernel, n_pages=n_pages),
        grid_spec=grid_spec,
        out_shape=[jax.ShapeDtypeStruct((db * nq, NSA_WIDTH), F32), jax.ShapeDtypeStruct((db, 256, wb), F32)],
        compiler_params=_params("arbitrary"),
        name="nsa_sample",
    )(page_table.reshape(-1), q, qr, gates, kcvc, *([cache_slc] * n_pages), slc_new, cache_win, win_new, ov, et)


HG_ROWS = 128
HG_SUB = 16


def _dot_split3(m_bf16, x):
    a = x.astype(BF16)
    r = x - a.astype(F32)
    b = r.astype(BF16)
    c = (r - b.astype(F32)).astype(BF16)
    return _dot(m_bf16, a) + _dot(m_bf16, b) + _dot(m_bf16, c)


def _hgrn_chunk(q, fpre, v, gpre, lb, ng, s0, n_real):
    f = lb + (1.0 - lb) * jax.nn.sigmoid(fpre)
    logf = jnp.log(f)
    kk = 1.0 - f
    if n_real < HG_ROWS:
        pad = lambda a: jnp.concatenate([a, jnp.zeros((HG_ROWS - n_real, LANES), F32)], axis=0)
        q, logf, kk, v = pad(q), pad(logf), pad(kk), pad(v)
    r_i = _iota((HG_ROWS, HG_ROWS), 0)
    c_i = _iota((HG_ROWS, HG_ROWS), 1)
    tri_b = r_i >= c_i
    tri = jnp.where(tri_b, 1.0, 0.0).astype(BF16)
    cum = _dot_split3(tri, logf)
    cprev = cum - logf
    cum_last = cum[HG_ROWS - 1:HG_ROWS, :]
    s0b = s0.astype(BF16)
    o = _dot((q * jnp.exp(cum)).astype(BF16), s0b)
    row = _iota((HG_ROWS, LANES), 0)
    a_rows = []
    n_sub = -(-n_real // HG_SUB)
    for i in range(n_sub):
        r0 = i * HG_SUB
        c_ref = cprev[r0:r0 + 1, :]
        qt = q[r0:r0 + HG_SUB] * jnp.exp(cum[r0:r0 + HG_SUB] - c_ref)
        kt = kk * jnp.exp(jnp.where(row < r0 + HG_SUB, c_ref - cum, NEG_INF))
        a_rows.append(_dot_nt(qt.astype(BF16), kt.astype(BF16)))
    if n_sub * HG_SUB < HG_ROWS:
        a_rows.append(jnp.zeros((HG_ROWS - n_sub * HG_SUB, HG_ROWS), F32))
    a = jnp.where(tri_b, jnp.concatenate(a_rows, axis=0), 0.0)
    vb = v.astype(BF16)
    o = o + _dot(a.astype(BF16), vb)
    kdec = kk * jnp.exp(cum_last - cum)
    scale = jnp.broadcast_to(jnp.exp(cum_last), (HG_ROWS, LANES)).T
    s_new = scale * s0 + _dot(kdec.T.astype(BF16), vb)
    o = o[0:n_real]
    o = o * lax.rsqrt(jnp.mean(o * o, axis=-1, keepdims=True) + RMS_EPS) * ng
    return o * jax.nn.sigmoid(gpre), s_new


def _hgrn_kernel(q_ref, f_ref, v_ref, g_ref, s0_ref, lb_ref, ng_ref, o_ref, sfin_ref, s_scr, *, n_chunks, n_real):
    @pl.when(pl.program_id(1) == 0)
    def _init():
        s_scr[...] = s0_ref[...]

    lbp = lb_ref[...]
    e = jnp.exp(lbp - jnp.max(lbp, axis=0, keepdims=True))
    lb = e[0:1, :] / jnp.sum(e, axis=0, keepdims=True)
    ng = ng_ref[...]

    def chunk(c, carry):
        rows = slice(0, n_real) if n_chunks == 1 else pl.ds(pl.multiple_of(c * n_real, n_real), n_real)
        for hd in range(HG_HEADS):
            cols = slice(hd * LANES, (hd + 1) * LANES)
            o, s_new = _hgrn_chunk(q_ref[rows, cols], f_ref[rows, cols], v_ref[rows, cols], g_ref[rows, cols],
                                   lb[:, cols], ng[:, cols], s_scr[hd], n_real)
            o_ref[rows, cols] = o
            s_scr[hd] = s_new
        return carry

    if n_chunks == 1:
        chunk(0, 0)
    else:
        lax.fori_loop(0, n_chunks, chunk, 0)
    sfin_ref[...] = s_scr[...]


def _hgrn(h, s0, hg_lb, hg_norm, nb, t, n_chunks, n_real):
    rows = n_chunks * n_real
    steps = t // rows
    col = lambda sec: pl.BlockSpec((rows, HG_WIDTH), lambda b, j: (b * steps + j, sec))
    st = pl.BlockSpec((None, HG_HEADS, HG_DK, HG_DV), lambda b, j: (b, 0, 0, 0))
    return pl.pallas_call(
        functools.partial(_hgrn_kernel, n_chunks=n_chunks, n_real=n_real),
        grid=(nb, steps),
        in_specs=[col(0), col(1), col(2), col(3), st,
                  pl.BlockSpec(hg_lb.shape, lambda b, j: (0, 0)),
                  pl.BlockSpec((1, HG_WIDTH), lambda b, j: (0, 0))],
        out_specs=[pl.BlockSpec((rows, HG_WIDTH), lambda b, j: (b * steps + j, 0)), st],
        out_shape=[jax.ShapeDtypeStruct((nb * t, HG_WIDTH), F32),
                   jax.ShapeDtypeStruct((nb, HG_HEADS, HG_DK, HG_DV), F32)],
        scratch_shapes=[pltpu.VMEM((HG_HEADS, HG_DK, HG_DV), F32)],
        compiler_params=_params("parallel", "arbitrary"),
        name="hgrn",
    )(h, h, h, h, s0, hg_lb, hg_norm)


def _outproj_kernel(x_ref, a_ref, hg_ref, wo_ref, nf_ref, wq_ref, x1_ref, xn_ref, pq_ref):
    mix = jnp.concatenate([a_ref[...], hg_ref[...]], axis=1).astype(BF16)
    x1 = x_ref[...] + _dot(mix, wo_ref[...])
    x1_ref[...] = x1
    ms = jnp.mean(x1 * x1, axis=-1, keepdims=True)
    xb = ((x1 * lax.rsqrt(ms + RMS_EPS)) * nf_ref[...]).astype(BF16)
    xn_ref[...] = xb
    pq_ref[...] = _dot(xb, wq_ref[...])


def _outproj(x, attn, ohg, w_out, norm_ffn, wq, tm):
    n = x.shape[0]
    row = lambda w: pl.BlockSpec((tm, w), lambda i: (i, 0))
    full = lambda a: pl.BlockSpec(a.shape, lambda i: (0, 0))
    nq = wq.shape[1]
    return pl.pallas_call(
        _outproj_kernel,
        grid=(n // tm,),
        in_specs=[row(D_MODEL), row(NSA_WIDTH), row(HG_WIDTH), full(w_out), full(norm_ffn), full(wq)],
        out_specs=[row(D_MODEL), row(D_MODEL), row(nq)],
        out_shape=[jax.ShapeDtypeStruct((n, D_MODEL), F32), jax.ShapeDtypeStruct((n, D_MODEL), BF16),
                   jax.ShapeDtypeStruct((n, nq), F32)],
        compiler_params=_params("parallel"),
        name="outproj",
    )(x, attn, ohg, w_out, norm_ffn, wq)


def _top_rows(s, k, val_scr, idx_scr):
    bi = _iota(s.shape, 0)
    big = s.shape[0]
    for a in range(k):
        m = jnp.max(s, axis=0, keepdims=True)
        idx = jnp.min(jnp.where(s == m, bi, big), axis=0, keepdims=True)
        val_scr[a:a + 1, :] = m
        idx_scr[a:a + 1, :] = idx
        s = jnp.where(bi == idx, LOWEST, s)


def _peer_select_kernel(pq_ref, keys_ref, i_ref, j_ref, g_ref, v12, i12, sc, cd):
    tms = pq_ref.shape[0]
    half = PEER_QDIM // 2
    s1 = _dot_nt(keys_ref[0].astype(BF16), pq_ref[:, 0:half].astype(BF16))
    s2 = _dot_nt(keys_ref[1].astype(BF16), pq_ref[:, half:2 * half].astype(BF16))
    _top_rows(jnp.concatenate([s1, s2], axis=1), PEER_TOPK, v12, i12)
    a1 = v12[:, 0:tms]
    a2 = v12[:, tms:2 * tms]
    r16 = _iota((PEER_TOPK, tms), 0)
    r8 = _iota((8, tms), 0)
    parts = [a1[0:1] + a2]
    codes = [r16]
    for a in range(1, 8):
        parts.append(a1[a:a + 1] + a2[0:8])
        codes.append(r8 + PEER_TOPK * a)
    parts.append(a1[8:16] + a2[0:1])
    codes.append((r8 + 8) * PEER_TOPK)
    cand = jnp.concatenate(parts, axis=0)
    code = jnp.concatenate(codes, axis=0)
    for k in range(PEER_TOPK):
        m = jnp.max(cand, axis=0, keepdims=True)
        cs = jnp.min(jnp.where(cand == m, code, PEER_TOPK * PEER_TOPK), axis=0, keepdims=True)
        sc[k:k + 1, :] = m
        cd[k:k + 1, :] = cs
        cand = jnp.where(code == cs, LOWEST, cand)
    scv = sc[...]
    cdv = cd[...]
    ak = cdv >> 4
    bk = cdv & (PEER_TOPK - 1)
    idx1 = i12[:, 0:tms]
    idx2 = i12[:, tms:2 * tms]
    ik = jnp.zeros((PEER_TOPK, tms), jnp.int32)
    jk = jnp.zeros((PEER_TOPK, tms), jnp.int32)
    for a in range(PEER_TOPK):
        ik = jnp.where(ak == a, idx1[a:a + 1], ik)
        jk = jnp.where(bk == a, idx2[a:a + 1], jk)
    e = jnp.exp(scv - scv[0:1])
    i_ref[...] = ik.astype(F32)
    j_ref[...] = jk.astype(F32)
    g_ref[...] = e / jnp.sum(e, axis=0, keepdims=True)


def _peer_select(pq, keys, tms):
    n = pq.shape[0]
    out = pl.BlockSpec((PEER_TOPK, tms), lambda i, h: (h, i))
    shp = jax.ShapeDtypeStruct((PEER_HEADS * PEER_TOPK, n), F32)
    return pl.pallas_call(
        _peer_select_kernel,
        grid=(n // tms, PEER_HEADS),
        in_specs=[pl.BlockSpec((tms, PEER_QDIM), lambda i, h: (i, h)),
                  pl.BlockSpec((None, 2, PEER_KEYS, PEER_QDIM // 2), lambda i, h: (h, 0, 0, 0))],
        out_specs=[out, out, out],
        out_shape=[shp, shp, shp],
        scratch_shapes=[pltpu.VMEM((PEER_TOPK, 2 * tms), F32), pltpu.VMEM((PEER_TOPK, 2 * tms), jnp.int32),
                        pltpu.VMEM((PEER_TOPK, tms), F32), pltpu.VMEM((PEER_TOPK, tms), jnp.int32)],
        compiler_params=_params("parallel", "arbitrary"),
        name="peer_select",
    )(pq, keys)


W_PITCH = PEER_KEYS + 8
EXPERT_CHUNK = 256
EXPERT_GROUP = 512


def _peer_dense_kernel(xn_ref, ik_ref, jk_ref, gk_ref, u_ref, v_ref, x1_ref, nf_ref, out_ref, w_scr, *, tm, te):
    e_idx = pl.program_id(1)

    @pl.when(e_idx == 0)
    def _build():
        sub = _iota((PEER_KEYS, LANES), 0).astype(F32)

        def body(n, carry):
            irow = ik_ref[pl.ds(n, 1), :]
            jrow = jk_ref[pl.ds(n, 1), :]
            grow = gk_ref[pl.ds(n, 1), :]
            a = jnp.where(irow == sub, grow, 0.0).astype(BF16)
            bt = jnp.where(jrow == sub, 1.0, 0.0).astype(BF16)
            w_scr[pl.ds(pl.multiple_of(n * W_PITCH, 8), PEER_KEYS), :] = _dot_nt(a, bt)
            return carry

        lax.fori_loop(0, tm, body, 0, unroll=32)
        out_ref[...] = jnp.zeros(out_ref.shape, F32)

    def group(gi, carry):
        xn = xn_ref[...]
        acc = None
        for c in range(EXPERT_GROUP // EXPERT_CHUNK):
            r0 = pl.multiple_of(gi * EXPERT_GROUP + c * EXPERT_CHUNK, EXPERT_CHUNK)
            rows = pl.ds(r0, EXPERT_CHUNK)
            h = _dot_nt(xn, u_ref[rows, :])
            i0 = (e_idx * te + r0) // PEER_KEYS
            wt = jnp.concatenate(
                [w_scr[pl.ds(i0 + ii, tm, stride=W_PITCH), :] for ii in range(EXPERT_CHUNK // PEER_KEYS)], axis=1)
            d = _dot((jax.nn.gelu(h) * wt).astype(BF16), v_ref[rows, :])
            acc = d if acc is None else acc + d
        out_ref[...] += acc
        return carry

    lax.fori_loop(0, te // EXPERT_GROUP, group, 0)

    @pl.when(e_idx == pl.num_programs(1) - 1)
    def _finish():
        y = x1_ref[...] + out_ref[...]
        ms = jnp.mean(y * y, axis=-1, keepdims=True)
        out_ref[...] = (y * lax.rsqrt(ms + RMS_EPS)) * nf_ref[...]


def _peer_dense(xn, ik, jk, gk, u, v, x1, norm_final, tm, te):
    n = xn.shape[0]
    n_exp = u.shape[0]
    row = lambda w: pl.BlockSpec((tm, w), lambda i, e: (i, 0), pipeline_mode=pl.Buffered(1))
    exp_spec = pl.BlockSpec((te, D_MODEL), lambda i, e: (e, 0))
    return pl.pallas_call(
        functools.partial(_peer_dense_kernel, tm=tm, te=te),
        grid=(n // tm, n_exp // te),
        in_specs=[row(D_MODEL), row(LANES), row(LANES), row(LANES), exp_spec, exp_spec, row(D_MODEL),
                  pl.BlockSpec((1, D_MODEL), lambda i, e: (0, 0))],
        out_specs=pl.BlockSpec((tm, D_MODEL), lambda i, e: (i, 0)),
        out_shape=jax.ShapeDtypeStruct((n, D_MODEL), F32),
        scratch_shapes=[pltpu.VMEM((tm * W_PITCH, LANES), F32)],
        compiler_params=_params("parallel", "arbitrary"),
        name="peer_dense",
    )(xn, ik, jk, gk, u, v, x1, norm_final)


def _prep_w_in(w_in):
    w_main = jnp.concatenate([w_in[:, :COL_KV + 768], w_in[:, COL_KV + 768 + 3 * NSA_HEADS:]], axis=1)
    w_gate = jnp.pad(w_in[:, COL_KV + 768:COL_KV + 768 + 3 * NSA_HEADS], ((0, 0), (0, LANES - 3 * NSA_HEADS)))
    return jnp.concatenate([w_main, w_gate], axis=1).astype(BF16)


def _prep_cmp(cmp_wk, cmp_wv, cmp_pek, cmp_pev):
    def bd(w):
        z = jnp.zeros_like(w)
        return jnp.concatenate([jnp.concatenate([w, z], axis=2), jnp.concatenate([z, w], axis=2)], axis=1).astype(BF16)
    dup = lambda pe: jnp.concatenate([pe, pe], axis=1).astype(F32)
    return bd(cmp_wk), bd(cmp_wv), dup(cmp_pek), dup(cmp_pev)


def _ffn_tail(x, attn, ohg, w_out, norm_ffn, wq, keys, u, v, norm_final, tm, tms, tmd, te):
    x1, xn, pq = _outproj(x, attn, ohg, w_out, norm_ffn, wq, tm)
    ik, jk, gk = _peer_select(pq, keys, tms)
    return _peer_dense(xn, ik.T, jk.T, gk.T, u, v, x1, norm_final, tmd, te)


def kernel(x_prompt, x_sample, cache_cmp, cache_slc, cache_win, state_hgrn, page_table, norm_mix, w_in, cmp_wk, cmp_wv, cmp_pek, cmp_pev, hg_lb, hg_norm, w_out, norm_ffn, peer_wq, peer_keys, peer_u, peer_v, norm_final):
    b, t, d = x_prompt.shape
    db, tq, _ = x_sample.shape
    n_pool = cache_cmp.shape[1]
    n_pages = page_table.shape[1]
    past = n_pages * PAGE_ROWS
    wb = cache_win.shape[2]
    row = lambda a: a.reshape(1, -1)

    w_all = _prep_w_in(w_in[0])
    cw = _prep_cmp(cmp_wk[0], cmp_wv[0], cmp_pek[0], cmp_pev[0])
    w_out_b = w_out[0].astype(BF16)
    wq_b = peer_wq[0].astype(BF16)
    u_b = peer_u[0].astype(BF16)
    v_b = peer_v[0].astype(BF16)
    tail = lambda x, attn, ohg, tmd: _ffn_tail(x, attn, ohg, w_out_b, row(norm_ffn[0]), wq_b, peer_keys[0], u_b, v_b,
                                               row(norm_final), 512, 512, tmd, 1024)

    cos, sin = _rope_tables(jnp.arange(t, dtype=jnp.int32))
    xp = x_prompt.reshape(b * t, d)
    q, qr, cmp_p, cmp_t, slc_t, win_t, ksb, kwb, vst, vwt, gates, hp = _inproj_seq(
        xp, row(norm_mix[0]), w_all, cos, sin, b, t, 512)
    kcvc = _compress_prompt(cmp_p.reshape(b, t, 256), cw)
    ncp = t // CMP_STRIDE
    attn_p = _nsa_prompt(q.reshape(b, t, -1), qr.reshape(b, t, -1), gates.reshape(b, t, -1), kcvc,
                         ksb.reshape(b, t, LANES), vst.reshape(b, t // KEY_TILE, LANES, KEY_TILE),
                         kwb.reshape(b, t, LANES), vwt.reshape(b, t // LANES, LANES, LANES),
                         _overlap_matrix(ncp, ncp - 1).T, _block_expand_matrix(t), 128)
    ohg_p, s_p = _hgrn(hp, jnp.zeros((b, HG_HEADS, HG_DK, HG_DV), F32), hg_lb, row(hg_norm[0]), b, t, 4, HG_ROWS)
    y_p = tail(xp, attn_p.reshape(b * t, -1), ohg_p, 512)

    pos_s = past + (jnp.arange(db * tq, dtype=jnp.int32) % tq)
    cos_s, sin_s = _rope_tables(pos_s)
    xs = x_sample.reshape(db * tq, d)
    q_s, qr_s, cmp_s, slc_s, win_s, _, _, gates_s, hs = _inproj(xs, row(norm_mix[0]), w_all, cos_s, sin_s, 256)
    ncs = past // CMP_STRIDE
    kcvc_s = _compress_paged(_feature_major(cache_cmp, n_pool),
                             page_table.reshape(db // CMP_SEQS_PER_STEP, CMP_SEQS_PER_STEP * n_pages), cw)
    kcvc_s = kcvc_s.reshape(db, ncs, 256)
    attn_s, nwin_t = _nsa_sample(q_s, qr_s, gates_s, kcvc_s, _feature_major(cache_slc, n_pool), slc_s,
                                 _feature_major(cache_win, db), win_s, page_table,
                                 _overlap_matrix(ncs, ncs - 1), _block_expand_matrix(past + KEY_PAD), tq)
    ohg_s, s_s = _hgrn(hs, state_hgrn[0], hg_lb, row(hg_norm[0]), db, tq, 1, tq)
    y_s = tail(xs, attn_s, ohg_s, 512)

    kv5 = lambda a, nb, nt: a.reshape(1, nb, nt, 2, NSA_KV_HEADS, HEAD_DIM)
    kv5_t = lambda a_t, nb, nt: kv5(a_t.transpose(0, 2, 1), nb, nt)
    keep = min(WINDOW, t)
    return (y_p.reshape(b, t, d), y_s.reshape(db, tq, d),
            kv5_t(cmp_t, b, t), kv5_t(slc_t, b, t), kv5_t(win_t[:, :, t - keep:], b, keep), s_p[None],
            kv5(cmp_s, db, tq), kv5(slc_s, db, tq), kv5_t(nwin_t, db, wb), s_s[None])
```

```python
import functools

import numpy as np
import jax
import jax.numpy as jnp
from jax import lax
from jax.experimental import pallas as pl
from jax.experimental.pallas import tpu as pltpu

F32 = jnp.float32
BF16 = jnp.bfloat16

D_MODEL = 1024
HEAD_DIM = 64
NSA_HEADS = 8
NSA_KV_HEADS = 2
NSA_GROUP = NSA_HEADS // NSA_KV_HEADS
CMP_BLOCK = 32
CMP_STRIDE = 16
SLC_BLOCK = 64
SLC_TOPN = 16
WINDOW = 512
ROPE_THETA = 10000.0
HG_HEADS = 4
HG_DK = 128
HG_DV = 128
NSA_WIDTH = NSA_HEADS * HEAD_DIM
HG_WIDTH = HG_HEADS * HG_DV
KV_WIDTH = NSA_KV_HEADS * HEAD_DIM
PEER_HEADS = 8
PEER_KEYS = 128
PEER_QDIM = 256
PEER_TOPK = 16
RMS_EPS = 1e-6
NEG_INF = -1e30
FORCED_SCORE = 1e6
LOWEST = -3e38

LANES = 128
HALF = 64
VMEM_LIMIT = 56 * 1024 * 1024

COL_Q = 0
COL_KV = 512
COL_H = 1280
COL_G = 3328
PROJ_PAD = 3456


def _dot(a, b):
    return jnp.dot(a, b, preferred_element_type=F32)


def _dot_nt(a, b):
    return lax.dot_general(a, b, (((1,), (1,)), ((), ())), preferred_element_type=F32)


def _iota(shape, dim):
    return lax.broadcasted_iota(jnp.int32, shape, dim)


def _params(*sem):
    return pltpu.CompilerParams(dimension_semantics=sem, vmem_limit_bytes=VMEM_LIMIT)


def _rope_tile(x, cos, sin_signed, first_half):
    partner = jnp.where(first_half, pltpu.roll(x, LANES - 32, 1), pltpu.roll(x, 32, 1))
    return x * cos + partner * sin_signed


def _inproj_seq_kernel(x_ref, g_ref, w_ref, cos_ref, sin_ref,
                       q_ref, qr_ref, cmp_ref, cmpt_ref, slct_ref, wint_ref, ksb_ref, kwb_ref, vst_ref, vwt_ref,
                       gates_ref, h_ref):
    x = x_ref[...]
    ms = jnp.mean(x * x, axis=-1, keepdims=True)
    xn = (x * lax.rsqrt(ms + RMS_EPS)) * g_ref[...]
    proj = _dot(xn.astype(BF16), w_ref[...])
    cos = cos_ref[...]
    sin = sin_ref[...]
    first_half = (_iota(cos.shape, 1) & (HALF - 1)) < 32
    rope = lambda t: _rope_tile(t, cos, sin, first_half)
    q_ref[...] = proj[:, COL_Q:COL_Q + NSA_WIDTH]
    for c in range(NSA_WIDTH // LANES):
        qr_ref[:, c * LANES:(c + 1) * LANES] = rope(proj[:, COL_Q + c * LANES:COL_Q + (c + 1) * LANES])
    cmp = proj[:, COL_KV:COL_KV + 256]
    cmp_ref[...] = cmp
    cmpt_ref[...] = cmp.T
    ks = rope(proj[:, COL_KV + 256:COL_KV + 384])
    kw = rope(proj[:, COL_KV + 512:COL_KV + 640])
    vs_t = proj[:, COL_KV + 384:COL_KV + 512].T
    vw_t = proj[:, COL_KV + 640:COL_KV + 768].T
    slct_ref[0:LANES, :] = ks.T
    slct_ref[LANES:2 * LANES, :] = vs_t
    wint_ref[0:LANES, :] = kw.T
    wint_ref[LANES:2 * LANES, :] = vw_t
    ksb_ref[...] = ks.astype(BF16)
    kwb_ref[...] = kw.astype(BF16)
    for c in range(vst_ref.shape[0]):
        vst_ref[c] = vs_t[:, c * KEY_TILE:(c + 1) * KEY_TILE].astype(BF16)
    for c in range(vwt_ref.shape[0]):
        vwt_ref[c] = vw_t[:, c * LANES:(c + 1) * LANES].astype(BF16)
    gates_ref[...] = jax.nn.sigmoid(proj[:, COL_G:COL_G + LANES])
    h_ref[...] = proj[:, COL_H:COL_H + 4 * HG_WIDTH]


def _inproj_seq(x, norm_g, w_all, cos_t, sin_t, nb, t, tm):
    n = nb * t
    steps = t // tm
    row = lambda w: pl.BlockSpec((tm, w), lambda i: (i, 0))
    tab = pl.BlockSpec((tm, LANES), lambda i: (i % steps, 0))
    feat = pl.BlockSpec((None, 256, tm), lambda i: (i // steps, 0, i % steps))
    tiles = lambda w: pl.BlockSpec((tm // w, LANES, w), lambda i: (i, 0, 0))
    f32 = lambda *s: jax.ShapeDtypeStruct(s, F32)
    bf = lambda *s: jax.ShapeDtypeStruct(s, BF16)
    return pl.pallas_call(
        _inproj_seq_kernel,
        grid=(n // tm,),
        in_specs=[row(D_MODEL), pl.BlockSpec((1, D_MODEL), lambda i: (0, 0)),
                  pl.BlockSpec((D_MODEL, PROJ_PAD), lambda i: (0, 0)), tab, tab],
        out_specs=[row(NSA_WIDTH), row(NSA_WIDTH), row(256), feat, feat, feat, row(LANES), row(LANES),
                   tiles(KEY_TILE), tiles(LANES), row(LANES), row(4 * HG_WIDTH)],
        out_shape=[f32(n, NSA_WIDTH), f32(n, NSA_WIDTH), f32(n, 256), f32(nb, 256, t), f32(nb, 256, t),
                   f32(nb, 256, t), bf(n, LANES), bf(n, LANES), bf(n // KEY_TILE, LANES, KEY_TILE),
                   bf(n // LANES, LANES, LANES), f32(n, LANES), f32(n, 4 * HG_WIDTH)],
        compiler_params=_params("parallel"),
        name="inproj_seq",
    )(x, norm_g, w_all, cos_t, sin_t)


def _inproj_kernel(x_ref, g_ref, w_ref, cos_ref, sin_ref,
                   q_ref, qr_ref, cmp_ref, slc_ref, win_ref, gates_ref, h_ref):
    x = x_ref[...]
    ms = jnp.mean(x * x, axis=-1, keepdims=True)
    xn = (x * lax.rsqrt(ms + RMS_EPS)) * g_ref[...]
    proj = _dot(xn.astype(BF16), w_ref[...])
    cos = cos_ref[...]
    sin = sin_ref[...]
    first_half = (_iota(cos.shape, 1) & (HALF - 1)) < 32
    rope = lambda t: _rope_tile(t, cos, sin, first_half)
    q_ref[...] = proj[:, COL_Q:COL_Q + NSA_WIDTH]
    for c in range(NSA_WIDTH // LANES):
        qr_ref[:, c * LANES:(c + 1) * LANES] = rope(proj[:, COL_Q + c * LANES:COL_Q + (c + 1) * LANES])
    cmp_ref[...] = proj[:, COL_KV:COL_KV + 256]
    ks = rope(proj[:, COL_KV + 256:COL_KV + 384])
    vs = proj[:, COL_KV + 384:COL_KV + 512]
    kw = rope(proj[:, COL_KV + 512:COL_KV + 640])
    vw = proj[:, COL_KV + 640:COL_KV + 768]
    slc_ref[:, 0:LANES] = ks
    slc_ref[:, LANES:2 * LANES] = vs
    win_ref[:, 0:LANES] = kw
    win_ref[:, LANES:2 * LANES] = vw
    gates_ref[...] = jax.nn.sigmoid(proj[:, COL_G:COL_G + LANES])
    h_ref[...] = proj[:, COL_H:COL_H + 4 * HG_WIDTH]


def _inproj(x, norm_g, w_all, cos_t, sin_t, tm):
    n = x.shape[0]
    ntab = cos_t.shape[0] // tm
    row = lambda w: pl.BlockSpec((tm, w), lambda i: (i, 0))
    tab = pl.BlockSpec((tm, LANES), lambda i: (i % ntab, 0))
    out_shapes = [
        jax.ShapeDtypeStruct((n, NSA_WIDTH), F32), jax.ShapeDtypeStruct((n, NSA_WIDTH), F32),
        jax.ShapeDtypeStruct((n, 256), F32), jax.ShapeDtypeStruct((n, 256), F32),
        jax.ShapeDtypeStruct((n, 256), F32), jax.ShapeDtypeStruct((n, LANES), F32),
        jax.ShapeDtypeStruct((n, 4 * HG_WIDTH), F32),
    ]
    return pl.pallas_call(
        _inproj_kernel,
        grid=(n // tm,),
        in_specs=[row(D_MODEL), pl.BlockSpec((1, D_MODEL), lambda i: (0, 0)),
                  pl.BlockSpec((D_MODEL, PROJ_PAD), lambda i: (0, 0)), tab, tab],
        out_specs=[row(NSA_WIDTH), row(NSA_WIDTH), row(256), row(256), row(256), row(LANES), row(4 * HG_WIDTH)],
        out_shape=out_shapes,
        compiler_params=_params("parallel"),
        name="inproj",
    )(x, norm_g, w_all, cos_t, sin_t)


def _rope_tables(pos):
    half = HEAD_DIM // 2
    inv = ROPE_THETA ** (-jnp.arange(half, dtype=F32) / half)
    ang = pos.astype(F32)[:, None] * inv[None, :]
    cos = jnp.tile(jnp.cos(ang), (1, 4))
    sin = jnp.sin(ang)
    return cos, jnp.tile(jnp.concatenate([-sin, sin], axis=1), (1, 2))


def _compress_rows(rows_ref, n_out, w_ref, pe_ref, pitch=CMP_STRIDE):
    a = jnp.zeros((n_out, LANES), F32)
    b = jnp.zeros((n_out, LANES), F32)
    for j in range(CMP_STRIDE):
        xj = rows_ref[pl.ds(j, n_out, stride=pitch), :]
        a = a + _dot((xj + pe_ref[j:j + 1, :]).astype(BF16), w_ref[j])
        b = b + _dot((xj + pe_ref[CMP_STRIDE + j:CMP_STRIDE + j + 1, :]).astype(BF16), w_ref[CMP_STRIDE + j])
    return a + pltpu.roll(b, n_out - 1, 0)


def _compress_prompt_kernel(rk_ref, rv_ref, wk_ref, wv_ref, pek_ref, pev_ref, out_ref):
    n_out = out_ref.shape[0]
    out_ref[:, 0:LANES] = _compress_rows(rk_ref, n_out, wk_ref, pek_ref).astype(BF16)
    out_ref[:, LANES:2 * LANES] = _compress_rows(rv_ref, n_out, wv_ref, pev_ref).astype(BF16)


def _compress_prompt(rows, cw):
    b, t, _ = rows.shape
    n_out = t // CMP_STRIDE
    wspec = pl.BlockSpec((CMP_BLOCK, LANES, LANES), lambda i: (0, 0, 0))
    pspec = pl.BlockSpec((CMP_BLOCK, LANES), lambda i: (0, 0))
    return pl.pallas_call(
        _compress_prompt_kernel,
        grid=(b,),
        in_specs=[pl.BlockSpec((None, t, LANES), lambda i: (i, 0, 0)),
                  pl.BlockSpec((None, t, LANES), lambda i: (i, 0, 1)), wspec, wspec, pspec, pspec],
        out_specs=pl.BlockSpec((None, n_out, 256), lambda i: (i, 0, 0)),
        out_shape=jax.ShapeDtypeStruct((b, n_out, 256), BF16),
        compiler_params=_params("parallel"),
        name="compress_prompt",
    )(rows, rows, *cw)


def _stack_heads(ref, lo_half, nq):
    hi_half = jnp.logical_not(lo_half)
    parts = []
    for h in range(NSA_HEADS):
        g = h // NSA_GROUP
        tile = ref[:, LANES * (h // 2):LANES * (h // 2 + 1)]
        if h % 2 != g:
            tile = pltpu.roll(tile, HALF, 1)
        parts.append(jnp.where(lo_half if g == 0 else hi_half, tile, 0.0))
    return (jnp.concatenate(parts, axis=0) * (HEAD_DIM ** -0.5)).astype(BF16)


def _per_head(x, nq):
    return jnp.concatenate([x[0:nq]] * NSA_GROUP + [x[nq:2 * nq]] * NSA_GROUP, axis=0)


def _masked_softmax_rows(s, valid):
    sm = jnp.where(valid, s, NEG_INF)
    m = jnp.max(sm, axis=-1, keepdims=True)
    e = jnp.where(valid, jnp.exp(sm - m), 0.0)
    l = jnp.sum(e, axis=-1, keepdims=True)
    return e / jnp.where(l > 0.0, l, 1.0)


def _select_blocks(imp, pos_col):
    blk = _iota(imp.shape, 1)
    cur = pos_col >> 6
    forced = (blk == 0) | (blk == cur) | (blk == cur - 1)
    causal = (blk << 6) <= pos_col
    score = jnp.where(causal, jnp.where(forced, FORCED_SCORE, imp), -FORCED_SCORE)
    return score, causal


def _topk_mask_t(score_t, n_sel):
    bi = _iota(score_t.shape, 0)
    sel = jnp.zeros(score_t.shape, F32)
    sc = score_t
    for _ in range(n_sel):
        m = jnp.max(sc, axis=0, keepdims=True)
        idx = jnp.min(jnp.where(sc == m, bi, LANES), axis=0, keepdims=True)
        hit = bi == idx
        sel = jnp.where(hit, 1.0, sel)
        sc = jnp.where(hit, LOWEST, sc)
    return sel


def _cmp_branch(qn, kc, vc, ov, pos_rows, pos_grp, nq):
    s = _dot_nt(qn, kc)
    valid = (_iota(s.shape, 1) * CMP_STRIDE + (CMP_BLOCK - 1)) <= pos_rows
    p = _masked_softmax_rows(s, valid)
    o_c = _dot(p.astype(BF16), vc)
    slab = lambda h: p[h * nq:(h + 1) * nq]
    psum = jnp.concatenate([slab(0) + slab(1) + slab(2) + slab(3), slab(4) + slab(5) + slab(6) + slab(7)], axis=0)
    hi = psum.astype(BF16)
    lo = (psum - hi.astype(F32)).astype(BF16)
    imp = _dot(hi, ov) + _dot(lo, ov)
    score, causal = _select_blocks(imp, pos_grp)
    return o_c, score, causal


def _combine_heads(out_ref, gates, lo_half, o_c, o_s, o_w, nq):
    combs = []
    for h in range(NSA_HEADS):
        rows = slice(h * nq, (h + 1) * nq)
        comb = (gates[:, h:h + 1] * o_c[rows] + gates[:, NSA_HEADS + h:NSA_HEADS + h + 1] * o_s[rows]
                + gates[:, 2 * NSA_HEADS + h:2 * NSA_HEADS + h + 1] * o_w[rows])
        if h % 2 != h // NSA_GROUP:
            comb = pltpu.roll(comb, HALF, 1)
        combs.append(comb)
    for r in range(NSA_HEADS // 2):
        out_ref[:, LANES * r:LANES * (r + 1)] = jnp.where(lo_half, combs[2 * r], combs[2 * r + 1])


KEY_TILE = 512


def _masked_softmax_cols(s, valid):
    sm = jnp.where(valid, s, NEG_INF)
    m = jnp.max(sm, axis=0, keepdims=True)
    e = jnp.where(valid, jnp.exp(sm - m), 0.0)
    l = jnp.sum(e, axis=0, keepdims=True)
    return e / jnp.where(l > 0.0, l, 1.0)


def _nsa_prompt_kernel(q_ref, qr_ref, gates_ref, cmp_ref, vct_ref, ks_ref, vst_ref, kw_ref, vwt_ref, ovt_ref, et_ref,
                       out_ref, *, tq):
    t0 = pl.program_id(1) * tq
    cols = NSA_HEADS * tq
    grp = NSA_KV_HEADS * tq
    pos_cols = t0 + (_iota((1, cols), 1) & (tq - 1))
    pos_grp = t0 + (_iota((1, grp), 1) & (tq - 1))
    n_tiles = (t0 + tq + KEY_TILE - 1) // KEY_TILE
    w_chunk = jnp.maximum(t0 - WINDOW, 0) // LANES
    w_start = pl.multiple_of(w_chunk * LANES, LANES)
    w_len = WINDOW + tq
    per_head = lambda x: jnp.concatenate([x[:, 0:tq]] * NSA_GROUP + [x[:, tq:grp]] * NSA_GROUP, axis=1)

    lo_half = _iota((tq, LANES), 1) < HALF
    qn = _stack_heads(q_ref, lo_half, tq)
    qr = _stack_heads(qr_ref, lo_half, tq)

    all_heads = lambda x: jnp.concatenate([x] * NSA_HEADS, axis=1)
    pos_tok = t0 + _iota((1, tq), 1)

    sc = _dot_nt(cmp_ref[:, 0:LANES], qn)
    n_i = _iota((sc.shape[0], tq), 0)
    sc = sc + all_heads(jnp.where(n_i * CMP_STRIDE + (CMP_BLOCK - 1) <= pos_tok, 0.0, NEG_INF))
    ec = jnp.exp(sc - jnp.max(sc, axis=0, keepdims=True))
    norm_c = jnp.where(pos_cols >= CMP_BLOCK - 1, 1.0 / jnp.sum(ec, axis=0, keepdims=True), 0.0)
    o_c = _dot(vct_ref[...], ec.astype(BF16)) * norm_c
    pc = ec * norm_c
    slab = lambda h: pc[:, h * tq:(h + 1) * tq]
    psum = jnp.concatenate([slab(0) + slab(1) + slab(2) + slab(3), slab(4) + slab(5) + slab(6) + slab(7)], axis=1)
    hi = psum.astype(BF16)
    lo = (psum - hi.astype(F32)).astype(BF16)
    ovt = ovt_ref[...]
    imp = _dot(ovt, hi) + _dot(ovt, lo)
    blk = _iota(imp.shape, 0)
    cur = pos_grp >> 6
    forced = (blk == 0) | (blk == cur) | (blk == cur - 1)
    causal = (blk << 6) <= pos_grp
    score = jnp.where(causal, jnp.where(forced, FORCED_SCORE, imp), -FORCED_SCORE)
    sel = jnp.where(causal, _topk_mask_t(score, SLC_TOPN), 0.0)
    notsel = (1.0 - sel).astype(BF16)

    sw = _dot_nt(kw_ref[pl.ds(w_start, w_len), :], qr)
    kp = w_start + _iota((w_len, tq), 0)
    sw = sw + all_heads(jnp.where((kp <= pos_tok) & (kp >= pos_tok - WINDOW), 0.0, NEG_INF))
    ew = jnp.exp(sw - jnp.max(sw, axis=0, keepdims=True))
    vwt = jnp.concatenate([vwt_ref[w_chunk + c] for c in range(w_len // LANES)], axis=1)
    o_w = _dot(vwt, ew.astype(BF16)) / jnp.sum(ew, axis=0, keepdims=True)

    def tile_step(kt, carry, diag):
        m, l, acc = carry
        k0 = pl.multiple_of(kt * KEY_TILE, KEY_TILE)
        s = _dot_nt(ks_ref[pl.ds(k0, KEY_TILE), :], qr)
        s = s + per_head(_dot(et_ref[pl.ds(k0, KEY_TILE), :], notsel) * NEG_INF)
        if diag:
            s = s + all_heads(jnp.where(k0 + _iota((KEY_TILE, tq), 0) <= pos_tok, 0.0, NEG_INF))
        m_new = jnp.maximum(m, jnp.max(s, axis=0, keepdims=True))
        alpha = jnp.exp(m - m_new)
        p = jnp.exp(s - m_new)
        l = alpha * l + jnp.sum(p, axis=0, keepdims=True)
        acc = alpha * acc + _dot(vst_ref[kt], p.astype(BF16))
        return m_new, l, acc

    init = (jnp.full((1, cols), NEG_INF, F32), jnp.zeros((1, cols), F32), jnp.zeros((LANES, cols), F32))
    carry = lax.fori_loop(0, n_tiles - 1, lambda kt, c: tile_step(kt, c, False), init)
    _, l, acc = tile_step(n_tiles - 1, carry, True)
    o_s = acc / l

    gt = gates_ref[...].T
    for r in range(NSA_HEADS // 2):
        halves = []
        for h in (2 * r, 2 * r + 1):
            c = slice(h * tq, (h + 1) * tq)
            comb = (gt[h:h + 1] * o_c[:, c] + gt[NSA_HEADS + h:NSA_HEADS + h + 1] * o_s[:, c]
                    + gt[2 * NSA_HEADS + h:2 * NSA_HEADS + h + 1] * o_w[:, c])
            g = h // NSA_GROUP
            halves.append(comb[HALF * g:HALF * (g + 1)])
        out_ref[:, LANES * r:LANES * (r + 1)] = jnp.concatenate(halves, axis=0).T


def _nsa_prompt(q, qr, gates, kcvc, ksb, vst, kwb, vwt, ovt, et, tq):
    b, t, _ = q.shape
    ncp = kcvc.shape[1]
    vct = kcvc[:, :, LANES:].transpose(0, 2, 1)
    blk = lambda w: pl.BlockSpec((None, tq, w), lambda bi, i: (bi, i, 0))
    full = lambda r, w: pl.BlockSpec((None, r, w), lambda bi, i: (bi, 0, 0))
    full4 = lambda a: pl.BlockSpec((None,) + a.shape[1:], lambda bi, i: (bi, 0, 0, 0))
    return pl.pallas_call(
        functools.partial(_nsa_prompt_kernel, tq=tq),
        grid=(b, t // tq),
        in_specs=[blk(NSA_WIDTH), blk(NSA_WIDTH), blk(LANES), full(ncp, 256), full(LANES, ncp), full(t, LANES),
                  full4(vst), full(t, LANES), full4(vwt),
                  pl.BlockSpec((LANES, ncp), lambda bi, i: (0, 0)),
                  pl.BlockSpec((t, LANES), lambda bi, i: (0, 0))],
        out_specs=blk(NSA_WIDTH),
        out_shape=jax.ShapeDtypeStruct((b, t, NSA_WIDTH), F32),
        compiler_params=_params("parallel", "arbitrary"),
        name="nsa_prompt",
    )(q, qr, gates, kcvc, vct, ksb, vst, kwb, vwt, ovt, et)


def _overlap_matrix(n_cmp_pad, n_cmp):
    c0 = jnp.arange(n_cmp_pad, dtype=jnp.int32)[:, None] * CMP_STRIDE
    s0 = jnp.arange(LANES, dtype=jnp.int32)[None, :] * SLC_BLOCK
    real = jnp.arange(n_cmp_pad, dtype=jnp.int32)[:, None] < n_cmp
    return ((c0 < s0 + SLC_BLOCK) & (c0 + CMP_BLOCK > s0) & real).astype(BF16)


def _block_expand_matrix(n_keys):
    r = jnp.arange(n_keys, dtype=jnp.int32)[:, None] // SLC_BLOCK
    return (r == jnp.arange(LANES, dtype=jnp.int32)[None, :]).astype(BF16)


PAGE_ROWS = 128
CMP_SEQS_PER_STEP = 4
GROUP_PITCH = 24


def _compress_paged_kernel(pt_ref, *refs, n_pages):
    pages = refs[:n_pages]
    wk_ref, wv_ref, pek_ref, pev_ref, out_ref, rk_scr, rv_scr = refs[n_pages:]
    groups = PAGE_ROWS // CMP_STRIDE
    for p in range(n_pages):
        for scr, half in ((rk_scr, 0), (rv_scr, 1)):
            rows = pages[p][half * LANES:(half + 1) * LANES, :].T
            for m in range(groups):
                r0 = (p * groups + m) * GROUP_PITCH
                scr[r0:r0 + CMP_STRIDE, :] = rows[m * CMP_STRIDE:(m + 1) * CMP_STRIDE]
    n_out = out_ref.shape[0]
    out_ref[:, 0:LANES] = _compress_rows(rk_scr, n_out, wk_ref, pek_ref, GROUP_PITCH).astype(BF16)
    out_ref[:, LANES:2 * LANES] = _compress_rows(rv_scr, n_out, wv_ref, pev_ref, GROUP_PITCH).astype(BF16)


def _page_specs(n_pages):
    return [pl.BlockSpec((None, 256, PAGE_ROWS), functools.partial(lambda i, pt, p: (pt[i * n_pages + p], 0, 0), p=p))
            for p in range(n_pages)]


def _feature_major(cache, lead):
    return cache[0].reshape(lead, cache.shape[2], 256).transpose(0, 2, 1)


def _compress_paged(cache, page_table, cw):
    db, n_pages = page_table.shape
    past = n_pages * PAGE_ROWS
    n_out = past // CMP_STRIDE
    wspec = pl.BlockSpec((CMP_BLOCK, LANES, LANES), lambda i, pt: (0, 0, 0))
    pspec = pl.BlockSpec((CMP_BLOCK, LANES), lambda i, pt: (0, 0))
    grid_spec = pltpu.PrefetchScalarGridSpec(
        num_scalar_prefetch=1, grid=(db,),
        in_specs=_page_specs(n_pages) + [wspec, wspec, pspec, pspec],
        out_specs=pl.BlockSpec((None, n_out, 256), lambda i, pt: (i, 0, 0)),
        scratch_shapes=[pltpu.VMEM((n_out * GROUP_PITCH, LANES), F32), pltpu.VMEM((n_out * GROUP_PITCH, LANES), F32)])
    return pl.pallas_call(
        functools.partial(_compress_paged_kernel, n_pages=n_pages),
        grid_spec=grid_spec,
        out_shape=jax.ShapeDtypeStruct((db, n_out, 256), BF16),
        compiler_params=_params("arbitrary"),
        name="compress_paged",
    )(page_table.reshape(-1), *([cache] * n_pages), *cw)


KEY_PAD = LANES
NSA_SEQS_PER_STEP = 2


def _nsa_sample_kernel(pt_ref, q_ref, qr_ref, gates_ref, cmp_ref, *refs, n_pages):
    n_seq = cmp_ref.shape[0]
    pages = refs[:n_seq * n_pages]
    snew_ref, cwin_ref, wnew_ref, ov_ref, et_ref, out_ref, nwin_ref = refs[n_seq * n_pages:]
    nq = q_ref.shape[0] // n_seq
    past = n_pages * PAGE_ROWS
    wb = cwin_ref.shape[2]
    rows = NSA_HEADS * nq
    n_grp = NSA_KV_HEADS * nq
    pad_rows = lambda x: jnp.concatenate([x, jnp.zeros((KEY_PAD - nq, x.shape[1]), F32)], axis=0)
    lo_half = _iota((nq, LANES), 1) < HALF
    pos_rows = past + (_iota((rows, 1), 0) & (nq - 1))
    pos_grp = past + (_iota((n_grp, 1), 0) & (nq - 1))
    seq_rows = lambda ref, b: ref.at[b * nq:(b + 1) * nq, :]

    qrs, o_cs, scores = [], [], []
    for b in range(n_seq):
        qn = _stack_heads(seq_rows(q_ref, b), lo_half, nq)
        qrs.append(_stack_heads(seq_rows(qr_ref, b), lo_half, nq))
        o_c, score, causal = _cmp_branch(qn, cmp_ref[b, :, 0:LANES], cmp_ref[b, :, LANES:2 * LANES], ov_ref[...],
                                         pos_rows, pos_grp, nq)
        o_cs.append(o_c)
        scores.append(score)
    score_sq = jnp.concatenate(scores + [jnp.full((LANES - n_seq * n_grp, LANES), LOWEST, F32)], axis=0)
    sel_all = _topk_mask_t(score_sq.T, SLC_TOPN).T

    for b in range(n_seq):
        pg = pages[b * n_pages:(b + 1) * n_pages]
        qr = qrs[b]
        snew = pad_rows(snew_ref[b * nq:(b + 1) * nq, :])
        wnew = pad_rows(wnew_ref[b * nq:(b + 1) * nq, :])

        shifted = pltpu.roll(cwin_ref[b], wb - nq, 1)
        new_t = jnp.concatenate([wnew[:, 0:LANES].T, wnew[:, LANES:2 * LANES].T], axis=0)
        tail = jnp.where(_iota((256, LANES), 1) >= LANES - nq, pltpu.roll(new_t, LANES - nq, 1),
                         shifted[:, wb - LANES:wb])
        nwin_ref[b, :, 0:wb - LANES] = shifted[:, 0:wb - LANES]
        nwin_ref[b, :, wb - LANES:wb] = tail

        sel = jnp.where(causal, sel_all[b * n_grp:(b + 1) * n_grp], 0.0)
        notsel = (1.0 - sel).astype(BF16)
        s = jnp.concatenate([_dot(qr, pg[p][0:LANES, :].astype(BF16)) for p in range(n_pages)]
                            + [_dot_nt(qr, snew[:, 0:LANES].astype(BF16))], axis=1)
        s = s + _per_head(_dot_nt(notsel, et_ref[...]) * NEG_INF, nq)
        s = jnp.where(_iota(s.shape, 1) <= pos_rows, s, NEG_INF)
        e = jnp.exp(s - jnp.max(s, axis=-1, keepdims=True))
        eb = e.astype(BF16)
        o_s = _dot(eb[:, past:past + KEY_PAD], snew[:, LANES:2 * LANES].astype(BF16))
        for p in range(n_pages):
            o_s = o_s + _dot_nt(eb[:, p * PAGE_ROWS:(p + 1) * PAGE_ROWS], pg[p][LANES:2 * LANES, :].astype(BF16))
        o_s = o_s / jnp.sum(e, axis=-1, keepdims=True)

        sw = jnp.concatenate([_dot(qr, cwin_ref[b, 0:LANES, :].astype(BF16)),
                              _dot_nt(qr, wnew[:, 0:LANES].astype(BF16))], axis=1)
        kp = (past - wb) + _iota(sw.shape, 1)
        pw = _masked_softmax_rows(sw, (kp <= pos_rows) & (kp >= pos_rows - WINDOW) & (kp >= 0)).astype(BF16)
        o_w = (_dot_nt(pw[:, 0:wb], cwin_ref[b, LANES:2 * LANES, :].astype(BF16))
               + _dot(pw[:, wb:wb + KEY_PAD], wnew[:, LANES:2 * LANES].astype(BF16)))
        _combine_heads(seq_rows(out_ref, b), seq_rows(gates_ref, b)[...], lo_half, o_cs[b], o_s, o_w, nq)


def _nsa_sample(q, qr, gates, kcvc, cache_slc, slc_new, cache_win, win_new, page_table, ov, et, nq):
    db, n_pages = page_table.shape
    past = n_pages * PAGE_ROWS
    wb = cache_win.shape[2]
    ncp = kcvc.shape[1]
    n_seq = NSA_SEQS_PER_STEP
    blk = lambda w: pl.BlockSpec((n_seq * nq, w), lambda i, pt: (i, 0))
    win_spec = pl.BlockSpec((n_seq, 256, wb), lambda i, pt: (i, 0, 0))
    grid_spec = pltpu.PrefetchScalarGridSpec(
        num_scalar_prefetch=1, grid=(db // n_seq,),
        in_specs=[blk(NSA_WIDTH), blk(NSA_WIDTH), blk(LANES),
                  pl.BlockSpec((n_seq, ncp, 256), lambda i, pt: (i, 0, 0))] + _page_specs(n_seq * n_pages) + [
                  blk(256), win_spec, blk(256),
                  pl.BlockSpec((ncp, LANES), lambda i, pt: (0, 0)),
                  pl.BlockSpec((past + KEY_PAD, LANES), lambda i, pt: (0, 0))],
        out_specs=[blk(NSA_WIDTH), win_spec])
    return pl.pallas_call(
        functools.partial(_nsa_sample_kernel, n_pages=n_pages),
        grid_spec=grid_spec,
        out_shape=[jax.ShapeDtypeStruct((db * nq, NSA_WIDTH), F32), jax.ShapeDtypeStruct((db, 256, wb), F32)],
        compiler_params=_params("arbitrary"),
        name="nsa_sample",
    )(page_table.reshape(-1), q, qr, gates, kcvc, *([cache_slc] * (n_seq * n_pages)), slc_new, cache_win, win_new, ov, et)


HG_ROWS = 128
HG_SUB = 16


def _dot_split3(m_bf16, x):
    a = x.astype(BF16)
    r = x - a.astype(F32)
    b = r.astype(BF16)
    c = (r - b.astype(F32)).astype(BF16)
    return _dot(m_bf16, a) + _dot(m_bf16, b) + _dot(m_bf16, c)


def _hgrn_chunk(q, fpre, v, gpre, lb, ng, s0, n_real):
    f = lb + (1.0 - lb) * jax.nn.sigmoid(fpre)
    logf = jnp.log(f)
    kk = 1.0 - f
    if n_real < HG_ROWS:
        pad = lambda a: jnp.concatenate([a, jnp.zeros((HG_ROWS - n_real, LANES), F32)], axis=0)
        q, logf, kk, v = pad(q), pad(logf), pad(kk), pad(v)
    r_i = _iota((HG_ROWS, HG_ROWS), 0)
    c_i = _iota((HG_ROWS, HG_ROWS), 1)
    tri_b = r_i >= c_i
    tri = jnp.where(tri_b, 1.0, 0.0).astype(BF16)
    cum = _dot_split3(tri, logf)
    cprev = cum - logf
    cum_last = cum[HG_ROWS - 1:HG_ROWS, :]
    s0b = s0.astype(BF16)
    o = _dot((q * jnp.exp(cum)).astype(BF16), s0b)
    row = _iota((HG_ROWS, LANES), 0)
    a_rows = []
    n_sub = -(-n_real // HG_SUB)
    for i in range(n_sub):
        r0 = i * HG_SUB
        c_ref = cprev[r0:r0 + 1, :]
        qt = q[r0:r0 + HG_SUB] * jnp.exp(cum[r0:r0 + HG_SUB] - c_ref)
        kt = kk * jnp.exp(jnp.where(row < r0 + HG_SUB, c_ref - cum, NEG_INF))
        a_rows.append(_dot_nt(qt.astype(BF16), kt.astype(BF16)))
    if n_sub * HG_SUB < HG_ROWS:
        a_rows.append(jnp.zeros((HG_ROWS - n_sub * HG_SUB, HG_ROWS), F32))
    a = jnp.where(tri_b, jnp.concatenate(a_rows, axis=0), 0.0)
    vb = v.astype(BF16)
    o = o + _dot(a.astype(BF16), vb)
    kdec = kk * jnp.exp(cum_last - cum)
    scale = jnp.broadcast_to(jnp.exp(cum_last), (HG_ROWS, LANES)).T
    s_new = scale * s0 + _dot(kdec.T.astype(BF16), vb)
    o = o[0:n_real]
    o = o * lax.rsqrt(jnp.mean(o * o, axis=-1, keepdims=True) + RMS_EPS) * ng
    return o * jax.nn.sigmoid(gpre), s_new


def _hgrn_kernel(q_ref, f_ref, v_ref, g_ref, s0_ref, lb_ref, ng_ref, o_ref, sfin_ref, s_scr, *, n_chunks, n_real):
    @pl.when(pl.program_id(1) == 0)
    def _init():
        s_scr[...] = s0_ref[...]

    lbp = lb_ref[...]
    e = jnp.exp(lbp - jnp.max(lbp, axis=0, keepdims=True))
    lb = e[0:1, :] / jnp.sum(e, axis=0, keepdims=True)
    ng = ng_ref[...]

    def chunk(c, carry):
        rows = slice(0, n_real) if n_chunks == 1 else pl.ds(pl.multiple_of(c * n_real, n_real), n_real)
        for hd in range(HG_HEADS):
            cols = slice(hd * LANES, (hd + 1) * LANES)
            o, s_new = _hgrn_chunk(q_ref[rows, cols], f_ref[rows, cols], v_ref[rows, cols], g_ref[rows, cols],
                                   lb[:, cols], ng[:, cols], s_scr[hd], n_real)
            o_ref[rows, cols] = o
            s_scr[hd] = s_new
        return carry

    if n_chunks == 1:
        chunk(0, 0)
    else:
        lax.fori_loop(0, n_chunks, chunk, 0)
    sfin_ref[...] = s_scr[...]


def _hgrn(h, s0, hg_lb, hg_norm, nb, t, n_chunks, n_real):
    rows = n_chunks * n_real
    steps = t // rows
    col = lambda sec: pl.BlockSpec((rows, HG_WIDTH), lambda b, j: (b * steps + j, sec))
    st = pl.BlockSpec((None, HG_HEADS, HG_DK, HG_DV), lambda b, j: (b, 0, 0, 0))
    return pl.pallas_call(
        functools.partial(_hgrn_kernel, n_chunks=n_chunks, n_real=n_real),
        grid=(nb, steps),
        in_specs=[col(0), col(1), col(2), col(3), st,
                  pl.BlockSpec(hg_lb.shape, lambda b, j: (0, 0)),
                  pl.BlockSpec((1, HG_WIDTH), lambda b, j: (0, 0))],
        out_specs=[pl.BlockSpec((rows, HG_WIDTH), lambda b, j: (b * steps + j, 0)), st],
        out_shape=[jax.ShapeDtypeStruct((nb * t, HG_WIDTH), F32),
                   jax.ShapeDtypeStruct((nb, HG_HEADS, HG_DK, HG_DV), F32)],
        scratch_shapes=[pltpu.VMEM((HG_HEADS, HG_DK, HG_DV), F32)],
        compiler_params=_params("parallel", "arbitrary"),
        name="hgrn",
    )(h, h, h, h, s0, hg_lb, hg_norm)


def _outproj_kernel(x_ref, a_ref, hg_ref, wo_ref, nf_ref, wq_ref, x1_ref, xn_ref, pq_ref):
    mix = jnp.concatenate([a_ref[...], hg_ref[...]], axis=1).astype(BF16)
    x1 = x_ref[...] + _dot(mix, wo_ref[...])
    x1_ref[...] = x1
    ms = jnp.mean(x1 * x1, axis=-1, keepdims=True)
    xb = ((x1 * lax.rsqrt(ms + RMS_EPS)) * nf_ref[...]).astype(BF16)
    xn_ref[...] = xb
    pq_ref[...] = _dot(xb, wq_ref[...])


def _outproj(x, attn, ohg, w_out, norm_ffn, wq, tm):
    n = x.shape[0]
    row = lambda w: pl.BlockSpec((tm, w), lambda i: (i, 0))
    full = lambda a: pl.BlockSpec(a.shape, lambda i: (0, 0))
    nq = wq.shape[1]
    return pl.pallas_call(
        _outproj_kernel,
        grid=(n // tm,),
        in_specs=[row(D_MODEL), row(NSA_WIDTH), row(HG_WIDTH), full(w_out), full(norm_ffn), full(wq)],
        out_specs=[row(D_MODEL), row(D_MODEL), row(nq)],
        out_shape=[jax.ShapeDtypeStruct((n, D_MODEL), F32), jax.ShapeDtypeStruct((n, D_MODEL), BF16),
                   jax.ShapeDtypeStruct((n, nq), F32)],
        compiler_params=_params("parallel"),
        name="outproj",
    )(x, attn, ohg, w_out, norm_ffn, wq)


def _top_rows(s, k, val_scr, idx_scr):
    bi = _iota(s.shape, 0)
    big = s.shape[0]
    for a in range(k):
        m = jnp.max(s, axis=0, keepdims=True)
        idx = jnp.min(jnp.where(s == m, bi, big), axis=0, keepdims=True)
        val_scr[a:a + 1, :] = m
        idx_scr[a:a + 1, :] = idx
        s = jnp.where(bi == idx, LOWEST, s)


def _peer_select_kernel(pq_ref, keys_ref, i_ref, j_ref, g_ref, v12, i12, sc, cd):
    tms = pq_ref.shape[0]
    half = PEER_QDIM // 2
    s1 = _dot_nt(keys_ref[0].astype(BF16), pq_ref[:, 0:half].astype(BF16))
    s2 = _dot_nt(keys_ref[1].astype(BF16), pq_ref[:, half:2 * half].astype(BF16))
    _top_rows(jnp.concatenate([s1, s2], axis=1), PEER_TOPK, v12, i12)
    a1 = v12[:, 0:tms]
    a2 = v12[:, tms:2 * tms]
    r16 = _iota((PEER_TOPK, tms), 0)
    r8 = _iota((8, tms), 0)
    parts = [a1[0:1] + a2]
    codes = [r16]
    for a in range(1, 8):
        parts.append(a1[a:a + 1] + a2[0:8])
        codes.append(r8 + PEER_TOPK * a)
    parts.append(a1[8:16] + a2[0:1])
    codes.append((r8 + 8) * PEER_TOPK)
    cand = jnp.concatenate(parts, axis=0)
    code = jnp.concatenate(codes, axis=0)
    for k in range(PEER_TOPK):
        m = jnp.max(cand, axis=0, keepdims=True)
        cs = jnp.min(jnp.where(cand == m, code, PEER_TOPK * PEER_TOPK), axis=0, keepdims=True)
        sc[k:k + 1, :] = m
        cd[k:k + 1, :] = cs
        cand = jnp.where(code == cs, LOWEST, cand)
    scv = sc[...]
    cdv = cd[...]
    ak = cdv >> 4
    bk = cdv & (PEER_TOPK - 1)
    idx1 = i12[:, 0:tms]
    idx2 = i12[:, tms:2 * tms]
    ik = jnp.zeros((PEER_TOPK, tms), jnp.int32)
    jk = jnp.zeros((PEER_TOPK, tms), jnp.int32)
    for a in range(PEER_TOPK):
        ik = jnp.where(ak == a, idx1[a:a + 1], ik)
        jk = jnp.where(bk == a, idx2[a:a + 1], jk)
    e = jnp.exp(scv - scv[0:1])
    i_ref[...] = ik.astype(F32)
    j_ref[...] = jk.astype(F32)
    g_ref[...] = e / jnp.sum(e, axis=0, keepdims=True)


def _peer_select(pq, keys, tms):
    n = pq.shape[0]
    out = pl.BlockSpec((PEER_TOPK, tms), lambda i, h: (h, i))
    shp = jax.ShapeDtypeStruct((PEER_HEADS * PEER_TOPK, n), F32)
    return pl.pallas_call(
        _peer_select_kernel,
        grid=(n // tms, PEER_HEADS),
        in_specs=[pl.BlockSpec((tms, PEER_QDIM), lambda i, h: (i, h)),
                  pl.BlockSpec((None, 2, PEER_KEYS, PEER_QDIM // 2), lambda i, h: (h, 0, 0, 0))],
        out_specs=[out, out, out],
        out_shape=[shp, shp, shp],
        scratch_shapes=[pltpu.VMEM((PEER_TOPK, 2 * tms), F32), pltpu.VMEM((PEER_TOPK, 2 * tms), jnp.int32),
                        pltpu.VMEM((PEER_TOPK, tms), F32), pltpu.VMEM((PEER_TOPK, tms), jnp.int32)],
        compiler_params=_params("parallel", "arbitrary"),
        name="peer_select",
    )(pq, keys)


W_PITCH = PEER_KEYS + 8
EXPERT_CHUNK = 256
EXPERT_GROUP = 512


def _peer_dense_kernel(xn_ref, ik_ref, jk_ref, gk_ref, u_ref, v_ref, x1_ref, nf_ref, out_ref, w_scr, *, tm, te):
    e_idx = pl.program_id(1)

    @pl.when(e_idx == 0)
    def _build():
        sub = _iota((PEER_KEYS, LANES), 0).astype(F32)

        def body(n, carry):
            irow = ik_ref[pl.ds(n, 1), :]
            jrow = jk_ref[pl.ds(n, 1), :]
            grow = gk_ref[pl.ds(n, 1), :]
            a = jnp.where(irow == sub, grow, 0.0).astype(BF16)
            bt = jnp.where(jrow == sub, 1.0, 0.0).astype(BF16)
            w_scr[pl.ds(pl.multiple_of(n * W_PITCH, 8), PEER_KEYS), :] = _dot_nt(a, bt)
            return carry

        lax.fori_loop(0, tm, body, 0, unroll=32)
        out_ref[...] = jnp.zeros(out_ref.shape, F32)

    def group(gi, carry):
        xn = xn_ref[...]
        acc = None
        for c in range(EXPERT_GROUP // EXPERT_CHUNK):
            r0 = pl.multiple_of(gi * EXPERT_GROUP + c * EXPERT_CHUNK, EXPERT_CHUNK)
            rows = pl.ds(r0, EXPERT_CHUNK)
            h = _dot_nt(xn, u_ref[rows, :])
            i0 = (e_idx * te + r0) // PEER_KEYS
            wt = jnp.concatenate(
                [w_scr[pl.ds(i0 + ii, tm, stride=W_PITCH), :] for ii in range(EXPERT_CHUNK // PEER_KEYS)], axis=1)
            d = _dot((jax.nn.gelu(h) * wt).astype(BF16), v_ref[rows, :])
            acc = d if acc is None else acc + d
        out_ref[...] += acc
        return carry

    lax.fori_loop(0, te // EXPERT_GROUP, group, 0)

    @pl.when(e_idx == pl.num_programs(1) - 1)
    def _finish():
        y = x1_ref[...] + out_ref[...]
        ms = jnp.mean(y * y, axis=-1, keepdims=True)
        out_ref[...] = (y * lax.rsqrt(ms + RMS_EPS)) * nf_ref[...]


def _peer_dense(xn, ik, jk, gk, u, v, x1, norm_final, tm, te):
    n = xn.shape[0]
    n_exp = u.shape[0]
    row = lambda w: pl.BlockSpec((tm, w), lambda i, e: (i, 0), pipeline_mode=pl.Buffered(1))
    exp_spec = pl.BlockSpec((te, D_MODEL), lambda i, e: (e, 0))
    return pl.pallas_call(
        functools.partial(_peer_dense_kernel, tm=tm, te=te),
        grid=(n // tm, n_exp // te),
        in_specs=[row(D_MODEL), row(LANES), row(LANES), row(LANES), exp_spec, exp_spec, row(D_MODEL),
                  pl.BlockSpec((1, D_MODEL), lambda i, e: (0, 0))],
        out_specs=pl.BlockSpec((tm, D_MODEL), lambda i, e: (i, 0)),
        out_shape=jax.ShapeDtypeStruct((n, D_MODEL), F32),
        scratch_shapes=[pltpu.VMEM((tm * W_PITCH, LANES), F32)],
        compiler_params=_params("parallel", "arbitrary"),
        name="peer_dense",
    )(xn, ik, jk, gk, u, v, x1, norm_final)


def _prep_w_in(w_in):
    w_main = jnp.concatenate([w_in[:, :COL_KV + 768], w_in[:, COL_KV + 768 + 3 * NSA_HEADS:]], axis=1)
    w_gate = jnp.pad(w_in[:, COL_KV + 768:COL_KV + 768 + 3 * NSA_HEADS], ((0, 0), (0, LANES - 3 * NSA_HEADS)))
    return jnp.concatenate([w_main, w_gate], axis=1).astype(BF16)


def _prep_cmp(cmp_wk, cmp_wv, cmp_pek, cmp_pev):
    def bd(w):
        z = jnp.zeros_like(w)
        return jnp.concatenate([jnp.concatenate([w, z], axis=2), jnp.concatenate([z, w], axis=2)], axis=1).astype(BF16)
    dup = lambda pe: jnp.concatenate([pe, pe], axis=1).astype(F32)
    return bd(cmp_wk), bd(cmp_wv), dup(cmp_pek), dup(cmp_pev)


ROW_TILE = 512
SAMPLE_ROW_TILE = 256
NSA_Q_BLOCK = 128
HG_CHUNKS_PER_STEP = 4
SELECT_TOKENS = 512
PEER_TOKEN_TILE = 512
PEER_EXPERT_TILE = 1024


def _ffn_tail(x, attn, ohg, w_out, norm_ffn, wq, keys, u, v, norm_final):
    x1, xn, pq = _outproj(x, attn, ohg, w_out, norm_ffn, wq, ROW_TILE)
    ik, jk, gk = _peer_select(pq, keys, SELECT_TOKENS)
    return _peer_dense(xn, ik.T, jk.T, gk.T, u, v, x1, norm_final, PEER_TOKEN_TILE, PEER_EXPERT_TILE)


def kernel(x_prompt, x_sample, cache_cmp, cache_slc, cache_win, state_hgrn, page_table, norm_mix, w_in, cmp_wk, cmp_wv, cmp_pek, cmp_pev, hg_lb, hg_norm, w_out, norm_ffn, peer_wq, peer_keys, peer_u, peer_v, norm_final):
    b, t, d = x_prompt.shape
    db, tq, _ = x_sample.shape
    n_pool = cache_cmp.shape[1]
    n_pages = page_table.shape[1]
    past = n_pages * PAGE_ROWS
    wb = cache_win.shape[2]
    row = lambda a: a.reshape(1, -1)

    w_all = _prep_w_in(w_in[0])
    cw = _prep_cmp(cmp_wk[0], cmp_wv[0], cmp_pek[0], cmp_pev[0])
    w_out_b = w_out[0].astype(BF16)
    wq_b = peer_wq[0].astype(BF16)
    u_b = peer_u[0].astype(BF16)
    v_b = peer_v[0].astype(BF16)
    tail = lambda x, attn, ohg: _ffn_tail(x, attn, ohg, w_out_b, row(norm_ffn[0]), wq_b, peer_keys[0], u_b, v_b,
                                          row(norm_final))

    cos, sin = _rope_tables(jnp.arange(t, dtype=jnp.int32))
    xp = x_prompt.reshape(b * t, d)
    q, qr, cmp_p, cmp_t, slc_t, win_t, ksb, kwb, vst, vwt, gates, hp = _inproj_seq(
        xp, row(norm_mix[0]), w_all, cos, sin, b, t, ROW_TILE)
    kcvc = _compress_prompt(cmp_p.reshape(b, t, 256), cw)
    ncp = t // CMP_STRIDE
    attn_p = _nsa_prompt(q.reshape(b, t, -1), qr.reshape(b, t, -1), gates.reshape(b, t, -1), kcvc,
                         ksb.reshape(b, t, LANES), vst.reshape(b, t // KEY_TILE, LANES, KEY_TILE),
                         kwb.reshape(b, t, LANES), vwt.reshape(b, t // LANES, LANES, LANES),
                         _overlap_matrix(ncp, ncp - 1).T, _block_expand_matrix(t), NSA_Q_BLOCK)
    ohg_p, s_p = _hgrn(hp, jnp.zeros((b, HG_HEADS, HG_DK, HG_DV), F32), hg_lb, row(hg_norm[0]), b, t,
                       HG_CHUNKS_PER_STEP, HG_ROWS)
    y_p = tail(xp, attn_p.reshape(b * t, -1), ohg_p)

    pos_s = past + (jnp.arange(db * tq, dtype=jnp.int32) % tq)
    cos_s, sin_s = _rope_tables(pos_s)
    xs = x_sample.reshape(db * tq, d)
    q_s, qr_s, cmp_s, slc_s, win_s, gates_s, hs = _inproj(xs, row(norm_mix[0]), w_all, cos_s, sin_s, SAMPLE_ROW_TILE)
    ncs = past // CMP_STRIDE
    kcvc_s = _compress_paged(_feature_major(cache_cmp, n_pool),
                             page_table.reshape(db // CMP_SEQS_PER_STEP, CMP_SEQS_PER_STEP * n_pages), cw)
    kcvc_s = kcvc_s.reshape(db, ncs, 256)
    attn_s, nwin_t = _nsa_sample(q_s, qr_s, gates_s, kcvc_s, _feature_major(cache_slc, n_pool), slc_s,
                                 _feature_major(cache_win, db), win_s, page_table,
                                 _overlap_matrix(ncs, ncs - 1), _block_expand_matrix(past + KEY_PAD), tq)
    ohg_s, s_s = _hgrn(hs, state_hgrn[0], hg_lb, row(hg_norm[0]), db, tq, 1, tq)
    y_s = tail(xs, attn_s, ohg_s)

    kv5 = lambda a, nb, nt: a.reshape(1, nb, nt, 2, NSA_KV_HEADS, HEAD_DIM)
    kv5_t = lambda a_t, nb, nt: kv5(a_t.transpose(0, 2, 1), nb, nt)
    keep = min(WINDOW, t)
    return (y_p.reshape(b, t, d), y_s.reshape(db, tq, d),
            kv5_t(cmp_t, b, t), kv5_t(slc_t, b, t), kv5_t(win_t[:, :, t - keep:], b, keep), s_p[None],
            kv5(cmp_s, db, tq), kv5(slc_s, db, tq), kv5_t(nwin_t, db, wb), s_s[None])
```

```python
import functools

import numpy as np
import jax
import jax.numpy as jnp
from jax import lax
from jax.experimental import pallas as pl
from jax.experimental.pallas import tpu as pltpu

F32 = jnp.float32
BF16 = jnp.bfloat16

D_MODEL = 1024
HEAD_DIM = 64
NSA_HEADS = 8
NSA_KV_HEADS = 2
NSA_GROUP = NSA_HEADS // NSA_KV_HEADS
CMP_BLOCK = 32
CMP_STRIDE = 16
SLC_BLOCK = 64
SLC_TOPN = 16
WINDOW = 512
ROPE_THETA = 10000.0
HG_HEADS = 4
HG_DK = 128
HG_DV = 128
NSA_WIDTH = NSA_HEADS * HEAD_DIM
HG_WIDTH = HG_HEADS * HG_DV
KV_WIDTH = NSA_KV_HEADS * HEAD_DIM
PEER_HEADS = 8
PEER_KEYS = 128
PEER_QDIM = 256
PEER_TOPK = 16
RMS_EPS = 1e-6
NEG_INF = -1e30
FORCED_SCORE = 1e6
LOWEST = -3e38

LANES = 128
HALF = 64
VMEM_LIMIT = 56 * 1024 * 1024

COL_Q = 0
COL_KV = 512
COL_H = 1280
COL_G = 3328
PROJ_PAD = 3456


def _dot(a, b):
    return jnp.dot(a, b, preferred_element_type=F32)


def _dot_nt(a, b):
    return lax.dot_general(a, b, (((1,), (1,)), ((), ())), preferred_element_type=F32)


def _iota(shape, dim):
    return lax.broadcasted_iota(jnp.int32, shape, dim)


def _params(*sem):
    return pltpu.CompilerParams(dimension_semantics=sem, vmem_limit_bytes=VMEM_LIMIT)


def _rope_tile(x, cos, sin_signed, first_half):
    partner = jnp.where(first_half, pltpu.roll(x, LANES - 32, 1), pltpu.roll(x, 32, 1))
    return x * cos + partner * sin_signed


def _inproj_seq_kernel(x_ref, g_ref, w_ref, cos_ref, sin_ref,
                       q_ref, qr_ref, cmp_ref, cmpt_ref, slct_ref, wint_ref, ksb_ref, kwb_ref, vst_ref, vwt_ref,
                       gates_ref, h_ref):
    x = x_ref[...]
    ms = jnp.mean(x * x, axis=-1, keepdims=True)
    xn = (x * lax.rsqrt(ms + RMS_EPS)) * g_ref[...]
    proj = _dot(xn.astype(BF16), w_ref[...])
    cos = cos_ref[...]
    sin = sin_ref[...]
    first_half = (_iota(cos.shape, 1) & (HALF - 1)) < 32
    rope = lambda t: _rope_tile(t, cos, sin, first_half)
    q_ref[...] = proj[:, COL_Q:COL_Q + NSA_WIDTH]
    for c in range(NSA_WIDTH // LANES):
        qr_ref[:, c * LANES:(c + 1) * LANES] = rope(proj[:, COL_Q + c * LANES:COL_Q + (c + 1) * LANES])
    cmp = proj[:, COL_KV:COL_KV + 256]
    cmp_ref[...] = cmp
    cmpt_ref[...] = cmp.T
    ks = rope(proj[:, COL_KV + 256:COL_KV + 384])
    kw = rope(proj[:, COL_KV + 512:COL_KV + 640])
    vs_t = proj[:, COL_KV + 384:COL_KV + 512].T
    vw_t = proj[:, COL_KV + 640:COL_KV + 768].T
    slct_ref[0:LANES, :] = ks.T
    slct_ref[LANES:2 * LANES, :] = vs_t
    wint_ref[0:LANES, :] = kw.T
    wint_ref[LANES:2 * LANES, :] = vw_t
    ksb_ref[...] = ks.astype(BF16)
    kwb_ref[...] = kw.astype(BF16)
    for c in range(vst_ref.shape[0]):
        vst_ref[c] = vs_t[:, c * KEY_TILE:(c + 1) * KEY_TILE].astype(BF16)
    for c in range(vwt_ref.shape[0]):
        vwt_ref[c] = vw_t[:, c * LANES:(c + 1) * LANES].astype(BF16)
    gates_ref[...] = jax.nn.sigmoid(proj[:, COL_G:COL_G + LANES])
    h_ref[...] = proj[:, COL_H:COL_H + 4 * HG_WIDTH]


def _inproj_seq(x, norm_g, w_all, cos_t, sin_t, nb, t, tm):
    n = nb * t
    steps = t // tm
    row = lambda w: pl.BlockSpec((tm, w), lambda i: (i, 0))
    tab = pl.BlockSpec((tm, LANES), lambda i: (i % steps, 0))
    feat = pl.BlockSpec((None, 256, tm), lambda i: (i // steps, 0, i % steps))
    tiles = lambda w: pl.BlockSpec((tm // w, LANES, w), lambda i: (i, 0, 0))
    f32 = lambda *s: jax.ShapeDtypeStruct(s, F32)
    bf = lambda *s: jax.ShapeDtypeStruct(s, BF16)
    return pl.pallas_call(
        _inproj_seq_kernel,
        grid=(n // tm,),
        in_specs=[row(D_MODEL), pl.BlockSpec((1, D_MODEL), lambda i: (0, 0)),
                  pl.BlockSpec((D_MODEL, PROJ_PAD), lambda i: (0, 0)), tab, tab],
        out_specs=[row(NSA_WIDTH), row(NSA_WIDTH), row(256), feat, feat, feat, row(LANES), row(LANES),
                   tiles(KEY_TILE), tiles(LANES), row(LANES), row(4 * HG_WIDTH)],
        out_shape=[f32(n, NSA_WIDTH), f32(n, NSA_WIDTH), f32(n, 256), f32(nb, 256, t), f32(nb, 256, t),
                   f32(nb, 256, t), bf(n, LANES), bf(n, LANES), bf(n // KEY_TILE, LANES, KEY_TILE),
                   bf(n // LANES, LANES, LANES), f32(n, LANES), f32(n, 4 * HG_WIDTH)],
        compiler_params=_params("parallel"),
        name="inproj_seq",
    )(x, norm_g, w_all, cos_t, sin_t)


def _inproj_kernel(x_ref, g_ref, w_ref, cos_ref, sin_ref,
                   q_ref, qr_ref, cmp_ref, slc_ref, win_ref, gates_ref, h_ref):
    x = x_ref[...]
    ms = jnp.mean(x * x, axis=-1, keepdims=True)
    xn = (x * lax.rsqrt(ms + RMS_EPS)) * g_ref[...]
    proj = _dot(xn.astype(BF16), w_ref[...])
    cos = cos_ref[...]
    sin = sin_ref[...]
    first_half = (_iota(cos.shape, 1) & (HALF - 1)) < 32
    rope = lambda t: _rope_tile(t, cos, sin, first_half)
    q_ref[...] = proj[:, COL_Q:COL_Q + NSA_WIDTH]
    for c in range(NSA_WIDTH // LANES):
        qr_ref[:, c * LANES:(c + 1) * LANES] = rope(proj[:, COL_Q + c * LANES:COL_Q + (c + 1) * LANES])
    cmp_ref[...] = proj[:, COL_KV:COL_KV + 256]
    ks = rope(proj[:, COL_KV + 256:COL_KV + 384])
    vs = proj[:, COL_KV + 384:COL_KV + 512]
    kw = rope(proj[:, COL_KV + 512:COL_KV + 640])
    vw = proj[:, COL_KV + 640:COL_KV + 768]
    slc_ref[:, 0:LANES] = ks
    slc_ref[:, LANES:2 * LANES] = vs
    win_ref[:, 0:LANES] = kw
    win_ref[:, LANES:2 * LANES] = vw
    gates_ref[...] = jax.nn.sigmoid(proj[:, COL_G:COL_G + LANES])
    h_ref[...] = proj[:, COL_H:COL_H + 4 * HG_WIDTH]


def _inproj(x, norm_g, w_all, cos_t, sin_t, tm):
    n = x.shape[0]
    ntab = cos_t.shape[0] // tm
    row = lambda w: pl.BlockSpec((tm, w), lambda i: (i, 0))
    tab = pl.BlockSpec((tm, LANES), lambda i: (i % ntab, 0))
    out_shapes = [
        jax.ShapeDtypeStruct((n, NSA_WIDTH), F32), jax.ShapeDtypeStruct((n, NSA_WIDTH), F32),
        jax.ShapeDtypeStruct((n, 256), F32), jax.ShapeDtypeStruct((n, 256), F32),
        jax.ShapeDtypeStruct((n, 256), F32), jax.ShapeDtypeStruct((n, LANES), F32),
        jax.ShapeDtypeStruct((n, 4 * HG_WIDTH), F32),
    ]
    return pl.pallas_call(
        _inproj_kernel,
        grid=(n // tm,),
        in_specs=[row(D_MODEL), pl.BlockSpec((1, D_MODEL), lambda i: (0, 0)),
                  pl.BlockSpec((D_MODEL, PROJ_PAD), lambda i: (0, 0)), tab, tab],
        out_specs=[row(NSA_WIDTH), row(NSA_WIDTH), row(256), row(256), row(256), row(LANES), row(4 * HG_WIDTH)],
        out_shape=out_shapes,
        compiler_params=_params("parallel"),
        name="inproj",
    )(x, norm_g, w_all, cos_t, sin_t)


def _rope_tables(pos):
    half = HEAD_DIM // 2
    inv = ROPE_THETA ** (-jnp.arange(half, dtype=F32) / half)
    ang = pos.astype(F32)[:, None] * inv[None, :]
    cos = jnp.tile(jnp.cos(ang), (1, 4))
    sin = jnp.sin(ang)
    return cos, jnp.tile(jnp.concatenate([-sin, sin], axis=1), (1, 2))


def _compress_rows(rows_ref, n_out, w_ref, pe_ref, pitch=CMP_STRIDE):
    a = jnp.zeros((n_out, LANES), F32)
    b = jnp.zeros((n_out, LANES), F32)
    for j in range(CMP_STRIDE):
        xj = rows_ref[pl.ds(j, n_out, stride=pitch), :]
        a = a + _dot((xj + pe_ref[j:j + 1, :]).astype(BF16), w_ref[j])
        b = b + _dot((xj + pe_ref[CMP_STRIDE + j:CMP_STRIDE + j + 1, :]).astype(BF16), w_ref[CMP_STRIDE + j])
    return a + pltpu.roll(b, n_out - 1, 0)


def _compress_prompt_kernel(rk_ref, rv_ref, wk_ref, wv_ref, pek_ref, pev_ref, out_ref):
    n_out = out_ref.shape[0]
    out_ref[:, 0:LANES] = _compress_rows(rk_ref, n_out, wk_ref, pek_ref).astype(BF16)
    out_ref[:, LANES:2 * LANES] = _compress_rows(rv_ref, n_out, wv_ref, pev_ref).astype(BF16)


def _compress_prompt(rows, cw):
    b, t, _ = rows.shape
    n_out = t // CMP_STRIDE
    wspec = pl.BlockSpec((CMP_BLOCK, LANES, LANES), lambda i: (0, 0, 0))
    pspec = pl.BlockSpec((CMP_BLOCK, LANES), lambda i: (0, 0))
    return pl.pallas_call(
        _compress_prompt_kernel,
        grid=(b,),
        in_specs=[pl.BlockSpec((None, t, LANES), lambda i: (i, 0, 0)),
                  pl.BlockSpec((None, t, LANES), lambda i: (i, 0, 1)), wspec, wspec, pspec, pspec],
        out_specs=pl.BlockSpec((None, n_out, 256), lambda i: (i, 0, 0)),
        out_shape=jax.ShapeDtypeStruct((b, n_out, 256), BF16),
        compiler_params=_params("parallel"),
        name="compress_prompt",
    )(rows, rows, *cw)


def _stack_heads(ref, lo_half, nq):
    hi_half = jnp.logical_not(lo_half)
    parts = []
    for h in range(NSA_HEADS):
        g = h // NSA_GROUP
        tile = ref[:, LANES * (h // 2):LANES * (h // 2 + 1)]
        if h % 2 != g:
            tile = pltpu.roll(tile, HALF, 1)
        parts.append(jnp.where(lo_half if g == 0 else hi_half, tile, 0.0))
    return (jnp.concatenate(parts, axis=0) * (HEAD_DIM ** -0.5)).astype(BF16)


def _per_head(x, nq):
    return jnp.concatenate([x[0:nq]] * NSA_GROUP + [x[nq:2 * nq]] * NSA_GROUP, axis=0)


def _masked_softmax_rows(s, valid):
    sm = jnp.where(valid, s, NEG_INF)
    m = jnp.max(sm, axis=-1, keepdims=True)
    e = jnp.where(valid, jnp.exp(sm - m), 0.0)
    l = jnp.sum(e, axis=-1, keepdims=True)
    return e / jnp.where(l > 0.0, l, 1.0)


def _select_blocks(imp, pos_col):
    blk = _iota(imp.shape, 1)
    cur = pos_col >> 6
    forced = (blk == 0) | (blk == cur) | (blk == cur - 1)
    causal = (blk << 6) <= pos_col
    score = jnp.where(causal, jnp.where(forced, FORCED_SCORE, imp), -FORCED_SCORE)
    return score, causal


def _topk_mask_t(score_t, n_sel):
    bi = _iota(score_t.shape, 0)
    sel = jnp.zeros(score_t.shape, F32)
    sc = score_t
    for _ in range(n_sel):
        m = jnp.max(sc, axis=0, keepdims=True)
        idx = jnp.min(jnp.where(sc == m, bi, LANES), axis=0, keepdims=True)
        hit = bi == idx
        sel = jnp.where(hit, 1.0, sel)
        sc = jnp.where(hit, LOWEST, sc)
    return sel


def _cmp_branch(qn, kc, vc, ov, pos_rows, pos_grp, nq):
    s = _dot_nt(qn, kc)
    valid = (_iota(s.shape, 1) * CMP_STRIDE + (CMP_BLOCK - 1)) <= pos_rows
    p = _masked_softmax_rows(s, valid)
    o_c = _dot(p.astype(BF16), vc)
    slab = lambda h: p[h * nq:(h + 1) * nq]
    psum = jnp.concatenate([slab(0) + slab(1) + slab(2) + slab(3), slab(4) + slab(5) + slab(6) + slab(7)], axis=0)
    hi = psum.astype(BF16)
    lo = (psum - hi.astype(F32)).astype(BF16)
    imp = _dot(hi, ov) + _dot(lo, ov)
    score, causal = _select_blocks(imp, pos_grp)
    return o_c, score, causal


def _combine_heads(out_ref, gates, lo_half, o_c, o_s, o_w, nq):
    combs = []
    for h in range(NSA_HEADS):
        rows = slice(h * nq, (h + 1) * nq)
        comb = (gates[:, h:h + 1] * o_c[rows] + gates[:, NSA_HEADS + h:NSA_HEADS + h + 1] * o_s[rows]
                + gates[:, 2 * NSA_HEADS + h:2 * NSA_HEADS + h + 1] * o_w[rows])
        if h % 2 != h // NSA_GROUP:
            comb = pltpu.roll(comb, HALF, 1)
        combs.append(comb)
    for r in range(NSA_HEADS // 2):
        out_ref[:, LANES * r:LANES * (r + 1)] = jnp.where(lo_half, combs[2 * r], combs[2 * r + 1])


KEY_TILE = 512


def _masked_softmax_cols(s, valid):
    sm = jnp.where(valid, s, NEG_INF)
    m = jnp.max(sm, axis=0, keepdims=True)
    e = jnp.where(valid, jnp.exp(sm - m), 0.0)
    l = jnp.sum(e, axis=0, keepdims=True)
    return e / jnp.where(l > 0.0, l, 1.0)


def _nsa_prompt_kernel(q_ref, qr_ref, gates_ref, cmp_ref, vct_ref, ks_ref, vst_ref, kw_ref, vwt_ref, ovt_ref, et_ref,
                       out_ref, *, tq):
    t0 = pl.program_id(1) * tq
    cols = NSA_HEADS * tq
    grp = NSA_KV_HEADS * tq
    pos_cols = t0 + (_iota((1, cols), 1) & (tq - 1))
    pos_grp = t0 + (_iota((1, grp), 1) & (tq - 1))
    n_tiles = (t0 + tq + KEY_TILE - 1) // KEY_TILE
    w_chunk = jnp.maximum(t0 - WINDOW, 0) // LANES
    w_start = pl.multiple_of(w_chunk * LANES, LANES)
    w_len = WINDOW + tq
    per_head = lambda x: jnp.concatenate([x[:, 0:tq]] * NSA_GROUP + [x[:, tq:grp]] * NSA_GROUP, axis=1)

    lo_half = _iota((tq, LANES), 1) < HALF
    qn = _stack_heads(q_ref, lo_half, tq)
    qr = _stack_heads(qr_ref, lo_half, tq)

    all_heads = lambda x: jnp.concatenate([x] * NSA_HEADS, axis=1)
    pos_tok = t0 + _iota((1, tq), 1)

    sc = _dot_nt(cmp_ref[:, 0:LANES], qn)
    n_i = _iota((sc.shape[0], tq), 0)
    sc = sc + all_heads(jnp.where(n_i * CMP_STRIDE + (CMP_BLOCK - 1) <= pos_tok, 0.0, NEG_INF))
    ec = jnp.exp(sc - jnp.max(sc, axis=0, keepdims=True))
    norm_c = jnp.where(pos_cols >= CMP_BLOCK - 1, 1.0 / jnp.sum(ec, axis=0, keepdims=True), 0.0)
    o_c = _dot(vct_ref[...], ec.astype(BF16)) * norm_c
    pc = ec * norm_c
    slab = lambda h: pc[:, h * tq:(h + 1) * tq]
    psum = jnp.concatenate([slab(0) + slab(1) + slab(2) + slab(3), slab(4) + slab(5) + slab(6) + slab(7)], axis=1)
    hi = psum.astype(BF16)
    lo = (psum - hi.astype(F32)).astype(BF16)
    ovt = ovt_ref[...]
    imp = _dot(ovt, hi) + _dot(ovt, lo)
    blk = _iota(imp.shape, 0)
    cur = pos_grp >> 6
    forced = (blk == 0) | (blk == cur) | (blk == cur - 1)
    causal = (blk << 6) <= pos_grp
    score = jnp.where(causal, jnp.where(forced, FORCED_SCORE, imp), -FORCED_SCORE)
    sel = jnp.where(causal, _topk_mask_t(score, SLC_TOPN), 0.0)
    notsel = (1.0 - sel).astype(BF16)

    sw = _dot_nt(kw_ref[pl.ds(w_start, w_len), :], qr)
    kp = w_start + _iota((w_len, tq), 0)
    sw = sw + all_heads(jnp.where((kp <= pos_tok) & (kp >= pos_tok - WINDOW), 0.0, NEG_INF))
    ew = jnp.exp(sw - jnp.max(sw, axis=0, keepdims=True))
    vwt = jnp.concatenate([vwt_ref[w_chunk + c] for c in range(w_len // LANES)], axis=1)
    o_w = _dot(vwt, ew.astype(BF16)) / jnp.sum(ew, axis=0, keepdims=True)

    def tile_step(kt, carry, diag):
        m, l, acc = carry
        k0 = pl.multiple_of(kt * KEY_TILE, KEY_TILE)
        s = _dot_nt(ks_ref[pl.ds(k0, KEY_TILE), :], qr)
        s = s + per_head(_dot(et_ref[pl.ds(k0, KEY_TILE), :], notsel) * NEG_INF)
        if diag:
            s = s + all_heads(jnp.where(k0 + _iota((KEY_TILE, tq), 0) <= pos_tok, 0.0, NEG_INF))
        m_new = jnp.maximum(m, jnp.max(s, axis=0, keepdims=True))
        alpha = jnp.exp(m - m_new)
        p = jnp.exp(s - m_new)
        l = alpha * l + jnp.sum(p, axis=0, keepdims=True)
        acc = alpha * acc + _dot(vst_ref[kt], p.astype(BF16))
        return m_new, l, acc

    init = (jnp.full((1, cols), NEG_INF, F32), jnp.zeros((1, cols), F32), jnp.zeros((LANES, cols), F32))
    carry = lax.fori_loop(0, n_tiles - 1, lambda kt, c: tile_step(kt, c, False), init)
    _, l, acc = tile_step(n_tiles - 1, carry, True)
    o_s = acc / l

    gt = gates_ref[...].T
    for r in range(NSA_HEADS // 2):
        halves = []
        for h in (2 * r, 2 * r + 1):
            c = slice(h * tq, (h + 1) * tq)
            comb = (gt[h:h + 1] * o_c[:, c] + gt[NSA_HEADS + h:NSA_HEADS + h + 1] * o_s[:, c]
                    + gt[2 * NSA_HEADS + h:2 * NSA_HEADS + h + 1] * o_w[:, c])
            g = h // NSA_GROUP
            halves.append(comb[HALF * g:HALF * (g + 1)])
        out_ref[:, LANES * r:LANES * (r + 1)] = jnp.concatenate(halves, axis=0).T


def _nsa_prompt(q, qr, gates, kcvc, ksb, vst, kwb, vwt, ovt, et, tq):
    b, t, _ = q.shape
    ncp = kcvc.shape[1]
    vct = kcvc[:, :, LANES:].transpose(0, 2, 1)
    blk = lambda w: pl.BlockSpec((None, tq, w), lambda bi, i: (bi, i, 0))
    full = lambda r, w: pl.BlockSpec((None, r, w), lambda bi, i: (bi, 0, 0))
    full4 = lambda a: pl.BlockSpec((None,) + a.shape[1:], lambda bi, i: (bi, 0, 0, 0))
    return pl.pallas_call(
        functools.partial(_nsa_prompt_kernel, tq=tq),
        grid=(b, t // tq),
        in_specs=[blk(NSA_WIDTH), blk(NSA_WIDTH), blk(LANES), full(ncp, 256), full(LANES, ncp), full(t, LANES),
                  full4(vst), full(t, LANES), full4(vwt),
                  pl.BlockSpec((LANES, ncp), lambda bi, i: (0, 0)),
                  pl.BlockSpec((t, LANES), lambda bi, i: (0, 0))],
        out_specs=blk(NSA_WIDTH),
        out_shape=jax.ShapeDtypeStruct((b, t, NSA_WIDTH), F32),
        compiler_params=_params("parallel", "arbitrary"),
        name="nsa_prompt",
    )(q, qr, gates, kcvc, vct, ksb, vst, kwb, vwt, ovt, et)


def _overlap_matrix(n_cmp_pad, n_cmp):
    c0 = jnp.arange(n_cmp_pad, dtype=jnp.int32)[:, None] * CMP_STRIDE
    s0 = jnp.arange(LANES, dtype=jnp.int32)[None, :] * SLC_BLOCK
    real = jnp.arange(n_cmp_pad, dtype=jnp.int32)[:, None] < n_cmp
    return ((c0 < s0 + SLC_BLOCK) & (c0 + CMP_BLOCK > s0) & real).astype(BF16)


def _block_expand_matrix(n_keys):
    r = jnp.arange(n_keys, dtype=jnp.int32)[:, None] // SLC_BLOCK
    return (r == jnp.arange(LANES, dtype=jnp.int32)[None, :]).astype(BF16)


PAGE_ROWS = 128
CMP_SEQS_PER_STEP = 4
GROUP_PITCH = 24


def _compress_paged_kernel(pt_ref, *refs, n_pages):
    pages = refs[:n_pages]
    wk_ref, wv_ref, pek_ref, pev_ref, out_ref, rk_scr, rv_scr = refs[n_pages:]
    groups = PAGE_ROWS // CMP_STRIDE
    for p in range(n_pages):
        for scr, half in ((rk_scr, 0), (rv_scr, 1)):
            rows = pages[p][half * LANES:(half + 1) * LANES, :].T
            for m in range(groups):
                r0 = (p * groups + m) * GROUP_PITCH
                scr[r0:r0 + CMP_STRIDE, :] = rows[m * CMP_STRIDE:(m + 1) * CMP_STRIDE]
    n_out = out_ref.shape[0]
    out_ref[:, 0:LANES] = _compress_rows(rk_scr, n_out, wk_ref, pek_ref, GROUP_PITCH).astype(BF16)
    out_ref[:, LANES:2 * LANES] = _compress_rows(rv_scr, n_out, wv_ref, pev_ref, GROUP_PITCH).astype(BF16)


def _page_specs(n_pages):
    return [pl.BlockSpec((None, 256, PAGE_ROWS), functools.partial(lambda i, pt, p: (pt[i * n_pages + p], 0, 0), p=p))
            for p in range(n_pages)]


def _feature_major(cache, lead):
    return cache[0].reshape(lead, cache.shape[2], 256).transpose(0, 2, 1)


def _compress_paged(cache, page_table, cw):
    db, n_pages = page_table.shape
    past = n_pages * PAGE_ROWS
    n_out = past // CMP_STRIDE
    wspec = pl.BlockSpec((CMP_BLOCK, LANES, LANES), lambda i, pt: (0, 0, 0))
    pspec = pl.BlockSpec((CMP_BLOCK, LANES), lambda i, pt: (0, 0))
    grid_spec = pltpu.PrefetchScalarGridSpec(
        num_scalar_prefetch=1, grid=(db,),
        in_specs=_page_specs(n_pages) + [wspec, wspec, pspec, pspec],
        out_specs=pl.BlockSpec((None, n_out, 256), lambda i, pt: (i, 0, 0)),
        scratch_shapes=[pltpu.VMEM((n_out * GROUP_PITCH, LANES), F32), pltpu.VMEM((n_out * GROUP_PITCH, LANES), F32)])
    return pl.pallas_call(
        functools.partial(_compress_paged_kernel, n_pages=n_pages),
        grid_spec=grid_spec,
        out_shape=jax.ShapeDtypeStruct((db, n_out, 256), BF16),
        compiler_params=_params("arbitrary"),
        name="compress_paged",
    )(page_table.reshape(-1), *([cache] * n_pages), *cw)


KEY_PAD = LANES
NSA_SEQS_PER_STEP = 2


def _nsa_sample_kernel(pt_ref, q_ref, qr_ref, gates_ref, cmp_ref, *refs, n_pages):
    n_seq = cmp_ref.shape[0]
    pages = refs[:n_seq * n_pages]
    snew_ref, cwin_ref, wnew_ref, ov_ref, et_ref, out_ref, nwin_ref = refs[n_seq * n_pages:]
    nq = q_ref.shape[0] // n_seq
    past = n_pages * PAGE_ROWS
    wb = cwin_ref.shape[2]
    rows = NSA_HEADS * nq
    n_grp = NSA_KV_HEADS * nq
    pad_rows = lambda x: jnp.concatenate([x, jnp.zeros((KEY_PAD - nq, x.shape[1]), F32)], axis=0)
    lo_half = _iota((nq, LANES), 1) < HALF
    pos_rows = past + (_iota((rows, 1), 0) & (nq - 1))
    pos_grp = past + (_iota((n_grp, 1), 0) & (nq - 1))
    seq_rows = lambda ref, b: ref.at[b * nq:(b + 1) * nq, :]

    qrs, o_cs, scores = [], [], []
    for b in range(n_seq):
        qn = _stack_heads(seq_rows(q_ref, b), lo_half, nq)
        qrs.append(_stack_heads(seq_rows(qr_ref, b), lo_half, nq))
        o_c, score, causal = _cmp_branch(qn, cmp_ref[b, :, 0:LANES], cmp_ref[b, :, LANES:2 * LANES], ov_ref[...],
                                         pos_rows, pos_grp, nq)
        o_cs.append(o_c)
        scores.append(score)
    score_sq = jnp.concatenate(scores + [jnp.full((LANES - n_seq * n_grp, LANES), LOWEST, F32)], axis=0)
    sel_all = _topk_mask_t(score_sq.T, SLC_TOPN).T

    for b in range(n_seq):
        pg = pages[b * n_pages:(b + 1) * n_pages]
        qr = qrs[b]
        snew = pad_rows(snew_ref[b * nq:(b + 1) * nq, :])
        wnew = pad_rows(wnew_ref[b * nq:(b + 1) * nq, :])

        shifted = pltpu.roll(cwin_ref[b], wb - nq, 1)
        new_t = jnp.concatenate([wnew[:, 0:LANES].T, wnew[:, LANES:2 * LANES].T], axis=0)
        tail = jnp.where(_iota((256, LANES), 1) >= LANES - nq, pltpu.roll(new_t, LANES - nq, 1),
                         shifted[:, wb - LANES:wb])
        nwin_ref[b, :, 0:wb - LANES] = shifted[:, 0:wb - LANES]
        nwin_ref[b, :, wb - LANES:wb] = tail

        sel = jnp.where(causal, sel_all[b * n_grp:(b + 1) * n_grp], 0.0)
        notsel = (1.0 - sel).astype(BF16)
        s = jnp.concatenate([_dot(qr, pg[p][0:LANES, :].astype(BF16)) for p in range(n_pages)]
                            + [_dot_nt(qr, snew[:, 0:LANES].astype(BF16))], axis=1)
        s = s + _per_head(_dot_nt(notsel, et_ref[...]) * NEG_INF, nq)
        s = jnp.where(_iota(s.shape, 1) <= pos_rows, s, NEG_INF)
        e = jnp.exp(s - jnp.max(s, axis=-1, keepdims=True))
        eb = e.astype(BF16)
        o_s = _dot(eb[:, past:past + KEY_PAD], snew[:, LANES:2 * LANES].astype(BF16))
        for p in range(n_pages):
            o_s = o_s + _dot_nt(eb[:, p * PAGE_ROWS:(p + 1) * PAGE_ROWS], pg[p][LANES:2 * LANES, :].astype(BF16))
        o_s = o_s / jnp.sum(e, axis=-1, keepdims=True)

        sw = jnp.concatenate([_dot(qr, cwin_ref[b, 0:LANES, :].astype(BF16)),
                              _dot_nt(qr, wnew[:, 0:LANES].astype(BF16))], axis=1)
        kp = (past - wb) + _iota(sw.shape, 1)
        pw = _masked_softmax_rows(sw, (kp <= pos_rows) & (kp >= pos_rows - WINDOW) & (kp >= 0)).astype(BF16)
        o_w = (_dot_nt(pw[:, 0:wb], cwin_ref[b, LANES:2 * LANES, :].astype(BF16))
               + _dot(pw[:, wb:wb + KEY_PAD], wnew[:, LANES:2 * LANES].astype(BF16)))
        _combine_heads(seq_rows(out_ref, b), seq_rows(gates_ref, b)[...], lo_half, o_cs[b], o_s, o_w, nq)


def _nsa_sample(q, qr, gates, kcvc, cache_slc, slc_new, cache_win, win_new, page_table, ov, et, nq):
    db, n_pages = page_table.shape
    past = n_pages * PAGE_ROWS
    wb = cache_win.shape[2]
    ncp = kcvc.shape[1]
    n_seq = NSA_SEQS_PER_STEP
    blk = lambda w: pl.BlockSpec((n_seq * nq, w), lambda i, pt: (i, 0))
    win_spec = pl.BlockSpec((n_seq, 256, wb), lambda i, pt: (i, 0, 0))
    grid_spec = pltpu.PrefetchScalarGridSpec(
        num_scalar_prefetch=1, grid=(db // n_seq,),
        in_specs=[blk(NSA_WIDTH), blk(NSA_WIDTH), blk(LANES),
                  pl.BlockSpec((n_seq, ncp, 256), lambda i, pt: (i, 0, 0))] + _page_specs(n_seq * n_pages) + [
                  blk(256), win_spec, blk(256),
                  pl.BlockSpec((ncp, LANES), lambda i, pt: (0, 0)),
                  pl.BlockSpec((past + KEY_PAD, LANES), lambda i, pt: (0, 0))],
        out_specs=[blk(NSA_WIDTH), win_spec])
    return pl.pallas_call(
        functools.partial(_nsa_sample_kernel, n_pages=n_pages),
        grid_spec=grid_spec,
        out_shape=[jax.ShapeDtypeStruct((db * nq, NSA_WIDTH), F32), jax.ShapeDtypeStruct((db, 256, wb), F32)],
        compiler_params=_params("arbitrary"),
        name="nsa_sample",
    )(page_table.reshape(-1), q, qr, gates, kcvc, *([cache_slc] * (n_seq * n_pages)), slc_new, cache_win, win_new, ov, et)


HG_ROWS = 128
HG_SUB = 16


def _dot_split3(m_bf16, x):
    a = x.astype(BF16)
    r = x - a.astype(F32)
    b = r.astype(BF16)
    c = (r - b.astype(F32)).astype(BF16)
    return _dot(m_bf16, a) + _dot(m_bf16, b) + _dot(m_bf16, c)


def _hgrn_chunk(q, fpre, v, gpre, lb, ng, s0, n_real):
    f = lb + (1.0 - lb) * jax.nn.sigmoid(fpre)
    logf = jnp.log(f)
    kk = 1.0 - f
    if n_real < HG_ROWS:
        pad = lambda a: jnp.concatenate([a, jnp.zeros((HG_ROWS - n_real, LANES), F32)], axis=0)
        q, logf, kk, v = pad(q), pad(logf), pad(kk), pad(v)
    r_i = _iota((HG_ROWS, HG_ROWS), 0)
    c_i = _iota((HG_ROWS, HG_ROWS), 1)
    tri_b = r_i >= c_i
    tri = jnp.where(tri_b, 1.0, 0.0).astype(BF16)
    cum = _dot_split3(tri, logf)
    cprev = cum - logf
    cum_last = cum[HG_ROWS - 1:HG_ROWS, :]
    s0b = s0.astype(BF16)
    o = _dot((q * jnp.exp(cum)).astype(BF16), s0b)
    row = _iota((HG_ROWS, LANES), 0)
    a_rows = []
    n_sub = -(-n_real // HG_SUB)
    for i in range(n_sub):
        r0 = i * HG_SUB
        c_ref = cprev[r0:r0 + 1, :]
        qt = q[r0:r0 + HG_SUB] * jnp.exp(cum[r0:r0 + HG_SUB] - c_ref)
        kt = kk * jnp.exp(jnp.where(row < r0 + HG_SUB, c_ref - cum, NEG_INF))
        a_rows.append(_dot_nt(qt.astype(BF16), kt.astype(BF16)))
    if n_sub * HG_SUB < HG_ROWS:
        a_rows.append(jnp.zeros((HG_ROWS - n_sub * HG_SUB, HG_ROWS), F32))
    a = jnp.where(tri_b, jnp.concatenate(a_rows, axis=0), 0.0)
    vb = v.astype(BF16)
    o = o + _dot(a.astype(BF16), vb)
    kdec = kk * jnp.exp(cum_last - cum)
    scale = jnp.broadcast_to(jnp.exp(cum_last), (HG_ROWS, LANES)).T
    s_new = scale * s0 + _dot(kdec.T.astype(BF16), vb)
    o = o[0:n_real]
    o = o * lax.rsqrt(jnp.mean(o * o, axis=-1, keepdims=True) + RMS_EPS) * ng
    return o * jax.nn.sigmoid(gpre), s_new


def _hgrn_kernel(q_ref, f_ref, v_ref, g_ref, s0_ref, lb_ref, ng_ref, o_ref, sfin_ref, s_scr, *, n_chunks, n_real):
    @pl.when(pl.program_id(1) == 0)
    def _init():
        s_scr[...] = s0_ref[...]

    lbp = lb_ref[...]
    e = jnp.exp(lbp - jnp.max(lbp, axis=0, keepdims=True))
    lb = e[0:1, :] / jnp.sum(e, axis=0, keepdims=True)
    ng = ng_ref[...]

    def chunk(c, carry):
        rows = slice(0, n_real) if n_chunks == 1 else pl.ds(pl.multiple_of(c * n_real, n_real), n_real)
        for hd in range(HG_HEADS):
            cols = slice(hd * LANES, (hd + 1) * LANES)
            o, s_new = _hgrn_chunk(q_ref[rows, cols], f_ref[rows, cols], v_ref[rows, cols], g_ref[rows, cols],
                                   lb[:, cols], ng[:, cols], s_scr[hd], n_real)
            o_ref[rows, cols] = o
            s_scr[hd] = s_new
        return carry

    if n_chunks == 1:
        chunk(0, 0)
    else:
        lax.fori_loop(0, n_chunks, chunk, 0)
    sfin_ref[...] = s_scr[...]


def _hgrn(h, s0, hg_lb, hg_norm, nb, t, n_chunks, n_real):
    rows = n_chunks * n_real
    steps = t // rows
    col = lambda sec: pl.BlockSpec((rows, HG_WIDTH), lambda b, j: (b * steps + j, sec))
    st = pl.BlockSpec((None, HG_HEADS, HG_DK, HG_DV), lambda b, j: (b, 0, 0, 0))
    return pl.pallas_call(
        functools.partial(_hgrn_kernel, n_chunks=n_chunks, n_real=n_real),
        grid=(nb, steps),
        in_specs=[col(0), col(1), col(2), col(3), st,
                  pl.BlockSpec(hg_lb.shape, lambda b, j: (0, 0)),
                  pl.BlockSpec((1, HG_WIDTH), lambda b, j: (0, 0))],
        out_specs=[pl.BlockSpec((rows, HG_WIDTH), lambda b, j: (b * steps + j, 0)), st],
        out_shape=[jax.ShapeDtypeStruct((nb * t, HG_WIDTH), F32),
                   jax.ShapeDtypeStruct((nb, HG_HEADS, HG_DK, HG_DV), F32)],
        scratch_shapes=[pltpu.VMEM((HG_HEADS, HG_DK, HG_DV), F32)],
        compiler_params=_params("parallel", "arbitrary"),
        name="hgrn",
    )(h, h, h, h, s0, hg_lb, hg_norm)


def _outproj_kernel(x_ref, a_ref, hg_ref, wo_ref, nf_ref, wq_ref, x1_ref, xn_ref, pq_ref):
    mix = jnp.concatenate([a_ref[...], hg_ref[...]], axis=1).astype(BF16)
    x1 = x_ref[...] + _dot(mix, wo_ref[...])
    x1_ref[...] = x1
    ms = jnp.mean(x1 * x1, axis=-1, keepdims=True)
    xb = ((x1 * lax.rsqrt(ms + RMS_EPS)) * nf_ref[...]).astype(BF16)
    xn_ref[...] = xb
    pq_ref[...] = _dot(xb, wq_ref[...])


def _outproj(x, attn, ohg, w_out, norm_ffn, wq, tm):
    n = x.shape[0]
    row = lambda w: pl.BlockSpec((tm, w), lambda i: (i, 0))
    full = lambda a: pl.BlockSpec(a.shape, lambda i: (0, 0))
    nq = wq.shape[1]
    return pl.pallas_call(
        _outproj_kernel,
        grid=(n // tm,),
        in_specs=[row(D_MODEL), row(NSA_WIDTH), row(HG_WIDTH), full(w_out), full(norm_ffn), full(wq)],
        out_specs=[row(D_MODEL), row(D_MODEL), row(nq)],
        out_shape=[jax.ShapeDtypeStruct((n, D_MODEL), F32), jax.ShapeDtypeStruct((n, D_MODEL), BF16),
                   jax.ShapeDtypeStruct((n, nq), F32)],
        compiler_params=_params("parallel"),
        name="outproj",
    )(x, attn, ohg, w_out, norm_ffn, wq)


def _top_rows(s, k, val_scr, idx_scr):
    bi = _iota(s.shape, 0)
    big = s.shape[0]
    for a in range(k):
        m = jnp.max(s, axis=0, keepdims=True)
        idx = jnp.min(jnp.where(s == m, bi, big), axis=0, keepdims=True)
        val_scr[a:a + 1, :] = m
        idx_scr[a:a + 1, :] = idx
        s = jnp.where(bi == idx, LOWEST, s)


def _peer_select_kernel(pq_ref, keys_ref, i_ref, j_ref, g_ref, v12, i12, sc, cd):
    tms = pq_ref.shape[0]
    half = PEER_QDIM // 2
    s1 = _dot_nt(keys_ref[0].astype(BF16), pq_ref[:, 0:half].astype(BF16))
    s2 = _dot_nt(keys_ref[1].astype(BF16), pq_ref[:, half:2 * half].astype(BF16))
    _top_rows(jnp.concatenate([s1, s2], axis=1), PEER_TOPK, v12, i12)
    a1 = v12[:, 0:tms]
    a2 = v12[:, tms:2 * tms]
    r16 = _iota((PEER_TOPK, tms), 0)
    r8 = _iota((8, tms), 0)
    parts = [a1[0:1] + a2]
    codes = [r16]
    for a in range(1, 8):
        parts.append(a1[a:a + 1] + a2[0:8])
        codes.append(r8 + PEER_TOPK * a)
    parts.append(a1[8:16] + a2[0:1])
    codes.append((r8 + 8) * PEER_TOPK)
    cand = jnp.concatenate(parts, axis=0)
    code = jnp.concatenate(codes, axis=0)
    for k in range(PEER_TOPK):
        m = jnp.max(cand, axis=0, keepdims=True)
        cs = jnp.min(jnp.where(cand == m, code, PEER_TOPK * PEER_TOPK), axis=0, keepdims=True)
        sc[k:k + 1, :] = m
        cd[k:k + 1, :] = cs
        cand = jnp.where(code == cs, LOWEST, cand)
    scv = sc[...]
    cdv = cd[...]
    ak = cdv >> 4
    bk = cdv & (PEER_TOPK - 1)
    idx1 = i12[:, 0:tms]
    idx2 = i12[:, tms:2 * tms]
    ik = jnp.zeros((PEER_TOPK, tms), jnp.int32)
    jk = jnp.zeros((PEER_TOPK, tms), jnp.int32)
    for a in range(PEER_TOPK):
        ik = jnp.where(ak == a, idx1[a:a + 1], ik)
        jk = jnp.where(bk == a, idx2[a:a + 1], jk)
    e = jnp.exp(scv - scv[0:1])
    i_ref[...] = ik.astype(F32)
    j_ref[...] = jk.astype(F32)
    g_ref[...] = e / jnp.sum(e, axis=0, keepdims=True)


def _peer_select(pq, keys, tms):
    n = pq.shape[0]
    out = pl.BlockSpec((PEER_TOPK, tms), lambda i, h: (h, i))
    shp = jax.ShapeDtypeStruct((PEER_HEADS * PEER_TOPK, n), F32)
    return pl.pallas_call(
        _peer_select_kernel,
        grid=(n // tms, PEER_HEADS),
        in_specs=[pl.BlockSpec((tms, PEER_QDIM), lambda i, h: (i, h)),
                  pl.BlockSpec((None, 2, PEER_KEYS, PEER_QDIM // 2), lambda i, h: (h, 0, 0, 0))],
        out_specs=[out, out, out],
        out_shape=[shp, shp, shp],
        scratch_shapes=[pltpu.VMEM((PEER_TOPK, 2 * tms), F32), pltpu.VMEM((PEER_TOPK, 2 * tms), jnp.int32),
                        pltpu.VMEM((PEER_TOPK, tms), F32), pltpu.VMEM((PEER_TOPK, tms), jnp.int32)],
        compiler_params=_params("parallel", "arbitrary"),
        name="peer_select",
    )(pq, keys)


W_PITCH = PEER_KEYS + 8
EXPERT_CHUNK = 256
EXPERT_GROUP = 512


def _peer_dense_kernel(xn_ref, ik_ref, jk_ref, gk_ref, u_ref, v_ref, x1_ref, nf_ref, out_ref, w_scr, *, tm, te):
    e_idx = pl.program_id(1)

    @pl.when(e_idx == 0)
    def _build():
        sub = _iota((PEER_KEYS, LANES), 0).astype(F32)

        def body(n, carry):
            irow = ik_ref[pl.ds(n, 1), :]
            jrow = jk_ref[pl.ds(n, 1), :]
            grow = gk_ref[pl.ds(n, 1), :]
            a = jnp.where(irow == sub, grow, 0.0).astype(BF16)
            bt = jnp.where(jrow == sub, 1.0, 0.0).astype(BF16)
            w_scr[pl.ds(pl.multiple_of(n * W_PITCH, 8), PEER_KEYS), :] = _dot_nt(a, bt)
            return carry

        lax.fori_loop(0, tm, body, 0, unroll=32)
        out_ref[...] = jnp.zeros(out_ref.shape, F32)

    def group(gi, carry):
        xn = xn_ref[...]
        acc = None
        for c in range(EXPERT_GROUP // EXPERT_CHUNK):
            r0 = pl.multiple_of(gi * EXPERT_GROUP + c * EXPERT_CHUNK, EXPERT_CHUNK)
            rows = pl.ds(r0, EXPERT_CHUNK)
            h = _dot_nt(xn, u_ref[rows, :])
            i0 = (e_idx * te + r0) // PEER_KEYS
            wt = jnp.concatenate(
                [w_scr[pl.ds(i0 + ii, tm, stride=W_PITCH), :] for ii in range(EXPERT_CHUNK // PEER_KEYS)], axis=1)
            d = _dot((jax.nn.gelu(h) * wt).astype(BF16), v_ref[rows, :])
            acc = d if acc is None else acc + d
        out_ref[...] += acc
        return carry

    lax.fori_loop(0, te // EXPERT_GROUP, group, 0)

    @pl.when(e_idx == pl.num_programs(1) - 1)
    def _finish():
        y = x1_ref[...] + out_ref[...]
        ms = jnp.mean(y * y, axis=-1, keepdims=True)
        out_ref[...] = (y * lax.rsqrt(ms + RMS_EPS)) * nf_ref[...]


def _peer_dense(xn, ik, jk, gk, u, v, x1, norm_final, tm, te):
    n = xn.shape[0]
    n_exp = u.shape[0]
    row = lambda w: pl.BlockSpec((tm, w), lambda i, e: (i, 0), pipeline_mode=pl.Buffered(1))
    exp_spec = pl.BlockSpec((te, D_MODEL), lambda i, e: (e, 0))
    return pl.pallas_call(
        functools.partial(_peer_dense_kernel, tm=tm, te=te),
        grid=(n // tm, n_exp // te),
        in_specs=[row(D_MODEL), row(LANES), row(LANES), row(LANES), exp_spec, exp_spec, row(D_MODEL),
                  pl.BlockSpec((1, D_MODEL), lambda i, e: (0, 0))],
        out_specs=pl.BlockSpec((tm, D_MODEL), lambda i, e: (i, 0)),
        out_shape=jax.ShapeDtypeStruct((n, D_MODEL), F32),
        scratch_shapes=[pltpu.VMEM((tm * W_PITCH, LANES), F32)],
        compiler_params=_params("parallel", "arbitrary"),
        name="peer_dense",
    )(xn, ik, jk, gk, u, v, x1, norm_final)


def _prep_w_in(w_in):
    w_main = jnp.concatenate([w_in[:, :COL_KV + 768], w_in[:, COL_KV + 768 + 3 * NSA_HEADS:]], axis=1)
    w_gate = jnp.pad(w_in[:, COL_KV + 768:COL_KV + 768 + 3 * NSA_HEADS], ((0, 0), (0, LANES - 3 * NSA_HEADS)))
    return jnp.concatenate([w_main, w_gate], axis=1).astype(BF16)


def _prep_cmp(cmp_wk, cmp_wv, cmp_pek, cmp_pev):
    def bd(w):
        z = jnp.zeros_like(w)
        return jnp.concatenate([jnp.concatenate([w, z], axis=2), jnp.concatenate([z, w], axis=2)], axis=1).astype(BF16)
    dup = lambda pe: jnp.concatenate([pe, pe], axis=1).astype(F32)
    return bd(cmp_wk), bd(cmp_wv), dup(cmp_pek), dup(cmp_pev)


ROW_TILE = 512
SAMPLE_ROW_TILE = 256
NSA_Q_BLOCK = 256
HG_CHUNKS_PER_STEP = 4
SELECT_TOKENS = 512
PEER_TOKEN_TILE = 512
PEER_EXPERT_TILE = 1024


def _ffn_tail(x, attn, ohg, w_out, norm_ffn, wq, keys, u, v, norm_final):
    x1, xn, pq = _outproj(x, attn, ohg, w_out, norm_ffn, wq, ROW_TILE)
    ik, jk, gk = _peer_select(pq, keys, SELECT_TOKENS)
    return _peer_dense(xn, ik.T, jk.T, gk.T, u, v, x1, norm_final, PEER_TOKEN_TILE, PEER_EXPERT_TILE)


def kernel(x_prompt, x_sample, cache_cmp, cache_slc, cache_win, state_hgrn, page_table, norm_mix, w_in, cmp_wk, cmp_wv, cmp_pek, cmp_pev, hg_lb, hg_norm, w_out, norm_ffn, peer_wq, peer_keys, peer_u, peer_v, norm_final):
    b, t, d = x_prompt.shape
    db, tq, _ = x_sample.shape
    n_pool = cache_cmp.shape[1]
    n_pages = page_table.shape[1]
    past = n_pages * PAGE_ROWS
    wb = cache_win.shape[2]
    row = lambda a: a.reshape(1, -1)

    w_all = _prep_w_in(w_in[0])
    cw = _prep_cmp(cmp_wk[0], cmp_wv[0], cmp_pek[0], cmp_pev[0])
    w_out_b = w_out[0].astype(BF16)
    wq_b = peer_wq[0].astype(BF16)
    u_b = peer_u[0].astype(BF16)
    v_b = peer_v[0].astype(BF16)
    tail = lambda x, attn, ohg: _ffn_tail(x, attn, ohg, w_out_b, row(norm_ffn[0]), wq_b, peer_keys[0], u_b, v_b,
                                          row(norm_final))

    cos, sin = _rope_tables(jnp.arange(t, dtype=jnp.int32))
    xp = x_prompt.reshape(b * t, d)
    q, qr, cmp_p, cmp_t, slc_t, win_t, ksb, kwb, vst, vwt, gates, hp = _inproj_seq(
        xp, row(norm_mix[0]), w_all, cos, sin, b, t, ROW_TILE)
    kcvc = _compress_prompt(cmp_p.reshape(b, t, 256), cw)
    ncp = t // CMP_STRIDE
    attn_p = _nsa_prompt(q.reshape(b, t, -1), qr.reshape(b, t, -1), gates.reshape(b, t, -1), kcvc,
                         ksb.reshape(b, t, LANES), vst.reshape(b, t // KEY_TILE, LANES, KEY_TILE),
                         kwb.reshape(b, t, LANES), vwt.reshape(b, t // LANES, LANES, LANES),
                         _overlap_matrix(ncp, ncp - 1).T, _block_expand_matrix(t), NSA_Q_BLOCK)
    ohg_p, s_p = _hgrn(hp, jnp.zeros((b, HG_HEADS, HG_DK, HG_DV), F32), hg_lb, row(hg_norm[0]), b, t,
                       HG_CHUNKS_PER_STEP, HG_ROWS)
    y_p = tail(xp, attn_p.reshape(b * t, -1), ohg_p)

    pos_s = past + (jnp.arange(db * tq, dtype=jnp.int32) % tq)
    cos_s, sin_s = _rope_tables(pos_s)
    xs = x_sample.reshape(db * tq, d)
    q_s, qr_s, cmp_s, slc_s, win_s, gates_s, hs = _inproj(xs, row(norm_mix[0]), w_all, cos_s, sin_s, SAMPLE_ROW_TILE)
    ncs = past // CMP_STRIDE
    kcvc_s = _compress_paged(_feature_major(cache_cmp, n_pool),
                             page_table.reshape(db // CMP_SEQS_PER_STEP, CMP_SEQS_PER_STEP * n_pages), cw)
    kcvc_s = kcvc_s.reshape(db, ncs, 256)
    attn_s, nwin_t = _nsa_sample(q_s, qr_s, gates_s, kcvc_s, _feature_major(cache_slc, n_pool), slc_s,
                                 _feature_major(cache_win, db), win_s, page_table,
                                 _overlap_matrix(ncs, ncs - 1), _block_expand_matrix(past + KEY_PAD), tq)
    ohg_s, s_s = _hgrn(hs, state_hgrn[0], hg_lb, row(hg_norm[0]), db, tq, 1, tq)
    y_s = tail(xs, attn_s, ohg_s)

    kv5 = lambda a, nb, nt: a.reshape(1, nb, nt, 2, NSA_KV_HEADS, HEAD_DIM)
    kv5_t = lambda a_t, nb, nt: kv5(a_t.transpose(0, 2, 1), nb, nt)
    keep = min(WINDOW, t)
    return (y_p.reshape(b, t, d), y_s.reshape(db, tq, d),
            kv5_t(cmp_t, b, t), kv5_t(slc_t, b, t), kv5_t(win_t[:, :, t - keep:], b, keep), s_p[None],
            kv5(cmp_s, db, tq), kv5(slc_s, db, tq), kv5_t(nwin_t, db, wb), s_s[None])
```

```python
import functools

import numpy as np
import jax
import jax.numpy as jnp
from jax import lax
from jax.experimental import pallas as pl
from jax.experimental.pallas import tpu as pltpu

F32 = jnp.float32
BF16 = jnp.bfloat16

D_MODEL = 1024
HEAD_DIM = 64
NSA_HEADS = 8
NSA_KV_HEADS = 2
NSA_GROUP = NSA_HEADS // NSA_KV_HEADS
CMP_BLOCK = 32
CMP_STRIDE = 16
SLC_BLOCK = 64
SLC_TOPN = 16
WINDOW = 512
ROPE_THETA = 10000.0
HG_HEADS = 4
HG_DK = 128
HG_DV = 128
NSA_WIDTH = NSA_HEADS * HEAD_DIM
HG_WIDTH = HG_HEADS * HG_DV
KV_WIDTH = NSA_KV_HEADS * HEAD_DIM
PEER_HEADS = 8
PEER_KEYS = 128
PEER_QDIM = 256
PEER_TOPK = 16
RMS_EPS = 1e-6
NEG_INF = -1e30
FORCED_SCORE = 1e6
LOWEST = -3e38

LANES = 128
HALF = 64
VMEM_LIMIT = 56 * 1024 * 1024

COL_Q = 0
COL_KV = 512
COL_H = 1280
COL_G = 3328
PROJ_PAD = 3456


def _dot(a, b):
    return jnp.dot(a, b, preferred_element_type=F32)


def _dot_nt(a, b):
    return lax.dot_general(a, b, (((1,), (1,)), ((), ())), preferred_element_type=F32)


def _iota(shape, dim):
    return lax.broadcasted_iota(jnp.int32, shape, dim)


def _params(*sem):
    return pltpu.CompilerParams(dimension_semantics=sem, vmem_limit_bytes=VMEM_LIMIT)


def _rope_tile(x, cos, sin_signed, first_half):
    partner = jnp.where(first_half, pltpu.roll(x, LANES - 32, 1), pltpu.roll(x, 32, 1))
    return x * cos + partner * sin_signed


def _inproj_seq_kernel(x_ref, g_ref, w_ref, cos_ref, sin_ref,
                       q_ref, qr_ref, cmp_ref, cmpt_ref, slct_ref, wint_ref, ksb_ref, kwb_ref, vst_ref, vwt_ref,
                       gates_ref, h_ref):
    x = x_ref[...]
    ms = jnp.mean(x * x, axis=-1, keepdims=True)
    xn = (x * lax.rsqrt(ms + RMS_EPS)) * g_ref[...]
    proj = _dot(xn.astype(BF16), w_ref[...])
    cos = cos_ref[...]
    sin = sin_ref[...]
    first_half = (_iota(cos.shape, 1) & (HALF - 1)) < 32
    rope = lambda t: _rope_tile(t, cos, sin, first_half)
    q_ref[...] = proj[:, COL_Q:COL_Q + NSA_WIDTH]
    for c in range(NSA_WIDTH // LANES):
        qr_ref[:, c * LANES:(c + 1) * LANES] = rope(proj[:, COL_Q + c * LANES:COL_Q + (c + 1) * LANES])
    cmp = proj[:, COL_KV:COL_KV + 256]
    cmp_ref[...] = cmp
    cmpt_ref[...] = cmp.T
    ks = rope(proj[:, COL_KV + 256:COL_KV + 384])
    kw = rope(proj[:, COL_KV + 512:COL_KV + 640])
    vs_t = proj[:, COL_KV + 384:COL_KV + 512].T
    vw_t = proj[:, COL_KV + 640:COL_KV + 768].T
    slct_ref[0:LANES, :] = ks.T
    slct_ref[LANES:2 * LANES, :] = vs_t
    wint_ref[0:LANES, :] = kw.T
    wint_ref[LANES:2 * LANES, :] = vw_t
    ksb_ref[...] = ks.astype(BF16)
    kwb_ref[...] = kw.astype(BF16)
    for c in range(vst_ref.shape[0]):
        vst_ref[c] = vs_t[:, c * KEY_TILE:(c + 1) * KEY_TILE].astype(BF16)
    for c in range(vwt_ref.shape[0]):
        vwt_ref[c] = vw_t[:, c * LANES:(c + 1) * LANES].astype(BF16)
    gates_ref[...] = jax.nn.sigmoid(proj[:, COL_G:COL_G + LANES])
    h_ref[...] = proj[:, COL_H:COL_H + 4 * HG_WIDTH]


def _inproj_seq(x, norm_g, w_all, cos_t, sin_t, nb, t, tm):
    n = nb * t
    steps = t // tm
    row = lambda w: pl.BlockSpec((tm, w), lambda i: (i, 0))
    tab = pl.BlockSpec((tm, LANES), lambda i: (i % steps, 0))
    feat = pl.BlockSpec((None, 256, tm), lambda i: (i // steps, 0, i % steps))
    tiles = lambda w: pl.BlockSpec((tm // w, LANES, w), lambda i: (i, 0, 0))
    f32 = lambda *s: jax.ShapeDtypeStruct(s, F32)
    bf = lambda *s: jax.ShapeDtypeStruct(s, BF16)
    return pl.pallas_call(
        _inproj_seq_kernel,
        grid=(n // tm,),
        in_specs=[row(D_MODEL), pl.BlockSpec((1, D_MODEL), lambda i: (0, 0)),
                  pl.BlockSpec((D_MODEL, PROJ_PAD), lambda i: (0, 0)), tab, tab],
        out_specs=[row(NSA_WIDTH), row(NSA_WIDTH), row(256), feat, feat, feat, row(LANES), row(LANES),
                   tiles(KEY_TILE), tiles(LANES), row(LANES), row(4 * HG_WIDTH)],
        out_shape=[f32(n, NSA_WIDTH), f32(n, NSA_WIDTH), f32(n, 256), f32(nb, 256, t), f32(nb, 256, t),
                   f32(nb, 256, t), bf(n, LANES), bf(n, LANES), bf(n // KEY_TILE, LANES, KEY_TILE),
                   bf(n // LANES, LANES, LANES), f32(n, LANES), f32(n, 4 * HG_WIDTH)],
        compiler_params=_params("parallel"),
        name="inproj_seq",
    )(x, norm_g, w_all, cos_t, sin_t)


def _inproj_kernel(x_ref, g_ref, w_ref, cos_ref, sin_ref,
                   q_ref, qr_ref, cmp_ref, slc_ref, win_ref, gates_ref, h_ref):
    x = x_ref[...]
    ms = jnp.mean(x * x, axis=-1, keepdims=True)
    xn = (x * lax.rsqrt(ms + RMS_EPS)) * g_ref[...]
    proj = _dot(xn.astype(BF16), w_ref[...])
    cos = cos_ref[...]
    sin = sin_ref[...]
    first_half = (_iota(cos.shape, 1) & (HALF - 1)) < 32
    rope = lambda t: _rope_tile(t, cos, sin, first_half)
    q_ref[...] = proj[:, COL_Q:COL_Q + NSA_WIDTH]
    for c in range(NSA_WIDTH // LANES):
        qr_ref[:, c * LANES:(c + 1) * LANES] = rope(proj[:, COL_Q + c * LANES:COL_Q + (c + 1) * LANES])
    cmp_ref[...] = proj[:, COL_KV:COL_KV + 256]
    ks = rope(proj[:, COL_KV + 256:COL_KV + 384])
    vs = proj[:, COL_KV + 384:COL_KV + 512]
    kw = rope(proj[:, COL_KV + 512:COL_KV + 640])
    vw = proj[:, COL_KV + 640:COL_KV + 768]
    slc_ref[:, 0:LANES] = ks
    slc_ref[:, LANES:2 * LANES] = vs
    win_ref[:, 0:LANES] = kw
    win_ref[:, LANES:2 * LANES] = vw
    gates_ref[...] = jax.nn.sigmoid(proj[:, COL_G:COL_G + LANES])
    h_ref[...] = proj[:, COL_H:COL_H + 4 * HG_WIDTH]


def _inproj(x, norm_g, w_all, cos_t, sin_t, tm):
    n = x.shape[0]
    ntab = cos_t.shape[0] // tm
    row = lambda w: pl.BlockSpec((tm, w), lambda i: (i, 0))
    tab = pl.BlockSpec((tm, LANES), lambda i: (i % ntab, 0))
    out_shapes = [
        jax.ShapeDtypeStruct((n, NSA_WIDTH), F32), jax.ShapeDtypeStruct((n, NSA_WIDTH), F32),
        jax.ShapeDtypeStruct((n, 256), F32), jax.ShapeDtypeStruct((n, 256), F32),
        jax.ShapeDtypeStruct((n, 256), F32), jax.ShapeDtypeStruct((n, LANES), F32),
        jax.ShapeDtypeStruct((n, 4 * HG_WIDTH), F32),
    ]
    return pl.pallas_call(
        _inproj_kernel,
        grid=(n // tm,),
        in_specs=[row(D_MODEL), pl.BlockSpec((1, D_MODEL), lambda i: (0, 0)),
                  pl.BlockSpec((D_MODEL, PROJ_PAD), lambda i: (0, 0)), tab, tab],
        out_specs=[row(NSA_WIDTH), row(NSA_WIDTH), row(256), row(256), row(256), row(LANES), row(4 * HG_WIDTH)],
        out_shape=out_shapes,
        compiler_params=_params("parallel"),
        name="inproj",
    )(x, norm_g, w_all, cos_t, sin_t)


def _rope_tables(pos):
    half = HEAD_DIM // 2
    inv = ROPE_THETA ** (-jnp.arange(half, dtype=F32) / half)
    ang = pos.astype(F32)[:, None] * inv[None, :]
    cos = jnp.tile(jnp.cos(ang), (1, 4))
    sin = jnp.sin(ang)
    return cos, jnp.tile(jnp.concatenate([-sin, sin], axis=1), (1, 2))


def _compress_rows(rows_ref, n_out, w_ref, pe_ref, pitch=CMP_STRIDE):
    a = jnp.zeros((n_out, LANES), F32)
    b = jnp.zeros((n_out, LANES), F32)
    for j in range(CMP_STRIDE):
        xj = rows_ref[pl.ds(j, n_out, stride=pitch), :]
        a = a + _dot((xj + pe_ref[j:j + 1, :]).astype(BF16), w_ref[j])
        b = b + _dot((xj + pe_ref[CMP_STRIDE + j:CMP_STRIDE + j + 1, :]).astype(BF16), w_ref[CMP_STRIDE + j])
    return a + pltpu.roll(b, n_out - 1, 0)


def _compress_prompt_kernel(rk_ref, rv_ref, wk_ref, wv_ref, pek_ref, pev_ref, out_ref):
    n_out = out_ref.shape[0]
    out_ref[:, 0:LANES] = _compress_rows(rk_ref, n_out, wk_ref, pek_ref).astype(BF16)
    out_ref[:, LANES:2 * LANES] = _compress_rows(rv_ref, n_out, wv_ref, pev_ref).astype(BF16)


def _compress_prompt(rows, cw):
    b, t, _ = rows.shape
    n_out = t // CMP_STRIDE
    wspec = pl.BlockSpec((CMP_BLOCK, LANES, LANES), lambda i: (0, 0, 0))
    pspec = pl.BlockSpec((CMP_BLOCK, LANES), lambda i: (0, 0))
    return pl.pallas_call(
        _compress_prompt_kernel,
        grid=(b,),
        in_specs=[pl.BlockSpec((None, t, LANES), lambda i: (i, 0, 0)),
                  pl.BlockSpec((None, t, LANES), lambda i: (i, 0, 1)), wspec, wspec, pspec, pspec],
        out_specs=pl.BlockSpec((None, n_out, 256), lambda i: (i, 0, 0)),
        out_shape=jax.ShapeDtypeStruct((b, n_out, 256), BF16),
        compiler_params=_params("parallel"),
        name="compress_prompt",
    )(rows, rows, *cw)


def _stack_heads(ref, lo_half, nq):
    hi_half = jnp.logical_not(lo_half)
    parts = []
    for h in range(NSA_HEADS):
        g = h // NSA_GROUP
        tile = ref[:, LANES * (h // 2):LANES * (h // 2 + 1)]
        if h % 2 != g:
            tile = pltpu.roll(tile, HALF, 1)
        parts.append(jnp.where(lo_half if g == 0 else hi_half, tile, 0.0))
    return (jnp.concatenate(parts, axis=0) * (HEAD_DIM ** -0.5)).astype(BF16)


def _per_head(x, nq):
    return jnp.concatenate([x[0:nq]] * NSA_GROUP + [x[nq:2 * nq]] * NSA_GROUP, axis=0)


def _masked_softmax_rows(s, valid):
    sm = jnp.where(valid, s, NEG_INF)
    m = jnp.max(sm, axis=-1, keepdims=True)
    e = jnp.where(valid, jnp.exp(sm - m), 0.0)
    l = jnp.sum(e, axis=-1, keepdims=True)
    return e / jnp.where(l > 0.0, l, 1.0)


def _select_blocks(imp, pos_col):
    blk = _iota(imp.shape, 1)
    cur = pos_col >> 6
    forced = (blk == 0) | (blk == cur) | (blk == cur - 1)
    causal = (blk << 6) <= pos_col
    score = jnp.where(causal, jnp.where(forced, FORCED_SCORE, imp), -FORCED_SCORE)
    return score, causal


def _topk_mask_t(score_t, n_sel):
    bi = _iota(score_t.shape, 0)
    sel = jnp.zeros(score_t.shape, F32)
    sc = score_t
    for _ in range(n_sel):
        m = jnp.max(sc, axis=0, keepdims=True)
        idx = jnp.min(jnp.where(sc == m, bi, LANES), axis=0, keepdims=True)
        hit = bi == idx
        sel = jnp.where(hit, 1.0, sel)
        sc = jnp.where(hit, LOWEST, sc)
    return sel


def _cmp_branch(qn, kc, vc, ov, pos_rows, pos_grp, nq):
    s = _dot_nt(qn, kc)
    valid = (_iota(s.shape, 1) * CMP_STRIDE + (CMP_BLOCK - 1)) <= pos_rows
    p = _masked_softmax_rows(s, valid)
    o_c = _dot(p.astype(BF16), vc)
    slab = lambda h: p[h * nq:(h + 1) * nq]
    psum = jnp.concatenate([slab(0) + slab(1) + slab(2) + slab(3), slab(4) + slab(5) + slab(6) + slab(7)], axis=0)
    hi = psum.astype(BF16)
    lo = (psum - hi.astype(F32)).astype(BF16)
    imp = _dot(hi, ov) + _dot(lo, ov)
    score, causal = _select_blocks(imp, pos_grp)
    return o_c, score, causal


def _combine_heads(out_ref, gates, lo_half, o_c, o_s, o_w, nq):
    combs = []
    for h in range(NSA_HEADS):
        rows = slice(h * nq, (h + 1) * nq)
        comb = (gates[:, h:h + 1] * o_c[rows] + gates[:, NSA_HEADS + h:NSA_HEADS + h + 1] * o_s[rows]
                + gates[:, 2 * NSA_HEADS + h:2 * NSA_HEADS + h + 1] * o_w[rows])
        if h % 2 != h // NSA_GROUP:
            comb = pltpu.roll(comb, HALF, 1)
        combs.append(comb)
    for r in range(NSA_HEADS // 2):
        out_ref[:, LANES * r:LANES * (r + 1)] = jnp.where(lo_half, combs[2 * r], combs[2 * r + 1])


KEY_TILE = 512


def _masked_softmax_cols(s, valid):
    sm = jnp.where(valid, s, NEG_INF)
    m = jnp.max(sm, axis=0, keepdims=True)
    e = jnp.where(valid, jnp.exp(sm - m), 0.0)
    l = jnp.sum(e, axis=0, keepdims=True)
    return e / jnp.where(l > 0.0, l, 1.0)


def _nsa_prompt_kernel(q_ref, qr_ref, gates_ref, cmp_ref, vct_ref, ks_ref, vst_ref, kw_ref, vwt_ref, ovt_ref, et_ref,
                       out_ref, *, tq):
    t0 = pl.program_id(1) * tq
    cols = NSA_HEADS * tq
    grp = NSA_KV_HEADS * tq
    pos_cols = t0 + (_iota((1, cols), 1) & (tq - 1))
    pos_grp = t0 + (_iota((1, grp), 1) & (tq - 1))
    n_tiles = (t0 + tq + KEY_TILE - 1) // KEY_TILE
    w_chunk = jnp.maximum(t0 - WINDOW, 0) // LANES
    w_start = pl.multiple_of(w_chunk * LANES, LANES)
    w_len = WINDOW + tq
    per_head = lambda x: jnp.concatenate([x[:, 0:tq]] * NSA_GROUP + [x[:, tq:grp]] * NSA_GROUP, axis=1)

    lo_half = _iota((tq, LANES), 1) < HALF
    qn = _stack_heads(q_ref, lo_half, tq)
    qr = _stack_heads(qr_ref, lo_half, tq)

    all_heads = lambda x: jnp.concatenate([x] * NSA_HEADS, axis=1)
    pos_tok = t0 + _iota((1, tq), 1)

    sc = _dot_nt(cmp_ref[:, 0:LANES], qn)
    n_i = _iota((sc.shape[0], tq), 0)
    sc = sc + all_heads(jnp.where(n_i * CMP_STRIDE + (CMP_BLOCK - 1) <= pos_tok, 0.0, NEG_INF))
    ec = jnp.exp(sc - jnp.max(sc, axis=0, keepdims=True))
    norm_c = jnp.where(pos_cols >= CMP_BLOCK - 1, 1.0 / jnp.sum(ec, axis=0, keepdims=True), 0.0)
    o_c = _dot(vct_ref[...], ec.astype(BF16)) * norm_c
    pc = ec * norm_c
    slab = lambda h: pc[:, h * tq:(h + 1) * tq]
    psum = jnp.concatenate([slab(0) + slab(1) + slab(2) + slab(3), slab(4) + slab(5) + slab(6) + slab(7)], axis=1)
    hi = psum.astype(BF16)
    lo = (psum - hi.astype(F32)).astype(BF16)
    ovt = ovt_ref[...]
    imp = _dot(ovt, hi) + _dot(ovt, lo)
    blk = _iota(imp.shape, 0)
    cur = pos_grp >> 6
    forced = (blk == 0) | (blk == cur) | (blk == cur - 1)
    causal = (blk << 6) <= pos_grp
    score = jnp.where(causal, jnp.where(forced, FORCED_SCORE, imp), -FORCED_SCORE)
    sel = jnp.where(causal, _topk_mask_t(score, SLC_TOPN), 0.0)
    notsel = (1.0 - sel).astype(BF16)

    sw = _dot_nt(kw_ref[pl.ds(w_start, w_len), :], qr)
    kp = w_start + _iota((w_len, tq), 0)
    sw = sw + all_heads(jnp.where((kp <= pos_tok) & (kp >= pos_tok - WINDOW), 0.0, NEG_INF))
    ew = jnp.exp(sw - jnp.max(sw, axis=0, keepdims=True))
    vwt = jnp.concatenate([vwt_ref[w_chunk + c] for c in range(w_len // LANES)], axis=1)
    o_w = _dot(vwt, ew.astype(BF16)) / jnp.sum(ew, axis=0, keepdims=True)

    def tile_step(kt, carry, diag):
        m, l, acc = carry
        k0 = pl.multiple_of(kt * KEY_TILE, KEY_TILE)
        s = _dot_nt(ks_ref[pl.ds(k0, KEY_TILE), :], qr)
        s = s + per_head(_dot(et_ref[pl.ds(k0, KEY_TILE), :], notsel) * NEG_INF)
        if diag:
            s = s + all_heads(jnp.where(k0 + _iota((KEY_TILE, tq), 0) <= pos_tok, 0.0, NEG_INF))
        m_new = jnp.maximum(m, jnp.max(s, axis=0, keepdims=True))
        alpha = jnp.exp(m - m_new)
        p = jnp.exp(s - m_new)
        l = alpha * l + jnp.sum(p, axis=0, keepdims=True)
        acc = alpha * acc + _dot(vst_ref[kt], p.astype(BF16))
        return m_new, l, acc

    init = (jnp.full((1, cols), NEG_INF, F32), jnp.zeros((1, cols), F32), jnp.zeros((LANES, cols), F32))
    carry = lax.fori_loop(0, n_tiles - 1, lambda kt, c: tile_step(kt, c, False), init)
    _, l, acc = tile_step(n_tiles - 1, carry, True)
    o_s = acc / l

    gt = gates_ref[...].T
    for r in range(NSA_HEADS // 2):
        halves = []
        for h in (2 * r, 2 * r + 1):
            c = slice(h * tq, (h + 1) * tq)
            comb = (gt[h:h + 1] * o_c[:, c] + gt[NSA_HEADS + h:NSA_HEADS + h + 1] * o_s[:, c]
                    + gt[2 * NSA_HEADS + h:2 * NSA_HEADS + h + 1] * o_w[:, c])
            g = h // NSA_GROUP
            halves.append(comb[HALF * g:HALF * (g + 1)])
        out_ref[:, LANES * r:LANES * (r + 1)] = jnp.concatenate(halves, axis=0).T


def _nsa_prompt(q, qr, gates, kcvc, ksb, vst, kwb, vwt, ovt, et, tq):
    b, t, _ = q.shape
    ncp = kcvc.shape[1]
    vct = kcvc[:, :, LANES:].transpose(0, 2, 1)
    blk = lambda w: pl.BlockSpec((None, tq, w), lambda bi, i: (bi, i, 0))
    full = lambda r, w: pl.BlockSpec((None, r, w), lambda bi, i: (bi, 0, 0))
    full4 = lambda a: pl.BlockSpec((None,) + a.shape[1:], lambda bi, i: (bi, 0, 0, 0))
    return pl.pallas_call(
        functools.partial(_nsa_prompt_kernel, tq=tq),
        grid=(b, t // tq),
        in_specs=[blk(NSA_WIDTH), blk(NSA_WIDTH), blk(LANES), full(ncp, 256), full(LANES, ncp), full(t, LANES),
                  full4(vst), full(t, LANES), full4(vwt),
                  pl.BlockSpec((LANES, ncp), lambda bi, i: (0, 0)),
                  pl.BlockSpec((t, LANES), lambda bi, i: (0, 0))],
        out_specs=blk(NSA_WIDTH),
        out_shape=jax.ShapeDtypeStruct((b, t, NSA_WIDTH), F32),
        compiler_params=_params("parallel", "arbitrary"),
        name="nsa_prompt",
    )(q, qr, gates, kcvc, vct, ksb, vst, kwb, vwt, ovt, et)


def _overlap_matrix(n_cmp_pad, n_cmp):
    c0 = jnp.arange(n_cmp_pad, dtype=jnp.int32)[:, None] * CMP_STRIDE
    s0 = jnp.arange(LANES, dtype=jnp.int32)[None, :] * SLC_BLOCK
    real = jnp.arange(n_cmp_pad, dtype=jnp.int32)[:, None] < n_cmp
    return ((c0 < s0 + SLC_BLOCK) & (c0 + CMP_BLOCK > s0) & real).astype(BF16)


def _block_expand_matrix(n_keys):
    r = jnp.arange(n_keys, dtype=jnp.int32)[:, None] // SLC_BLOCK
    return (r == jnp.arange(LANES, dtype=jnp.int32)[None, :]).astype(BF16)


PAGE_ROWS = 128
CMP_SEQS_PER_STEP = 4
GROUP_PITCH = 24


def _compress_paged_kernel(pt_ref, *refs, n_pages):
    pages = refs[:n_pages]
    wk_ref, wv_ref, pek_ref, pev_ref, out_ref, rk_scr, rv_scr = refs[n_pages:]
    groups = PAGE_ROWS // CMP_STRIDE
    for p in range(n_pages):
        for scr, half in ((rk_scr, 0), (rv_scr, 1)):
            rows = pages[p][half * LANES:(half + 1) * LANES, :].T
            for m in range(groups):
                r0 = (p * groups + m) * GROUP_PITCH
                scr[r0:r0 + CMP_STRIDE, :] = rows[m * CMP_STRIDE:(m + 1) * CMP_STRIDE]
    n_out = out_ref.shape[0]
    out_ref[:, 0:LANES] = _compress_rows(rk_scr, n_out, wk_ref, pek_ref, GROUP_PITCH).astype(BF16)
    out_ref[:, LANES:2 * LANES] = _compress_rows(rv_scr, n_out, wv_ref, pev_ref, GROUP_PITCH).astype(BF16)


def _page_specs(n_pages):
    return [pl.BlockSpec((None, 256, PAGE_ROWS), functools.partial(lambda i, pt, p: (pt[i * n_pages + p], 0, 0), p=p))
            for p in range(n_pages)]


def _feature_major(cache, lead):
    return cache[0].reshape(lead, cache.shape[2], 256).transpose(0, 2, 1)


def _compress_paged(cache, page_table, cw):
    db, n_pages = page_table.shape
    past = n_pages * PAGE_ROWS
    n_out = past // CMP_STRIDE
    wspec = pl.BlockSpec((CMP_BLOCK, LANES, LANES), lambda i, pt: (0, 0, 0))
    pspec = pl.BlockSpec((CMP_BLOCK, LANES), lambda i, pt: (0, 0))
    grid_spec = pltpu.PrefetchScalarGridSpec(
        num_scalar_prefetch=1, grid=(db,),
        in_specs=_page_specs(n_pages) + [wspec, wspec, pspec, pspec],
        out_specs=pl.BlockSpec((None, n_out, 256), lambda i, pt: (i, 0, 0)),
        scratch_shapes=[pltpu.VMEM((n_out * GROUP_PITCH, LANES), F32), pltpu.VMEM((n_out * GROUP_PITCH, LANES), F32)])
    return pl.pallas_call(
        functools.partial(_compress_paged_kernel, n_pages=n_pages),
        grid_spec=grid_spec,
        out_shape=jax.ShapeDtypeStruct((db, n_out, 256), BF16),
        compiler_params=_params("arbitrary"),
        name="compress_paged",
    )(page_table.reshape(-1), *([cache] * n_pages), *cw)


KEY_PAD = LANES
NSA_SEQS_PER_STEP = 2


def _nsa_sample_kernel(pt_ref, q_ref, qr_ref, gates_ref, cmp_ref, *refs, n_pages):
    n_seq = cmp_ref.shape[0]
    pages = refs[:n_seq * n_pages]
    snew_ref, cwin_ref, wnew_ref, ov_ref, et_ref, out_ref, nwin_ref = refs[n_seq * n_pages:]
    nq = q_ref.shape[0] // n_seq
    past = n_pages * PAGE_ROWS
    wb = cwin_ref.shape[2]
    rows = NSA_HEADS * nq
    n_grp = NSA_KV_HEADS * nq
    pad_rows = lambda x: jnp.concatenate([x, jnp.zeros((KEY_PAD - nq, x.shape[1]), F32)], axis=0)
    lo_half = _iota((nq, LANES), 1) < HALF
    pos_rows = past + (_iota((rows, 1), 0) & (nq - 1))
    pos_grp = past + (_iota((n_grp, 1), 0) & (nq - 1))
    seq_rows = lambda ref, b: ref.at[b * nq:(b + 1) * nq, :]

    qrs, o_cs, scores = [], [], []
    for b in range(n_seq):
        qn = _stack_heads(seq_rows(q_ref, b), lo_half, nq)
        qrs.append(_stack_heads(seq_rows(qr_ref, b), lo_half, nq))
        o_c, score, causal = _cmp_branch(qn, cmp_ref[b, :, 0:LANES], cmp_ref[b, :, LANES:2 * LANES], ov_ref[...],
                                         pos_rows, pos_grp, nq)
        o_cs.append(o_c)
        scores.append(score)
    score_sq = jnp.concatenate(scores + [jnp.full((LANES - n_seq * n_grp, LANES), LOWEST, F32)], axis=0)
    sel_all = _topk_mask_t(score_sq.T, SLC_TOPN).T

    for b in range(n_seq):
        pg = pages[b * n_pages:(b + 1) * n_pages]
        qr = qrs[b]
        snew = pad_rows(snew_ref[b * nq:(b + 1) * nq, :])
        wnew = pad_rows(wnew_ref[b * nq:(b + 1) * nq, :])

        shifted = pltpu.roll(cwin_ref[b], wb - nq, 1)
        new_t = jnp.concatenate([wnew[:, 0:LANES].T, wnew[:, LANES:2 * LANES].T], axis=0)
        tail = jnp.where(_iota((256, LANES), 1) >= LANES - nq, pltpu.roll(new_t, LANES - nq, 1),
                         shifted[:, wb - LANES:wb])
        nwin_ref[b, :, 0:wb - LANES] = shifted[:, 0:wb - LANES]
        nwin_ref[b, :, wb - LANES:wb] = tail

        sel = jnp.where(causal, sel_all[b * n_grp:(b + 1) * n_grp], 0.0)
        notsel = (1.0 - sel).astype(BF16)
        s = jnp.concatenate([_dot(qr, pg[p][0:LANES, :].astype(BF16)) for p in range(n_pages)]
                            + [_dot_nt(qr, snew[:, 0:LANES].astype(BF16))], axis=1)
        s = s + _per_head(_dot_nt(notsel, et_ref[...]) * NEG_INF, nq)
        s = jnp.where(_iota(s.shape, 1) <= pos_rows, s, NEG_INF)
        e = jnp.exp(s - jnp.max(s, axis=-1, keepdims=True))
        eb = e.astype(BF16)
        o_s = _dot(eb[:, past:past + KEY_PAD], snew[:, LANES:2 * LANES].astype(BF16))
        for p in range(n_pages):
            o_s = o_s + _dot_nt(eb[:, p * PAGE_ROWS:(p + 1) * PAGE_ROWS], pg[p][LANES:2 * LANES, :].astype(BF16))
        o_s = o_s / jnp.sum(e, axis=-1, keepdims=True)

        sw = jnp.concatenate([_dot(qr, cwin_ref[b, 0:LANES, :].astype(BF16)),
                              _dot_nt(qr, wnew[:, 0:LANES].astype(BF16))], axis=1)
        kp = (past - wb) + _iota(sw.shape, 1)
        pw = _masked_softmax_rows(sw, (kp <= pos_rows) & (kp >= pos_rows - WINDOW) & (kp >= 0)).astype(BF16)
        o_w = (_dot_nt(pw[:, 0:wb], cwin_ref[b, LANES:2 * LANES, :].astype(BF16))
               + _dot(pw[:, wb:wb + KEY_PAD], wnew[:, LANES:2 * LANES].astype(BF16)))
        _combine_heads(seq_rows(out_ref, b), seq_rows(gates_ref, b)[...], lo_half, o_cs[b], o_s, o_w, nq)


def _nsa_sample(q, qr, gates, kcvc, cache_slc, slc_new, cache_win, win_new, page_table, ov, et, nq):
    db, n_pages = page_table.shape
    past = n_pages * PAGE_ROWS
    wb = cache_win.shape[2]
    ncp = kcvc.shape[1]
    n_seq = NSA_SEQS_PER_STEP
    blk = lambda w: pl.BlockSpec((n_seq * nq, w), lambda i, pt: (i, 0))
    win_spec = pl.BlockSpec((n_seq, 256, wb), lambda i, pt: (i, 0, 0))
    grid_spec = pltpu.PrefetchScalarGridSpec(
        num_scalar_prefetch=1, grid=(db // n_seq,),
        in_specs=[blk(NSA_WIDTH), blk(NSA_WIDTH), blk(LANES),
                  pl.BlockSpec((n_seq, ncp, 256), lambda i, pt: (i, 0, 0))] + _page_specs(n_seq * n_pages) + [
                  blk(256), win_spec, blk(256),
                  pl.BlockSpec((ncp, LANES), lambda i, pt: (0, 0)),
                  pl.BlockSpec((past + KEY_PAD, LANES), lambda i, pt: (0, 0))],
        out_specs=[blk(NSA_WIDTH), win_spec])
    return pl.pallas_call(
        functools.partial(_nsa_sample_kernel, n_pages=n_pages),
        grid_spec=grid_spec,
        out_shape=[jax.ShapeDtypeStruct((db * nq, NSA_WIDTH), F32), jax.ShapeDtypeStruct((db, 256, wb), F32)],
        compiler_params=_params("arbitrary"),
        name="nsa_sample",
    )(page_table.reshape(-1), q, qr, gates, kcvc, *([cache_slc] * (n_seq * n_pages)), slc_new, cache_win, win_new, ov, et)


HG_ROWS = 128
HG_SUB = 16


def _dot_split3(m_bf16, x):
    a = x.astype(BF16)
    r = x - a.astype(F32)
    b = r.astype(BF16)
    c = (r - b.astype(F32)).astype(BF16)
    return _dot(m_bf16, a) + _dot(m_bf16, b) + _dot(m_bf16, c)


def _hgrn_chunk(q, fpre, v, gpre, lb, ng, s0, n_real):
    f = lb + (1.0 - lb) * jax.nn.sigmoid(fpre)
    logf = jnp.log(f)
    kk = 1.0 - f
    if n_real < HG_ROWS:
        pad = lambda a: jnp.concatenate([a, jnp.zeros((HG_ROWS - n_real, LANES), F32)], axis=0)
        q, logf, kk, v = pad(q), pad(logf), pad(kk), pad(v)
    r_i = _iota((HG_ROWS, HG_ROWS), 0)
    c_i = _iota((HG_ROWS, HG_ROWS), 1)
    tri_b = r_i >= c_i
    tri = jnp.where(tri_b, 1.0, 0.0).astype(BF16)
    cum = _dot_split3(tri, logf)
    cprev = cum - logf
    cum_last = cum[HG_ROWS - 1:HG_ROWS, :]
    s0b = s0.astype(BF16)
    o = _dot((q * jnp.exp(cum)).astype(BF16), s0b)
    row = _iota((HG_ROWS, LANES), 0)
    a_rows = []
    n_sub = -(-n_real // HG_SUB)
    for i in range(n_sub):
        r0 = i * HG_SUB
        c_ref = cprev[r0:r0 + 1, :]
        qt = q[r0:r0 + HG_SUB] * jnp.exp(cum[r0:r0 + HG_SUB] - c_ref)
        kt = kk * jnp.exp(jnp.where(row < r0 + HG_SUB, c_ref - cum, NEG_INF))
        a_rows.append(_dot_nt(qt.astype(BF16), kt.astype(BF16)))
    if n_sub * HG_SUB < HG_ROWS:
        a_rows.append(jnp.zeros((HG_ROWS - n_sub * HG_SUB, HG_ROWS), F32))
    a = jnp.where(tri_b, jnp.concatenate(a_rows, axis=0), 0.0)
    vb = v.astype(BF16)
    o = o + _dot(a.astype(BF16), vb)
    kdec = kk * jnp.exp(cum_last - cum)
    scale = jnp.broadcast_to(jnp.exp(cum_last), (HG_ROWS, LANES)).T
    s_new = scale * s0 + _dot(kdec.T.astype(BF16), vb)
    o = o[0:n_real]
    o = o * lax.rsqrt(jnp.mean(o * o, axis=-1, keepdims=True) + RMS_EPS) * ng
    return o * jax.nn.sigmoid(gpre), s_new


def _hgrn_kernel(q_ref, f_ref, v_ref, g_ref, s0_ref, lb_ref, ng_ref, o_ref, sfin_ref, s_scr, *, n_chunks, n_real):
    @pl.when(pl.program_id(1) == 0)
    def _init():
        s_scr[...] = s0_ref[...]

    lbp = lb_ref[...]
    e = jnp.exp(lbp - jnp.max(lbp, axis=0, keepdims=True))
    lb = e[0:1, :] / jnp.sum(e, axis=0, keepdims=True)
    ng = ng_ref[...]

    def chunk(c, carry):
        rows = slice(0, n_real) if n_chunks == 1 else pl.ds(pl.multiple_of(c * n_real, n_real), n_real)
        for hd in range(HG_HEADS):
            cols = slice(hd * LANES, (hd + 1) * LANES)
            o, s_new = _hgrn_chunk(q_ref[rows, cols], f_ref[rows, cols], v_ref[rows, cols], g_ref[rows, cols],
                                   lb[:, cols], ng[:, cols], s_scr[hd], n_real)
            o_ref[rows, cols] = o
            s_scr[hd] = s_new
        return carry

    if n_chunks == 1:
        chunk(0, 0)
    else:
        lax.fori_loop(0, n_chunks, chunk, 0)
    sfin_ref[...] = s_scr[...]


def _hgrn(h, s0, hg_lb, hg_norm, nb, t, n_chunks, n_real):
    rows = n_chunks * n_real
    steps = t // rows
    col = lambda sec: pl.BlockSpec((rows, HG_WIDTH), lambda b, j: (b * steps + j, sec))
    st = pl.BlockSpec((None, HG_HEADS, HG_DK, HG_DV), lambda b, j: (b, 0, 0, 0))
    return pl.pallas_call(
        functools.partial(_hgrn_kernel, n_chunks=n_chunks, n_real=n_real),
        grid=(nb, steps),
        in_specs=[col(0), col(1), col(2), col(3), st,
                  pl.BlockSpec(hg_lb.shape, lambda b, j: (0, 0)),
                  pl.BlockSpec((1, HG_WIDTH), lambda b, j: (0, 0))],
        out_specs=[pl.BlockSpec((rows, HG_WIDTH), lambda b, j: (b * steps + j, 0)), st],
        out_shape=[jax.ShapeDtypeStruct((nb * t, HG_WIDTH), F32),
                   jax.ShapeDtypeStruct((nb, HG_HEADS, HG_DK, HG_DV), F32)],
        scratch_shapes=[pltpu.VMEM((HG_HEADS, HG_DK, HG_DV), F32)],
        compiler_params=_params("parallel", "arbitrary"),
        name="hgrn",
    )(h, h, h, h, s0, hg_lb, hg_norm)


def _outproj_kernel(x_ref, a_ref, hg_ref, wo_ref, nf_ref, wq_ref, x1_ref, xn_ref, pq_ref):
    mix = jnp.concatenate([a_ref[...], hg_ref[...]], axis=1).astype(BF16)
    x1 = x_ref[...] + _dot(mix, wo_ref[...])
    x1_ref[...] = x1
    ms = jnp.mean(x1 * x1, axis=-1, keepdims=True)
    xb = ((x1 * lax.rsqrt(ms + RMS_EPS)) * nf_ref[...]).astype(BF16)
    xn_ref[...] = xb
    pq_ref[...] = _dot(xb, wq_ref[...])


def _outproj(x, attn, ohg, w_out, norm_ffn, wq, tm):
    n = x.shape[0]
    row = lambda w: pl.BlockSpec((tm, w), lambda i: (i, 0))
    full = lambda a: pl.BlockSpec(a.shape, lambda i: (0, 0))
    nq = wq.shape[1]
    return pl.pallas_call(
        _outproj_kernel,
        grid=(n // tm,),
        in_specs=[row(D_MODEL), row(NSA_WIDTH), row(HG_WIDTH), full(w_out), full(norm_ffn), full(wq)],
        out_specs=[row(D_MODEL), row(D_MODEL), row(nq)],
        out_shape=[jax.ShapeDtypeStruct((n, D_MODEL), F32), jax.ShapeDtypeStruct((n, D_MODEL), BF16),
                   jax.ShapeDtypeStruct((n, nq), F32)],
        compiler_params=_params("parallel"),
        name="outproj",
    )(x, attn, ohg, w_out, norm_ffn, wq)


def _top_rows(s, k, val_scr, idx_scr):
    bi = _iota(s.shape, 0)
    big = s.shape[0]
    for a in range(k):
        m = jnp.max(s, axis=0, keepdims=True)
        idx = jnp.min(jnp.where(s == m, bi, big), axis=0, keepdims=True)
        val_scr[a:a + 1, :] = m
        idx_scr[a:a + 1, :] = idx
        s = jnp.where(bi == idx, LOWEST, s)


def _peer_select_kernel(pq_ref, keys_ref, i_ref, j_ref, g_ref, v12, i12, sc, cd):
    tms = pq_ref.shape[0]
    half = PEER_QDIM // 2
    s1 = _dot_nt(keys_ref[0].astype(BF16), pq_ref[:, 0:half].astype(BF16))
    s2 = _dot_nt(keys_ref[1].astype(BF16), pq_ref[:, half:2 * half].astype(BF16))
    _top_rows(jnp.concatenate([s1, s2], axis=1), PEER_TOPK, v12, i12)
    a1 = v12[:, 0:tms]
    a2 = v12[:, tms:2 * tms]
    r16 = _iota((PEER_TOPK, tms), 0)
    r8 = _iota((8, tms), 0)
    parts = [a1[0:1] + a2]
    codes = [r16]
    for a in range(1, 8):
        parts.append(a1[a:a + 1] + a2[0:8])
        codes.append(r8 + PEER_TOPK * a)
    parts.append(a1[8:16] + a2[0:1])
    codes.append((r8 + 8) * PEER_TOPK)
    cand = jnp.concatenate(parts, axis=0)
    code = jnp.concatenate(codes, axis=0)
    for k in range(PEER_TOPK):
        m = jnp.max(cand, axis=0, keepdims=True)
        cs = jnp.min(jnp.where(cand == m, code, PEER_TOPK * PEER_TOPK), axis=0, keepdims=True)
        sc[k:k + 1, :] = m
        cd[k:k + 1, :] = cs
        cand = jnp.where(code == cs, LOWEST, cand)
    scv = sc[...]
    cdv = cd[...]
    ak = cdv >> 4
    bk = cdv & (PEER_TOPK - 1)
    idx1 = i12[:, 0:tms]
    idx2 = i12[:, tms:2 * tms]
    ik = jnp.zeros((PEER_TOPK, tms), jnp.int32)
    jk = jnp.zeros((PEER_TOPK, tms), jnp.int32)
    for a in range(PEER_TOPK):
        ik = jnp.where(ak == a, idx1[a:a + 1], ik)
        jk = jnp.where(bk == a, idx2[a:a + 1], jk)
    e = jnp.exp(scv - scv[0:1])
    i_ref[...] = ik.astype(F32)
    j_ref[...] = jk.astype(F32)
    g_ref[...] = e / jnp.sum(e, axis=0, keepdims=True)


def _peer_select(pq, keys, tms):
    n = pq.shape[0]
    out = pl.BlockSpec((PEER_TOPK, tms), lambda i, h: (h, i))
    shp = jax.ShapeDtypeStruct((PEER_HEADS * PEER_TOPK, n), F32)
    return pl.pallas_call(
        _peer_select_kernel,
        grid=(n // tms, PEER_HEADS),
        in_specs=[pl.BlockSpec((tms, PEER_QDIM), lambda i, h: (i, h)),
                  pl.BlockSpec((None, 2, PEER_KEYS, PEER_QDIM // 2), lambda i, h: (h, 0, 0, 0))],
        out_specs=[out, out, out],
        out_shape=[shp, shp, shp],
        scratch_shapes=[pltpu.VMEM((PEER_TOPK, 2 * tms), F32), pltpu.VMEM((PEER_TOPK, 2 * tms), jnp.int32),
                        pltpu.VMEM((PEER_TOPK, tms), F32), pltpu.VMEM((PEER_TOPK, tms), jnp.int32)],
        compiler_params=_params("parallel", "arbitrary"),
        name="peer_select",
    )(pq, keys)


W_PITCH = PEER_KEYS + 8
EXPERT_CHUNK = 256
EXPERT_GROUP = 512


def _peer_dense_kernel(xn_ref, ik_ref, jk_ref, gk_ref, u_ref, v_ref, x1_ref, nf_ref, out_ref, w_scr, *, tm, te):
    e_idx = pl.program_id(1)

    @pl.when(e_idx == 0)
    def _build():
        sub = _iota((PEER_KEYS, LANES), 0).astype(F32)

        def body(n, carry):
            irow = ik_ref[pl.ds(n, 1), :]
            jrow = jk_ref[pl.ds(n, 1), :]
            grow = gk_ref[pl.ds(n, 1), :]
            a = jnp.where(irow == sub, grow, 0.0).astype(BF16)
            bt = jnp.where(jrow == sub, 1.0, 0.0).astype(BF16)
            w_scr[pl.ds(pl.multiple_of(n * W_PITCH, 8), PEER_KEYS), :] = _dot_nt(a, bt)
            return carry

        lax.fori_loop(0, tm, body, 0, unroll=32)
        out_ref[...] = jnp.zeros(out_ref.shape, F32)

    def group(gi, carry):
        xn = xn_ref[...]
        acc = None
        for c in range(EXPERT_GROUP // EXPERT_CHUNK):
            r0 = pl.multiple_of(gi * EXPERT_GROUP + c * EXPERT_CHUNK, EXPERT_CHUNK)
            rows = pl.ds(r0, EXPERT_CHUNK)
            h = _dot_nt(xn, u_ref[rows, :])
            i0 = (e_idx * te + r0) // PEER_KEYS
            wt = jnp.concatenate(
                [w_scr[pl.ds(i0 + ii, tm, stride=W_PITCH), :] for ii in range(EXPERT_CHUNK // PEER_KEYS)], axis=1)
            d = _dot((jax.nn.gelu(h) * wt).astype(BF16), v_ref[rows, :])
            acc = d if acc is None else acc + d
        out_ref[...] += acc
        return carry

    lax.fori_loop(0, te // EXPERT_GROUP, group, 0)

    @pl.when(e_idx == pl.num_programs(1) - 1)
    def _finish():
        y = x1_ref[...] + out_ref[...]
        ms = jnp.mean(y * y, axis=-1, keepdims=True)
        out_ref[...] = (y * lax.rsqrt(ms + RMS_EPS)) * nf_ref[...]


def _peer_dense(xn, ik, jk, gk, u, v, x1, norm_final, tm, te):
    n = xn.shape[0]
    n_exp = u.shape[0]
    row = lambda w: pl.BlockSpec((tm, w), lambda i, e: (i, 0), pipeline_mode=pl.Buffered(1))
    exp_spec = pl.BlockSpec((te, D_MODEL), lambda i, e: (e, 0))
    return pl.pallas_call(
        functools.partial(_peer_dense_kernel, tm=tm, te=te),
        grid=(n // tm, n_exp // te),
        in_specs=[row(D_MODEL), row(LANES), row(LANES), row(LANES), exp_spec, exp_spec, row(D_MODEL),
                  pl.BlockSpec((1, D_MODEL), lambda i, e: (0, 0))],
        out_specs=pl.BlockSpec((tm, D_MODEL), lambda i, e: (i, 0)),
        out_shape=jax.ShapeDtypeStruct((n, D_MODEL), F32),
        scratch_shapes=[pltpu.VMEM((tm * W_PITCH, LANES), F32)],
        compiler_params=_params("parallel", "arbitrary"),
        name="peer_dense",
    )(xn, ik, jk, gk, u, v, x1, norm_final)


def _prep_w_in(w_in):
    w_main = jnp.concatenate([w_in[:, :COL_KV + 768], w_in[:, COL_KV + 768 + 3 * NSA_HEADS:]], axis=1)
    w_gate = jnp.pad(w_in[:, COL_KV + 768:COL_KV + 768 + 3 * NSA_HEADS], ((0, 0), (0, LANES - 3 * NSA_HEADS)))
    return jnp.concatenate([w_main, w_gate], axis=1).astype(BF16)


def _prep_cmp(cmp_wk, cmp_wv, cmp_pek, cmp_pev):
    def bd(w):
        z = jnp.zeros_like(w)
        return jnp.concatenate([jnp.concatenate([w, z], axis=2), jnp.concatenate([z, w], axis=2)], axis=1).astype(BF16)
    dup = lambda pe: jnp.concatenate([pe, pe], axis=1).astype(F32)
    return bd(cmp_wk), bd(cmp_wv), dup(cmp_pek), dup(cmp_pev)


ROW_TILE = 512
SAMPLE_ROW_TILE = 256
NSA_Q_BLOCK = 256
HG_CHUNKS_PER_STEP = 4
SELECT_TOKENS = 1024
PEER_TOKEN_TILE = 512
PEER_EXPERT_TILE = 1024


def _ffn_tail(x, attn, ohg, w_out, norm_ffn, wq, keys, u, v, norm_final):
    x1, xn, pq = _outproj(x, attn, ohg, w_out, norm_ffn, wq, ROW_TILE)
    ik, jk, gk = _peer_select(pq, keys, SELECT_TOKENS)
    return _peer_dense(xn, ik.T, jk.T, gk.T, u, v, x1, norm_final, PEER_TOKEN_TILE, PEER_EXPERT_TILE)


def kernel(x_prompt, x_sample, cache_cmp, cache_slc, cache_win, state_hgrn, page_table, norm_mix, w_in, cmp_wk, cmp_wv, cmp_pek, cmp_pev, hg_lb, hg_norm, w_out, norm_ffn, peer_wq, peer_keys, peer_u, peer_v, norm_final):
    b, t, d = x_prompt.shape
    db, tq, _ = x_sample.shape
    n_pool = cache_cmp.shape[1]
    n_pages = page_table.shape[1]
    past = n_pages * PAGE_ROWS
    wb = cache_win.shape[2]
    row = lambda a: a.reshape(1, -1)

    w_all = _prep_w_in(w_in[0])
    cw = _prep_cmp(cmp_wk[0], cmp_wv[0], cmp_pek[0], cmp_pev[0])
    w_out_b = w_out[0].astype(BF16)
    wq_b = peer_wq[0].astype(BF16)
    u_b = peer_u[0].astype(BF16)
    v_b = peer_v[0].astype(BF16)
    tail = lambda x, attn, ohg: _ffn_tail(x, attn, ohg, w_out_b, row(norm_ffn[0]), wq_b, peer_keys[0], u_b, v_b,
                                          row(norm_final))

    cos, sin = _rope_tables(jnp.arange(t, dtype=jnp.int32))
    xp = x_prompt.reshape(b * t, d)
    q, qr, cmp_p, cmp_t, slc_t, win_t, ksb, kwb, vst, vwt, gates, hp = _inproj_seq(
        xp, row(norm_mix[0]), w_all, cos, sin, b, t, ROW_TILE)
    kcvc = _compress_prompt(cmp_p.reshape(b, t, 256), cw)
    ncp = t // CMP_STRIDE
    attn_p = _nsa_prompt(q.reshape(b, t, -1), qr.reshape(b, t, -1), gates.reshape(b, t, -1), kcvc,
                         ksb.reshape(b, t, LANES), vst.reshape(b, t // KEY_TILE, LANES, KEY_TILE),
                         kwb.reshape(b, t, LANES), vwt.reshape(b, t // LANES, LANES, LANES),
                         _overlap_matrix(ncp, ncp - 1).T, _block_expand_matrix(t), NSA_Q_BLOCK)
    ohg_p, s_p = _hgrn(hp, jnp.zeros((b, HG_HEADS, HG_DK, HG_DV), F32), hg_lb, row(hg_norm[0]), b, t,
                       HG_CHUNKS_PER_STEP, HG_ROWS)
    y_p = tail(xp, attn_p.reshape(b * t, -1), ohg_p)

    pos_s = past + (jnp.arange(db * tq, dtype=jnp.int32) % tq)
    cos_s, sin_s = _rope_tables(pos_s)
    xs = x_sample.reshape(db * tq, d)
    q_s, qr_s, cmp_s, slc_s, win_s, gates_s, hs = _inproj(xs, row(norm_mix[0]), w_all, cos_s, sin_s, SAMPLE_ROW_TILE)
    ncs = past // CMP_STRIDE
    kcvc_s = _compress_paged(_feature_major(cache_cmp, n_pool),
                             page_table.reshape(db // CMP_SEQS_PER_STEP, CMP_SEQS_PER_STEP * n_pages), cw)
    kcvc_s = kcvc_s.reshape(db, ncs, 256)
    attn_s, nwin_t = _nsa_sample(q_s, qr_s, gates_s, kcvc_s, _feature_major(cache_slc, n_pool), slc_s,
                                 _feature_major(cache_win, db), win_s, page_table,
                                 _overlap_matrix(ncs, ncs - 1), _block_expand_matrix(past + KEY_PAD), tq)
    ohg_s, s_s = _hgrn(hs, state_hgrn[0], hg_lb, row(hg_norm[0]), db, tq, 1, tq)
    y_s = tail(xs, attn_s, ohg_s)

    kv5 = lambda a, nb, nt: a.reshape(1, nb, nt, 2, NSA_KV_HEADS, HEAD_DIM)
    kv5_t = lambda a_t, nb, nt: kv5(a_t.transpose(0, 2, 1), nb, nt)
    keep = min(WINDOW, t)
    return (y_p.reshape(b, t, d), y_s.reshape(db, tq, d),
            kv5_t(cmp_t, b, t), kv5_t(slc_t, b, t), kv5_t(win_t[:, :, t - keep:], b, keep), s_p[None],
            kv5(cmp_s, db, tq), kv5(slc_s, db, tq), kv5_t(nwin_t, db, wb), s_s[None])
```
